```python
import math
import jax
import jax.numpy as jnp
from jax import lax
import numpy as np

D_MODEL = 1024
BATCH = 8
SEQ = 2048
DEPTH = 4
DEC_BATCH = 128
DEC_SEQ = 8
PAST_LEN = 8192
PAGE_SIZE = 128

HEAD_DIM = 64
A_Q_HEADS = D_MODEL // HEAD_DIM
A_KV_HEADS = A_Q_HEADS // 4
A_GROUP = A_Q_HEADS // A_KV_HEADS
WINDOW = 128
R_HEADS = 8
R_DK = D_MODEL // (2 * R_HEADS)
R_DV = D_MODEL // R_HEADS
R_CHUNK = 128
ROPE_BASE = 10000.0
C_CHUNK = 128
C_GROUPS = 8
C_GROUP_WIDTH = D_MODEL // C_GROUPS
N_EXPERTS = 32
TOP_K = 4
D_FF = D_MODEL
SWIGLU_LIMIT = 7.0
SWIGLU_ALPHA = 1.702
MOE_BLOCK = 128
EPS = 1e-6
IN_SPLITS = (A_Q_HEADS * HEAD_DIM, A_KV_HEADS * HEAD_DIM, A_KV_HEADS * HEAD_DIM,
             R_HEADS * R_DK, R_HEADS * R_DK, R_HEADS * R_DV, R_HEADS * R_DV,
             D_MODEL, D_MODEL, 3 * D_MODEL)

kernel_name = 'hybrid_swa_retention_gmlp_moe_step'


def _split_points():
    return [int(s) for s in np.cumsum(IN_SPLITS)[:-1]]


def rmsnorm(x, g):
    xf = x.astype(jnp.float32)
    y = xf * lax.rsqrt(jnp.mean(xf * xf, axis=-1, keepdims=True) + EPS)
    return (y * g.astype(jnp.float32)).astype(x.dtype)


def layernorm(x, g, b):
    xf = x.astype(jnp.float32)
    mu = jnp.mean(xf, axis=-1, keepdims=True)
    var = jnp.mean(jnp.square(xf - mu), axis=-1, keepdims=True)
    y = (xf - mu) * lax.rsqrt(var + EPS)
    return (y * g.astype(jnp.float32) + b.astype(jnp.float32)).astype(x.dtype)


def head_norm(x, g):
    B, L, H, dv = x.shape
    mu = jnp.mean(x, axis=-1, keepdims=True)
    var = jnp.mean(jnp.square(x - mu), axis=-1, keepdims=True)
    y = ((x - mu) * lax.rsqrt(var + EPS)).reshape(B, L, H * dv)
    return y * g.astype(jnp.float32)


def rope(x, pos):
    d = x.shape[-1]
    inv_freq = ROPE_BASE ** (-jnp.arange(0, d, 2, dtype=jnp.float32) / d)
    ang = pos.astype(jnp.float32)[:, None] * inv_freq[None, :]
    cos = jnp.cos(ang)[None, :, None, :]
    sin = jnp.sin(ang)[None, :, None, :]
    xf = x.astype(jnp.float32)
    x1, x2 = xf[..., : d // 2], xf[..., d // 2:]
    return jnp.concatenate([x1 * cos - x2 * sin, x2 * cos + x1 * sin], axis=-1)


def sink_attention(q, k, v, mask, sinks):
    s = jnp.einsum('bnqhgd,bnkhd->bnhgqk', q.astype(jnp.float32), k.astype(jnp.float32)) * (HEAD_DIM ** -0.5)
    s = jnp.where(mask[None, :, None, None], s, -jnp.inf)
    sink = sinks.astype(jnp.float32)[None, None, :, :, None, None]
    m = jnp.maximum(jnp.max(s, axis=-1, keepdims=True), sink)
    p = jnp.exp(s - m)
    denom = jnp.sum(p, axis=-1, keepdims=True) + jnp.exp(sink - m)
    return jnp.einsum('bnhgqk,bnkhd->bnqhgd', (p / denom).astype(v.dtype), v)


def window_attention_prompt(q, k, v, sinks):
    B, L = q.shape[:2]
    nb = L // WINDOW
    qb = q.reshape(B, nb, WINDOW, A_KV_HEADS, A_GROUP, HEAD_DIM)

    def band(t):
        tb = t.reshape(B, nb, WINDOW, A_KV_HEADS, HEAD_DIM)
        prev = jnp.concatenate([jnp.zeros_like(tb[:, :1]), tb[:, :-1]], axis=1)
        return jnp.concatenate([prev, tb], axis=2)

    i = jnp.arange(WINDOW)[:, None]
    j = jnp.arange(2 * WINDOW)[None, :]
    rel = i + WINDOW - j
    in_band = (rel >= 0) & (rel < WINDOW)
    valid = (jnp.arange(nb)[:, None, None] > 0) | (j >= WINDOW)[None]
    mask = in_band[None] & valid
    o = sink_attention(qb, band(k), band(v), mask, sinks)
    return o.reshape(B, L, A_Q_HEADS * HEAD_DIM)


def window_attention_step(q, k, v, k_buf, v_buf, sinks):
    B, L = q.shape[:2]
    W = k_buf.shape[1]
    kk = jnp.concatenate([k_buf.astype(k.dtype), k], axis=1)[:, None]
    vv = jnp.concatenate([v_buf.astype(v.dtype), v], axis=1)[:, None]
    q_pos = PAST_LEN + jnp.arange(L)
    k_pos = jnp.concatenate([PAST_LEN - W + jnp.arange(W), q_pos])
    d = q_pos[:, None] - k_pos[None, :]
    mask = ((d >= 0) & (d < WINDOW))[None]
    o = sink_attention(q[:, None], kk, vv, mask, sinks)
    return o.reshape(B, L, A_Q_HEADS * HEAD_DIM)


def retention(q, k, v, s0):
    B, L, H, dk = q.shape
    dv = v.shape[-1]
    C = math.gcd(L, R_CHUNK)
    nc = L // C
    log_gamma = jnp.log1p(-jnp.exp2(-5.0 - jnp.arange(H, dtype=jnp.float32)))
    idx = jnp.arange(C, dtype=jnp.float32)
    diff = idx[:, None] - idx[None, :]
    intra = jnp.where(diff[None] >= 0, jnp.exp(jnp.maximum(diff, 0.0)[None] * log_gamma[:, None, None]), 0.0)
    q_decay = jnp.exp((idx + 1.0)[:, None] * log_gamma[None, :])
    k_decay = jnp.exp((C - 1.0 - idx)[:, None] * log_gamma[None, :])
    c_decay = jnp.exp(C * log_gamma)

    def to_chunks(t):
        return jnp.moveaxis(t.astype(jnp.float32).reshape(B, nc, C, H, t.shape[-1]), 1, 0)

    def step(S, xs):
        qc, kc, vc = xs
        a = jnp.einsum('bihd,bjhd->bhij', qc, kc) * intra
        o = jnp.einsum('bhij,bjhe->bihe', a, vc) + jnp.einsum('bihd,bhde->bihe', qc, S) * q_decay[None, :, :, None]
        S = S * c_decay[None, :, None, None] + jnp.einsum('bjhd,bjhe->bhde', kc * k_decay[None, :, :, None], vc)
        return S, o

    S, o = lax.scan(step, s0.astype(jnp.float32), (to_chunks(q), to_chunks(k), to_chunks(v)))
    o = jnp.moveaxis(o, 0, 1).reshape(B, L, H, dv)
    return o, S


def chunk_gmlp(u, v, ws, bs):
    B, L, D = v.shape
    nc = -(-L // C_CHUNK)
    pad = nc * C_CHUNK - L
    vp = jnp.pad(v, ((0, 0), (0, pad), (0, 0))).reshape(B, nc, C_CHUNK, C_GROUPS, C_GROUP_WIDTH)
    causal = jnp.tril(jnp.ones((C_CHUNK, C_CHUNK), dtype=bool))
    w = jnp.where(causal[None], ws, 0.0).astype(v.dtype)
    mixed = jnp.einsum('gts,bnsgc->bntgc', w, vp) + bs.T.astype(v.dtype)[None, None, :, :, None]
    mixed = mixed.reshape(B, nc * C_CHUNK, D)[:, :L]
    return u * mixed


def moe(x, router_w, router_b, w1, b1, w2, b2):
    T, D = x.shape
    logits = (x @ router_w).astype(jnp.float32) + router_b.astype(jnp.float32)
    top_v, top_i = lax.top_k(logits, TOP_K)
    gates = jax.nn.softmax(top_v, axis=-1)
    n_assign = T * TOP_K
    flat_e = top_i.reshape(-1).astype(jnp.int32)
    flat_t = jnp.arange(n_assign, dtype=jnp.int32) // TOP_K
    flat_g = gates.reshape(-1)
    order = jnp.argsort(flat_e)
    se, st, sg = flat_e[order], flat_t[order], flat_g[order]
    counts = jnp.bincount(flat_e, length=N_EXPERTS)
    padded = (counts + MOE_BLOCK - 1) // MOE_BLOCK * MOE_BLOCK
    pend = jnp.cumsum(padded)
    pstart = pend - padded
    ustart = jnp.cumsum(counts) - counts
    dest = pstart[se] + jnp.arange(n_assign, dtype=jnp.int32) - ustart[se]
    n_blocks = -(-n_assign // MOE_BLOCK) + N_EXPERTS
    n_rows = n_blocks * MOE_BLOCK
    tok_buf = jnp.full((n_rows,), T, dtype=jnp.int32).at[dest].set(st)
    gate_buf = jnp.zeros((n_rows,), jnp.float32).at[dest].set(sg)
    block_e = jnp.clip(jnp.searchsorted(pend, jnp.arange(n_blocks) * MOE_BLOCK, side='right'), 0, N_EXPERTS - 1)
    x_pad = jnp.concatenate([x, jnp.zeros((1, D), x.dtype)], axis=0)
    xb = x_pad[tok_buf].reshape(n_blocks, MOE_BLOCK, D)

    def expert_block(args):
        xblk, e = args
        hdn = xblk @ w1[e] + b1[e]
        g, up = hdn[:, :D_FF], hdn[:, D_FF:]
        g = jnp.minimum(g, SWIGLU_LIMIT)
        up = jnp.clip(up, -SWIGLU_LIMIT, SWIGLU_LIMIT)
        act = (up + 1.0) * (g * jax.nn.sigmoid(g * SWIGLU_ALPHA))
        return act @ w2[e] + b2[e]

    yb = lax.map(expert_block, (xb, block_e)).reshape(n_rows, D)
    y = jnp.zeros((T + 1, D), yb.dtype).at[tok_buf].add(yb * gate_buf[:, None].astype(yb.dtype))
    return y[:T]


def block(x, c, pos0, k_buf, v_buf, s0, ln_mix_g, ln_ffn_g, w_ada, b_ada, w_in, b_gate,
          q_norm_g, k_norm_g, attn_sinks, ret_norm_g, gm_ln_g, gm_ln_b, gm_ws, gm_bs, w_out,
          router_w, router_b, moe_w1, moe_b1, moe_w2, moe_b2):
    B, L, D = x.shape
    mod = jax.nn.silu(c) @ w_ada + b_ada
    sh1, sc1, g1, sh2, sc2, g2 = [m[:, None, :] for m in jnp.split(mod, 6, axis=-1)]
    h = rmsnorm(x, ln_mix_g) * (1.0 + sc1) + sh1
    z = h @ w_in
    aq, ak, av, rq, rk, rv, rg, cu, cv, mg = jnp.split(z, _split_points(), axis=-1)

    aq = rmsnorm(aq.reshape(B, L, A_KV_HEADS, A_GROUP, HEAD_DIM), q_norm_g)
    ak = rmsnorm(ak.reshape(B, L, A_KV_HEADS, HEAD_DIM), k_norm_g)
    av = av.reshape(B, L, A_KV_HEADS, HEAD_DIM)
    sinks = attn_sinks.reshape(A_KV_HEADS, A_GROUP)
    if k_buf is None:
        oa = window_attention_prompt(aq, ak, av, sinks)
        keep = max(L - WINDOW, 0)
        new_k, new_v = ak[:, keep:], av[:, keep:]
    else:
        oa = window_attention_step(aq, ak, av, k_buf, v_buf, sinks)
        new_k, new_v = ak, av

    pos = pos0 + jnp.arange(L)
    rq = rope(rq.reshape(B, L, R_HEADS, R_DK), pos)
    rk = rope(rk.reshape(B, L, R_HEADS, R_DK), pos) * (R_DK ** -0.5)
    rv = rv.reshape(B, L, R_HEADS, R_DV)
    if s0 is None:
        s0 = jnp.zeros((B, R_HEADS, R_DK, R_DV), jnp.float32)
    ob, s_new = retention(rq, rk, rv, s0)
    ob = head_norm(ob, ret_norm_g).astype(x.dtype) * jax.nn.silu(rg)

    cu = jax.nn.gelu(cu)
    cv = layernorm(jax.nn.gelu(cv), gm_ln_g, gm_ln_b)
    oc = chunk_gmlp(cu, cv, gm_ws, gm_bs)

    ga, gb, gc = jnp.split(jax.nn.sigmoid(mg + b_gate), 3, axis=-1)
    x = x + g1 * ((ga * oa + gb * ob + gc * oc) @ w_out)

    h2 = rmsnorm(x, ln_ffn_g) * (1.0 + sc2) + sh2
    f = moe(h2.reshape(B * L, D), router_w, router_b, moe_w1, moe_b1, moe_w2, moe_b2).reshape(B, L, D)
    x = x + g2 * f
    return x, new_k, new_v, s_new.astype(x.dtype), cv


def setup_inputs(seed: int = 0) -> dict:
    key = jax.random.key(seed)
    ks = iter(jax.random.split(key, 40))

    def nrm(shape, scale):
        return jax.random.normal(next(ks), shape, jnp.float32) * scale

    D = D_MODEL
    w_buf = min(WINDOW, PAST_LEN)
    d_in = sum(IN_SPLITS)
    return {
        'x_prompt': nrm((BATCH, SEQ, D), 1.0),
        'x_sample': nrm((DEC_BATCH, DEC_SEQ, D), 1.0),
        'c_prompt': nrm((BATCH, D), 1.0),
        'c_sample': nrm((DEC_BATCH, D), 1.0),
        'cache_attn_k': nrm((DEPTH, DEC_BATCH, w_buf, A_KV_HEADS, HEAD_DIM), 1.0),
        'cache_attn_v': nrm((DEPTH, DEC_BATCH, w_buf, A_KV_HEADS, HEAD_DIM), 1.0),
        'state_retention': nrm((DEPTH, DEC_BATCH, R_HEADS, R_DK, R_DV), 0.5),
        'ln_mix_g': 1.0 + nrm((DEPTH, D), 0.02),
        'ln_ffn_g': 1.0 + nrm((DEPTH, D), 0.02),
        'w_ada': nrm((DEPTH, D, 6 * D), 0.5 * D ** -0.5),
        'b_ada': nrm((DEPTH, 6 * D), 0.02),
        'w_in': nrm((DEPTH, D, d_in), D ** -0.5),
        'b_gate': nrm((DEPTH, 3 * D), 0.1),
        'q_norm_g': 1.0 + nrm((DEPTH, HEAD_DIM), 0.02),
        'k_norm_g': 1.0 + nrm((DEPTH, HEAD_DIM), 0.02),
        'attn_sinks': nrm((DEPTH, A_Q_HEADS), 0.5),
        'ret_norm_g': 1.0 + nrm((DEPTH, D), 0.02),
        'gm_ln_g': 1.0 + nrm((DEPTH, D), 0.02),
        'gm_ln_b': nrm((DEPTH, D), 0.02),
        'gm_ws': nrm((DEPTH, C_GROUPS, C_CHUNK, C_CHUNK), C_CHUNK ** -0.5),
        'gm_bs': 1.0 + nrm((DEPTH, C_GROUPS, C_CHUNK), 0.02),
        'w_out': nrm((DEPTH, D, D), D ** -0.5),
        'router_w': nrm((DEPTH, D, N_EXPERTS), D ** -0.5),
        'router_b': nrm((DEPTH, N_EXPERTS), 0.01),
        'moe_w1': nrm((DEPTH, N_EXPERTS, D, 2 * D_FF), D ** -0.5),
        'moe_b1': nrm((DEPTH, N_EXPERTS, 2 * D_FF), 0.01),
        'moe_w2': nrm((DEPTH, N_EXPERTS, D_FF, D), D_FF ** -0.5),
        'moe_b2': nrm((DEPTH, N_EXPERTS, D), 0.01),
    }


def reference(x_prompt, x_sample, c_prompt, c_sample, cache_attn_k, cache_attn_v, state_retention,
              ln_mix_g, ln_ffn_g, w_ada, b_ada, w_in, b_gate, q_norm_g, k_norm_g, attn_sinks,
              ret_norm_g, gm_ln_g, gm_ln_b, gm_ws, gm_bs, w_out, router_w, router_b,
              moe_w1, moe_b1, moe_w2, moe_b2):
    xp, xs = x_prompt, x_sample
    pk, pv, ps, sk, sv, ss, sg = [], [], [], [], [], [], []
    for l in range(DEPTH):
        lw = (ln_mix_g[l], ln_ffn_g[l], w_ada[l], b_ada[l], w_in[l], b_gate[l], q_norm_g[l], k_norm_g[l],
              attn_sinks[l], ret_norm_g[l], gm_ln_g[l], gm_ln_b[l], gm_ws[l], gm_bs[l], w_out[l],
              router_w[l], router_b[l], moe_w1[l], moe_b1[l], moe_w2[l], moe_b2[l])
        xp, k_p, v_p, s_p, _ = block(xp, c_prompt, 0, None, None, None, *lw)
        xs, k_s, v_s, s_s, g_s = block(xs, c_sample, PAST_LEN, cache_attn_k[l], cache_attn_v[l],
                                       state_retention[l], *lw)
        pk.append(k_p)
        pv.append(v_p)
        ps.append(s_p)
        sk.append(k_s)
        sv.append(v_s)
        ss.append(s_s)
        sg.append(g_s)
    return (xp, xs, jnp.stack(pk), jnp.stack(pv), jnp.stack(ps), jnp.stack(sk), jnp.stack(sv), jnp.stack(ss), jnp.stack(sg))
```

```python
import functools
import math

import numpy as np
import jax
import jax.numpy as jnp
from jax import lax
from jax.experimental import pallas as pl
from jax.experimental.pallas import tpu as pltpu

F32 = jnp.float32
BF16 = jnp.bfloat16

D_MODEL = 1024
DEPTH = 4
PAST_LEN = 8192
HEAD_DIM = 64
A_Q_HEADS = 16
A_KV_HEADS = 4
A_GROUP = 4
WINDOW = 128
R_HEADS = 8
R_DK = 64
R_DV = 128
R_CHUNK = 128
ROPE_BASE = 10000.0
C_CHUNK = 128
C_GROUPS = 8
N_EXPERTS = 32
TOP_K = 4
D_FF = D_MODEL
SWIGLU_LIMIT = 7.0
SWIGLU_ALPHA = 1.702
EPS = 1e-6

Z_AQ, Z_RV, Z_RG, Z_CU, Z_CV, Z_GA, Z_GB, Z_GC = 0, 1024, 2048, 3072, 4096, 5120, 6144, 7168
Z_RQ, Z_RK, Z_AK, Z_AV = 8192, 8704, 9216, 9472
D_IN = 9728
_ORIG = dict(aq=0, ak=1024, av=1280, rq=1536, rk=2048, rv=2560, rg=3584, cu=4608, cv=5632, mg=6656)

ROW_TILE = 512
MOE_ROWS = 256
ROUTER_PAD = 128
VMEM_LIMIT = 56 * 1024 * 1024
NEG_BIG = -1e30


def _cparams(sem):
    return pltpu.CompilerParams(dimension_semantics=sem, vmem_limit_bytes=VMEM_LIMIT)


def _split_bf16(x):
    hi = x.astype(BF16)
    lo = (x - hi.astype(F32)).astype(BF16)
    return hi, lo


def _dot(a, b):
    return jnp.dot(a, b, preferred_element_type=F32)


def _dot_nt(a, b):
    return lax.dot_general(a, b, (((1,), (1,)), ((), ())), preferred_element_type=F32)


def _dot_tn(a, b):
    return lax.dot_general(a, b, (((0,), (0,)), ((), ())), preferred_element_type=F32)


def _ada_kernel(c_ref, w_ref, b_ref, o_ref):
    c = c_ref[...]
    s_hi, s_lo = _split_bf16(c * jax.nn.sigmoid(c))
    w_hi, w_lo = _split_bf16(w_ref[0])
    acc = _dot(s_hi, w_hi) + _dot(s_lo, w_hi) + _dot(s_hi, w_lo)
    o_ref[0] = acc + b_ref[0]


def _ada_call(c_all, w_ada, b_ada):
    depth, d, n = w_ada.shape
    m = c_all.shape[0]
    tn = 1024
    return pl.pallas_call(
        _ada_kernel,
        grid=(depth, n // tn),
        in_specs=[
            pl.BlockSpec((m, d), lambda l, j: (0, 0)),
            pl.BlockSpec((1, d, tn), lambda l, j: (l, 0, j)),
            pl.BlockSpec((1, 1, tn), lambda l, j: (l, 0, j)),
        ],
        out_specs=pl.BlockSpec((1, m, tn), lambda l, j: (l, 0, j)),
        out_shape=jax.ShapeDtypeStruct((depth, m, n), F32),
        compiler_params=_cparams(("arbitrary", "arbitrary")),
        name="ada",
    )(c_all, w_ada, b_ada.reshape(depth, 1, n))


def _rms(x):
    return x * lax.rsqrt(jnp.mean(x * x, axis=-1, keepdims=True) + EPS)


def _inproj_kernel(*refs, has_resid):
    if has_resid:
        x_ref, f_ref, g2_ref, sh_ref, sc_ref, lng_ref, w_ref, z_ref, xo_ref = refs
        x = x_ref[...] + g2_ref[...] * f_ref[...]
        xo_ref[...] = x
    else:
        x_ref, sh_ref, sc_ref, lng_ref, w_ref, z_ref = refs
        x = x_ref[...]
    gb, rb, d = x.shape
    h = _rms(x) * lng_ref[...] * (1.0 + sc_ref[...]) + sh_ref[...]
    hb = h.reshape(gb * rb, d).astype(BF16)
    n = w_ref.shape[1]
    for c0 in range(0, n, 1024):
        c1 = min(c0 + 1024, n)
        z_ref[:, c0:c1] = _dot(hb, w_ref[:, c0:c1]).astype(BF16)


def _group_blocks(g, r):
    if r >= ROW_TILE:
        return 1, ROW_TILE
    return ROW_TILE // r, r


def _inproj_call(x, mod, ln_g, w_bf, f=None, mod_prev=None):
    g, r, d = x.shape
    gb, rb = _group_blocks(g, r)
    nj = r // rb
    grid = (g // gb, nj)
    xspec = pl.BlockSpec((gb, rb, d), lambda i, j: (i, j, 0))
    mspec = lambda col: pl.BlockSpec((gb, 1, d), lambda i, j: (i, 0, col))
    in_specs, args = [xspec], [x]
    if f is not None:
        in_specs += [xspec, mspec(5)]
        args += [f, mod_prev]
    in_specs += [mspec(0), mspec(1), pl.BlockSpec((1, 1, d), lambda i, j: (0, 0, 0)),
                 pl.BlockSpec((d, D_IN), lambda i, j: (0, 0), pipeline_mode=pl.Buffered(1))]
    args += [mod, mod, ln_g.reshape(1, 1, d), w_bf]
    zspec = pl.BlockSpec((gb * rb, D_IN), lambda i, j: (i * nj + j, 0))
    zshape = jax.ShapeDtypeStruct((g * r, D_IN), BF16)
    if f is not None:
        out_specs, out_shape = [zspec, xspec], [zshape, jax.ShapeDtypeStruct(x.shape, F32)]
    else:
        out_specs, out_shape = zspec, zshape
    return pl.pallas_call(
        functools.partial(_inproj_kernel, has_resid=f is not None),
        grid=grid, in_specs=in_specs, out_specs=out_specs, out_shape=out_shape,
        compiler_params=_cparams(("arbitrary", "arbitrary")),
        name="inproj",
    )(*args)


def _resid_kernel(x_ref, f_ref, g2_ref, o_ref):
    o_ref[...] = x_ref[...] + g2_ref[...] * f_ref[...]


def _resid_call(x, f, mod):
    g, r, d = x.shape
    gb, rb = _group_blocks(g, r)
    xspec = pl.BlockSpec((gb, rb, d), lambda i, j: (i, j, 0))
    return pl.pallas_call(
        _resid_kernel,
        grid=(g // gb, r // rb),
        in_specs=[xspec, xspec, pl.BlockSpec((gb, 1, d), lambda i, j: (i, 0, 5))],
        out_specs=xspec,
        out_shape=jax.ShapeDtypeStruct(x.shape, F32),
        compiler_params=_cparams(("arbitrary", "arbitrary")),
        name="resid",
    )(x, f, mod)


def _head_rms(x, g):
    return x * lax.rsqrt(jnp.mean(x * x, axis=-1, keepdims=True) + EPS) * g


def _stack_q(q, qg, kvh, scale):
    parts = []
    for g in range(A_GROUP):
        c0 = (kvh * A_GROUP + g) * HEAD_DIM
        parts.append(_head_rms(q[:, c0:c0 + HEAD_DIM], qg) * scale)
    return jnp.concatenate(parts, axis=0).astype(BF16)


def _sink_col(sink_ref, kvh, rows):
    return jnp.concatenate(
        [jnp.full((rows, 1), sink_ref[kvh * A_GROUP + g], F32) for g in range(A_GROUP)], axis=0)


def _attn_prompt_kernel(sink_ref, q_ref, k_ref, v_ref, qg_ref, kg_ref, o_ref, nk_ref, nv_ref, kprev, vprev):
    n = pl.program_id(1)
    w = WINDOW

    @pl.when(n == 0)
    def _():
        kprev[...] = jnp.zeros_like(kprev)
        vprev[...] = jnp.zeros_like(vprev)

    q = q_ref[...].astype(F32)
    k = k_ref[...].astype(F32)
    v = v_ref[...]
    qg = qg_ref[...]
    kg = kg_ref[...]
    kn = jnp.concatenate(
        [_head_rms(k[:, h * HEAD_DIM:(h + 1) * HEAD_DIM], kg) for h in range(A_KV_HEADS)], axis=-1)
    knb = kn.astype(BF16)
    kcat = jnp.concatenate([kprev[...], knb], axis=0)
    vcat = jnp.concatenate([vprev[...], v], axis=0)

    i = lax.broadcasted_iota(jnp.int32, (A_GROUP * w, 2 * w), 0) % w
    j = lax.broadcasted_iota(jnp.int32, (A_GROUP * w, 2 * w), 1)
    lo = jnp.where(n == 0, w - 1, -1)
    mask = (j > i) & (j <= i + w) & (j > lo)
    scale = HEAD_DIM ** -0.5
    outs = []
    for h in range(A_KV_HEADS):
        qs = _stack_q(q, qg, h, scale)
        s = _dot_nt(qs, kcat[:, h * HEAD_DIM:(h + 1) * HEAD_DIM])
        s = jnp.where(mask, s, NEG_BIG)
        sink = _sink_col(sink_ref, h, w)
        m = jnp.maximum(jnp.max(s, axis=-1, keepdims=True), sink)
        p = jnp.exp(s - m)
        denom = jnp.sum(p, axis=-1, keepdims=True) + jnp.exp(sink - m)
        o = _dot(p.astype(BF16), vcat[:, h * HEAD_DIM:(h + 1) * HEAD_DIM]) / denom
        outs += [o[g * w:(g + 1) * w] for g in range(A_GROUP)]
    o_ref[...] = jnp.concatenate(outs, axis=-1).astype(o_ref.dtype)

    kprev[...] = knb
    vprev[...] = v

    @pl.when(n == pl.num_programs(1) - 1)
    def _():
        nk_ref[0] = kn
        nv_ref[0] = v.astype(F32)


def _attn_prompt_call(z, sinks, qg, kg, batch, seq):
    nb = seq // WINDOW
    kvw = A_KV_HEADS * HEAD_DIM
    row = lambda b, n, s: b * nb + n
    grid_spec = pltpu.PrefetchScalarGridSpec(
        num_scalar_prefetch=1,
        grid=(batch, nb),
        in_specs=[
            pl.BlockSpec((WINDOW, D_MODEL), lambda b, n, s: (row(b, n, s), Z_AQ // D_MODEL)),
            pl.BlockSpec((WINDOW, kvw), lambda b, n, s: (row(b, n, s), Z_AK // kvw)),
            pl.BlockSpec((WINDOW, kvw), lambda b, n, s: (row(b, n, s), Z_AV // kvw)),
            pl.BlockSpec((1, HEAD_DIM), lambda b, n, s: (0, 0)),
            pl.BlockSpec((1, HEAD_DIM), lambda b, n, s: (0, 0)),
        ],
        out_specs=[
            pl.BlockSpec((WINDOW, D_MODEL), lambda b, n, s: (row(b, n, s), 0)),
            pl.BlockSpec((1, WINDOW, kvw), lambda b, n, s: (b, 0, 0)),
            pl.BlockSpec((1, WINDOW, kvw), lambda b, n, s: (b, 0, 0)),
        ],
        scratch_shapes=[pltpu.VMEM((WINDOW, kvw), BF16), pltpu.VMEM((WINDOW, kvw), BF16)],
    )
    return pl.pallas_call(
        _attn_prompt_kernel,
        grid_spec=grid_spec,
        out_shape=[
            jax.ShapeDtypeStruct((batch * seq, D_MODEL), BF16),
            jax.ShapeDtypeStruct((batch, WINDOW, kvw), F32),
            jax.ShapeDtypeStruct((batch, WINDOW, kvw), F32),
        ],
        compiler_params=_cparams(("arbitrary", "arbitrary")),
        name="attn_prompt",
    )(sinks, z, z, z, qg.reshape(1, HEAD_DIM), kg.reshape(1, HEAD_DIM))


ATTN_S_SEQS = 8


def _attn_sample_kernel(sink_ref, q_ref, k_ref, v_ref, kc_ref, vc_ref, qg_ref, kg_ref, o_ref, nk_ref, nv_ref):
    sb, wb, kvw = kc_ref.shape
    l = q_ref.shape[0] // sb
    rows = sb * l
    q = q_ref[...].astype(F32)
    k = k_ref[...].astype(F32)
    v = v_ref[...]
    qg = qg_ref[...]
    kg = kg_ref[...]
    kn = jnp.concatenate(
        [_head_rms(k[:, h * HEAD_DIM:(h + 1) * HEAD_DIM], kg) for h in range(A_KV_HEADS)], axis=-1)
    nk_ref[...] = kn
    nv_ref[...] = v.astype(F32)
    knb = kn.astype(BF16)
    kc = kc_ref[...].reshape(sb * wb, kvw).astype(BF16)
    vc = vc_ref[...].reshape(sb * wb, kvw).astype(BF16)

    rq = lax.broadcasted_iota(jnp.int32, (A_GROUP * rows, sb * wb), 0) % rows
    cc = lax.broadcasted_iota(jnp.int32, (A_GROUP * rows, sb * wb), 1)
    mask_c = (rq // l == cc // wb) & (cc % wb > rq % l + (wb - WINDOW))
    rq2 = lax.broadcasted_iota(jnp.int32, (A_GROUP * rows, rows), 0) % rows
    cn = lax.broadcasted_iota(jnp.int32, (A_GROUP * rows, rows), 1)
    mask_n = (rq2 // l == cn // l) & (cn % l <= rq2 % l)
    scale = HEAD_DIM ** -0.5
    outs = []
    for h in range(A_KV_HEADS):
        hs = slice(h * HEAD_DIM, (h + 1) * HEAD_DIM)
        qs = _stack_q(q, qg, h, scale)
        s_c = jnp.where(mask_c, _dot_nt(qs, kc[:, hs]), NEG_BIG)
        s_n = jnp.where(mask_n, _dot_nt(qs, knb[:, hs]), NEG_BIG)
        sink = _sink_col(sink_ref, h, rows)
        m = jnp.maximum(jnp.maximum(jnp.max(s_c, axis=-1, keepdims=True),
                                    jnp.max(s_n, axis=-1, keepdims=True)), sink)
        p_c = jnp.exp(s_c - m)
        p_n = jnp.exp(s_n - m)
        denom = (jnp.sum(p_c, axis=-1, keepdims=True) + jnp.sum(p_n, axis=-1, keepdims=True)
                 + jnp.exp(sink - m))
        o = (_dot(p_c.astype(BF16), vc[:, hs]) + _dot(p_n.astype(BF16), v[:, hs])) / denom
        outs += [o[g * rows:(g + 1) * rows] for g in range(A_GROUP)]
    o_ref[...] = jnp.concatenate(outs, axis=-1).astype(o_ref.dtype)


def _attn_sample_call(z, kc, vc, sinks, qg, kg, batch, l):
    kvw = A_KV_HEADS * HEAD_DIM
    sb = ATTN_S_SEQS
    rows = sb * l
    wb = kc.shape[1]
    grid_spec = pltpu.PrefetchScalarGridSpec(
        num_scalar_prefetch=1,
        grid=(batch // sb,),
        in_specs=[
            pl.BlockSpec((rows, D_MODEL), lambda i, s: (i, Z_AQ // D_MODEL)),
            pl.BlockSpec((rows, kvw), lambda i, s: (i, Z_AK // kvw)),
            pl.BlockSpec((rows, kvw), lambda i, s: (i, Z_AV // kvw)),
            pl.BlockSpec((sb, wb, kvw), lambda i, s: (i, 0, 0)),
            pl.BlockSpec((sb, wb, kvw), lambda i, s: (i, 0, 0)),
            pl.BlockSpec((1, HEAD_DIM), lambda i, s: (0, 0)),
            pl.BlockSpec((1, HEAD_DIM), lambda i, s: (0, 0)),
        ],
        out_specs=[
            pl.BlockSpec((rows, D_MODEL), lambda i, s: (i, 0)),
            pl.BlockSpec((rows, kvw), lambda i, s: (i, 0)),
            pl.BlockSpec((rows, kvw), lambda i, s: (i, 0)),
        ],
    )
    return pl.pallas_call(
        _attn_sample_kernel,
        grid_spec=grid_spec,
        out_shape=[
            jax.ShapeDtypeStruct((batch * l, D_MODEL), BF16),
            jax.ShapeDtypeStruct((batch * l, kvw), F32),
            jax.ShapeDtypeStruct((batch * l, kvw), F32),
        ],
        compiler_params=_cparams(("arbitrary",)),
        name="attn_sample",
    )(sinks, z, z, z, kc, vc, qg.reshape(1, HEAD_DIM), kg.reshape(1, HEAD_DIM))


def _ret_tables(chunk, pos0, length):
    h = np.arange(R_HEADS, dtype=np.float64)
    log_gamma = np.log1p(-np.exp2(-5.0 - h))
    idx = np.arange(chunk, dtype=np.float64)
    diff = idx[:, None] - idx[None, :]
    intra = np.where(diff[None] >= 0, np.exp(np.maximum(diff, 0.0)[None] * log_gamma[:, None, None]), 0.0)
    q_decay = np.exp((idx + 1.0)[:, None] * log_gamma[None, :])
    k_decay = np.exp((chunk - 1.0 - idx)[:, None] * log_gamma[None, :])
    c_decay = np.exp(chunk * log_gamma)
    qd = np.repeat(q_decay, R_DK, axis=1)
    kd = np.repeat(k_decay, R_DK, axis=1)
    inv_freq = ROPE_BASE ** (-np.arange(0, R_DK, 2, dtype=np.float64) / R_DK)
    ang = (pos0 + np.arange(length, dtype=np.float64))[:, None] * inv_freq[None, :]
    cos = np.tile(np.concatenate([np.cos(ang), np.cos(ang)], axis=1), (1, R_HEADS))
    sin = np.tile(np.concatenate([-np.sin(ang), np.sin(ang)], axis=1), (1, R_HEADS))
    f = lambda a: jnp.asarray(a, F32)
    return f(intra), f(qd), f(kd), [float(c) for c in c_decay], f(cos), f(sin)


def _rope(x, cos, sin):
    n = x.shape[-1]
    half = R_DK // 2
    lane = lax.broadcasted_iota(jnp.int32, x.shape, 1)
    up = pltpu.roll(x, n - half, axis=1)
    dn = pltpu.roll(x, half, axis=1)
    partner = jnp.where(lane % R_DK < half, up, dn)
    return x * cos + partner * sin


def _head_ln(o, g):
    mu = jnp.mean(o, axis=-1, keepdims=True)
    oc = o - mu
    var = jnp.mean(oc * oc, axis=-1, keepdims=True)
    return oc * lax.rsqrt(var + EPS) * g


def _silu(x):
    return x * jax.nn.sigmoid(x)


def _ret_prompt_kernel(q_ref, k_ref, v_ref, g_ref, cos_ref, sin_ref, intra_ref, qd_ref, kd_ref, ng_ref,
                       o_ref, s_ref, state, *, c_decay):
    n = pl.program_id(1)

    @pl.when(n == 0)
    def _():
        state[...] = jnp.zeros_like(state)

    cos = cos_ref[...]
    sin = sin_ref[...]
    q = _rope(q_ref[...].astype(F32), cos, sin)
    k = _rope(k_ref[...].astype(F32), cos, sin) * (R_DK ** -0.5)
    qb = q.astype(BF16)
    kb = k.astype(BF16)
    qdb = (q * qd_ref[...]).astype(BF16)
    kdb = (k * kd_ref[...]).astype(BF16)
    for h in range(R_HEADS):
        ks = slice(h * R_DK, (h + 1) * R_DK)
        vs = slice(h * R_DV, (h + 1) * R_DV)
        vh = v_ref[:, vs]
        a = _dot_nt(qb[:, ks], kb[:, ks]) * intra_ref[h]
        s_old = state[h]
        o = _dot(a.astype(BF16), vh) + _dot(qdb[:, ks], s_old.astype(BF16))
        state[h] = s_old * c_decay[h] + _dot_tn(kdb[:, ks], vh)
        y = _head_ln(o, ng_ref[:, vs]) * _silu(g_ref[:, vs].astype(F32))
        o_ref[:, vs] = y.astype(o_ref.dtype)

    @pl.when(n == pl.num_programs(1) - 1)
    def _():
        s_ref[0] = state[...]


def _ret_prompt_call(z, ret_norm_g, batch, seq):
    c = R_CHUNK
    nc = seq // c
    intra, qd, kd, c_decay, cos, sin = _ret_tables(c, 0, seq)
    qkw = R_HEADS * R_DK
    row = lambda b, n: b * nc + n
    const2 = lambda b, n: (0, 0)
    return pl.pallas_call(
        functools.partial(_ret_prompt_kernel, c_decay=c_decay),
        grid=(batch, nc),
        in_specs=[
            pl.BlockSpec((c, qkw), lambda b, n: (row(b, n), Z_RQ // qkw)),
            pl.BlockSpec((c, qkw), lambda b, n: (row(b, n), Z_RK // qkw)),
            pl.BlockSpec((c, D_MODEL), lambda b, n: (row(b, n), Z_RV // D_MODEL)),
            pl.BlockSpec((c, D_MODEL), lambda b, n: (row(b, n), Z_RG // D_MODEL)),
            pl.BlockSpec((c, qkw), lambda b, n: (n, 0)),
            pl.BlockSpec((c, qkw), lambda b, n: (n, 0)),
            pl.BlockSpec((R_HEADS, c, c), lambda b, n: (0, 0, 0)),
            pl.BlockSpec((c, qkw), const2),
            pl.BlockSpec((c, qkw), const2),
            pl.BlockSpec((1, D_MODEL), const2),
        ],
        out_specs=[
            pl.BlockSpec((c, D_MODEL), lambda b, n: (row(b, n), 0)),
            pl.BlockSpec((1, R_HEADS, R_DK, R_DV), lambda b, n: (b, 0, 0, 0)),
        ],
        out_shape=[
            jax.ShapeDtypeStruct((batch * seq, D_MODEL), BF16),
            jax.ShapeDtypeStruct((batch, R_HEADS, R_DK, R_DV), F32),
        ],
        scratch_shapes=[pltpu.VMEM((R_HEADS, R_DK, R_DV), F32)],
        compiler_params=_cparams(("arbitrary", "arbitrary")),
        name="ret_prompt",
    )(z, z, z, z, cos, sin, intra, qd, kd, ret_norm_g.reshape(1, D_MODEL))


RET_S_SEQS = 8


def _ret_sample_kernel(q_ref, k_ref, v_ref, g_ref, s0_ref, cos_ref, sin_ref, intra_ref, qd_ref, kd_ref, ng_ref,
                       o_ref, s_ref, *, c_decay, l):
    sb = s0_ref.shape[0]
    rows = sb * l
    cos = cos_ref[...]
    sin = sin_ref[...]
    q = _rope(q_ref[...].astype(F32), cos, sin)
    k = _rope(k_ref[...].astype(F32), cos, sin) * (R_DK ** -0.5)
    qb = q.astype(BF16)
    kb = k.astype(BF16)
    qdb = (q * qd_ref[...]).astype(BF16)
    kdb = (k * kd_ref[...]).astype(BF16)
    for h in range(R_HEADS):
        ks = slice(h * R_DK, (h + 1) * R_DK)
        vs = slice(h * R_DV, (h + 1) * R_DV)
        vh = v_ref[:, vs]
        a = _dot_nt(qb[:, ks], kb[:, ks]) * intra_ref[h]
        o = _dot(a.astype(BF16), vh)
        cross, new_s = [], []
        for b in range(sb):
            rs = slice(b * l, (b + 1) * l)
            s_old = s0_ref[b, h]
            cross.append(_dot(qdb[rs, ks], s_old.astype(BF16)))
            s_ref[b, h] = s_old * c_decay[h] + _dot_tn(kdb[rs, ks], vh[rs])
        o = o + jnp.concatenate(cross, axis=0)
        y = _head_ln(o, ng_ref[:, vs]) * _silu(g_ref[:, vs].astype(F32))
        o_ref[:, vs] = y.astype(o_ref.dtype)


def _ret_sample_call(z, s0, ret_norm_g, batch, l):
    c = math.gcd(l, R_CHUNK)
    assert c == l, "sample step expects a single retention chunk"
    sb = RET_S_SEQS
    rows = sb * l
    intra, qd, kd, c_decay, cos, sin = _ret_tables(c, PAST_LEN, l)
    eye = jnp.eye(sb, dtype=F32)
    intra_bd = jnp.einsum("ab,hij->haibj", eye, intra).reshape(R_HEADS, rows, rows)
    tile = lambda t: jnp.tile(t, (sb, 1))
    qkw = R_HEADS * R_DK
    const2 = lambda i: (0, 0)
    return pl.pallas_call(
        functools.partial(_ret_sample_kernel, c_decay=c_decay, l=l),
        grid=(batch // sb,),
        in_specs=[
            pl.BlockSpec((rows, qkw), lambda i: (i, Z_RQ // qkw)),
            pl.BlockSpec((rows, qkw), lambda i: (i, Z_RK // qkw)),
            pl.BlockSpec((rows, D_MODEL), lambda i: (i, Z_RV // D_MODEL)),
            pl.BlockSpec((rows, D_MODEL), lambda i: (i, Z_RG // D_MODEL)),
            pl.BlockSpec((sb, R_HEADS, R_DK, R_DV), lambda i: (i, 0, 0, 0)),
            pl.BlockSpec((rows, qkw), const2),
            pl.BlockSpec((rows, qkw), const2),
            pl.BlockSpec((R_HEADS, rows, rows), lambda i: (0, 0, 0)),
            pl.BlockSpec((rows, qkw), const2),
            pl.BlockSpec((rows, qkw), const2),
            pl.BlockSpec((1, D_MODEL), const2),
        ],
        out_specs=[
            pl.BlockSpec((rows, D_MODEL), lambda i: (i, 0)),
            pl.BlockSpec((sb, R_HEADS, R_DK, R_DV), lambda i: (i, 0, 0, 0)),
        ],
        out_shape=[
            jax.ShapeDtypeStruct((batch * l, D_MODEL), BF16),
            jax.ShapeDtypeStruct(s0.shape, F32),
        ],
        compiler_params=_cparams(("arbitrary",)),
        name="ret_sample",
    )(z, z, z, z, s0, tile(cos), tile(sin), intra_bd, tile(qd), tile(kd), ret_norm_g.reshape(1, D_MODEL))


def _gelu(x):
    return jax.nn.gelu(x, approximate=True)


def _merge_kernel(*refs, emit_cv):
    (x_ref, oa_ref, ob_ref, cu_ref, cv_ref, ga_ref, gb_ref, gc_ref, g1_ref, sh2_ref, sc2_ref,
     bg_ref, lnf_ref, gmg_ref, gmb_ref, mix_ref, mixb_ref, wout_ref, rwh_ref, rwl_ref, rb_ref) = refs[:21]
    if emit_cv:
        xo_ref, h2_ref, lg_ref, cvo_ref = refs[21:]
    else:
        xo_ref, h2_ref, lg_ref = refs[21:]
    x = x_ref[...]
    gbk, rb, d = x.shape
    tm = gbk * rb
    cw = C_CHUNK
    gw = d // C_GROUPS

    cv = _gelu(cv_ref[...].astype(F32))
    mu = jnp.mean(cv, axis=-1, keepdims=True)
    cvc = cv - mu
    var = jnp.mean(cvc * cvc, axis=-1, keepdims=True)
    cv = cvc * lax.rsqrt(var + EPS) * gmg_ref[...] + gmb_ref[...]
    if emit_cv:
        cvo_ref[...] = cv
    cvb = cv.astype(BF16)
    mixed_rows = []
    for c in range(tm // cw):
        rs = slice(c * cw, (c + 1) * cw)
        cols = [_dot(mix_ref[g], cvb[rs, g * gw:(g + 1) * gw]) for g in range(C_GROUPS)]
        mixed_rows.append(jnp.concatenate(cols, axis=-1) + mixb_ref[...])
    mixed = jnp.concatenate(mixed_rows, axis=0)
    oc = _gelu(cu_ref[...].astype(F32)) * mixed

    bg = bg_ref[...]
    ga = jax.nn.sigmoid(ga_ref[...].astype(F32) + bg[0:1])
    gb = jax.nn.sigmoid(gb_ref[...].astype(F32) + bg[1:2])
    gc = jax.nn.sigmoid(gc_ref[...].astype(F32) + bg[2:3])
    merged = ga * oa_ref[...].astype(F32) + gb * ob_ref[...].astype(F32) + gc * oc
    y = _dot(merged.astype(BF16), wout_ref[...])
    x = x + g1_ref[...] * y.reshape(gbk, rb, d)
    xo_ref[...] = x

    h2 = (_rms(x) * lnf_ref[...] * (1.0 + sc2_ref[...]) + sh2_ref[...]).reshape(tm, d)
    h2b = h2.astype(BF16)
    h2_ref[...] = h2b
    h2l = (h2 - h2b.astype(F32)).astype(BF16)
    rwh = rwh_ref[...]
    lg_ref[...] = _dot(h2b, rwh) + _dot(h2l, rwh) + _dot(h2b, rwl_ref[...]) + rb_ref[...]


def _merge_call(x, z, oa, ob, mod, b_gate, ln_ffn_g, gm_ln_g, gm_ln_b, mix, mixb, w_out_bf, rw_hi, rw_lo, rb,
                emit_cv):
    g, r, d = x.shape
    gb, rb_ = _group_blocks(g, r)
    tm = gb * rb_
    nj = r // rb_
    t = g * r
    xspec = pl.BlockSpec((gb, rb_, d), lambda i, j: (i, j, 0))
    rows = lambda col: pl.BlockSpec((tm, d), lambda i, j: (i * nj + j, col))
    mspec = lambda col: pl.BlockSpec((gb, 1, d), lambda i, j: (i, 0, col))
    const = lambda shape: pl.BlockSpec(shape, lambda i, j: (0,) * len(shape))
    in_specs = [
        xspec, rows(0), rows(0),
        rows(Z_CU // d), rows(Z_CV // d), rows(Z_GA // d), rows(Z_GB // d), rows(Z_GC // d),
        mspec(2), mspec(3), mspec(4),
        const((3, d)), const((1, 1, d)), const((1, d)), const((1, d)),
        const((C_GROUPS, C_CHUNK, C_CHUNK)), const((C_CHUNK, d)),
        const((d, d)), const((d, ROUTER_PAD)), const((d, ROUTER_PAD)), const((1, ROUTER_PAD)),
    ]
    out_specs = [xspec, rows(0), pl.BlockSpec((tm, ROUTER_PAD), lambda i, j: (i * nj + j, 0))]
    out_shape = [jax.ShapeDtypeStruct(x.shape, F32), jax.ShapeDtypeStruct((t, d), BF16),
                 jax.ShapeDtypeStruct((t, ROUTER_PAD), F32)]
    if emit_cv:
        out_specs.append(rows(0))
        out_shape.append(jax.ShapeDtypeStruct((t, d), F32))
    return pl.pallas_call(
        functools.partial(_merge_kernel, emit_cv=emit_cv),
        grid=(g // gb, nj), in_specs=in_specs, out_specs=out_specs, out_shape=out_shape,
        compiler_params=_cparams(("arbitrary", "arbitrary")),
        name="merge",
    )(x, oa, ob, z, z, z, z, z, mod, mod, mod,
      b_gate.reshape(3, d), ln_ffn_g.reshape(1, 1, d), gm_ln_g.reshape(1, d), gm_ln_b.reshape(1, d),
      mix, mixb, w_out_bf, rw_hi, rw_lo, rb)


def _moe_kernel(be_ref, nu_ref, x_ref, w1_ref, b1_ref, w2_ref, b2_ref, o_ref, w1b, w2b):
    i = pl.program_id(0)
    e = be_ref[i]
    prev = be_ref[jnp.maximum(i - 1, 0)]

    @pl.when((i == 0) | (e != prev))
    def _():
        w1b[...] = w1_ref[0].astype(BF16)
        w2b[...] = w2_ref[0].astype(BF16)

    @pl.when(i < nu_ref[0])
    def _():
        hdn = _dot(x_ref[...], w1b[...]) + b1_ref[0]
        g = jnp.minimum(hdn[:, :D_FF], SWIGLU_LIMIT)
        up = jnp.clip(hdn[:, D_FF:], -SWIGLU_LIMIT, SWIGLU_LIMIT)
        act = (up + 1.0) * (g * jax.nn.sigmoid(g * SWIGLU_ALPHA))
        o_ref[...] = (_dot(act.astype(BF16), w2b[...]) + b2_ref[0]).astype(o_ref.dtype)


def _moe_call(xb, block_e, n_used, w1, b1, w2, b2):
    n_rows, d = xb.shape
    nblk = n_rows // MOE_ROWS
    ne, _, f2 = w1.shape
    last = lambda i, be, nu: jnp.minimum(i, nu[0] - 1)
    grid_spec = pltpu.PrefetchScalarGridSpec(
        num_scalar_prefetch=2,
        grid=(nblk,),
        in_specs=[
            pl.BlockSpec((MOE_ROWS, d), lambda i, be, nu: (last(i, be, nu), 0)),
            pl.BlockSpec((1, d, f2), lambda i, be, nu: (be[i], 0, 0)),
            pl.BlockSpec((1, 1, f2), lambda i, be, nu: (be[i], 0, 0)),
            pl.BlockSpec((1, f2 // 2, d), lambda i, be, nu: (be[i], 0, 0)),
            pl.BlockSpec((1, 1, d), lambda i, be, nu: (be[i], 0, 0)),
        ],
        out_specs=pl.BlockSpec((MOE_ROWS, d), lambda i, be, nu: (last(i, be, nu), 0)),
        scratch_shapes=[pltpu.VMEM((d, f2), BF16), pltpu.VMEM((f2 // 2, d), BF16)],
    )
    return pl.pallas_call(
        _moe_kernel,
        grid_spec=grid_spec,
        out_shape=jax.ShapeDtypeStruct((n_rows, d), BF16),
        compiler_params=_cparams(("arbitrary",)),
        name="moe",
    )(block_e, n_used, xb, w1, b1.reshape(ne, 1, f2), w2, b2.reshape(ne, 1, d))


def _route(logits):
    t = logits.shape[0]
    top_v, top_i = lax.top_k(logits, TOP_K)
    gates = jax.nn.softmax(top_v, axis=-1)
    n_assign = t * TOP_K
    flat_e = top_i.reshape(-1).astype(jnp.int32)
    order = jnp.argsort(flat_e, stable=True).astype(jnp.int32)
    se = flat_e[order]
    st = order // TOP_K
    counts = jnp.bincount(flat_e, length=N_EXPERTS).astype(jnp.int32)
    padded = (counts + MOE_ROWS - 1) // MOE_ROWS * MOE_ROWS
    pend = jnp.cumsum(padded)
    pstart = pend - padded
    ustart = jnp.cumsum(counts) - counts
    dest_sorted = pstart[se] + jnp.arange(n_assign, dtype=jnp.int32) - ustart[se]
    n_blocks = -(-n_assign // MOE_ROWS) + N_EXPERTS
    n_rows = n_blocks * MOE_ROWS
    tok_buf = jnp.zeros((n_rows,), jnp.int32).at[dest_sorted].set(st)
    dest = jnp.zeros((n_assign,), jnp.int32).at[order].set(dest_sorted).reshape(t, TOP_K)
    block_e = jnp.clip(jnp.searchsorted(pend, jnp.arange(n_blocks, dtype=jnp.int32) * MOE_ROWS, side="right"),
                       0, N_EXPERTS - 1).astype(jnp.int32)
    n_used = (pend[-1] // MOE_ROWS).astype(jnp.int32).reshape(1)
    block_e = jnp.where(jnp.arange(n_blocks) < n_used[0], block_e, block_e[jnp.maximum(n_used[0] - 1, 0)])
    return gates, tok_buf, dest, block_e, n_used


def _moe(h2, logits, w1, b1, w2, b2):
    gates, tok_buf, dest, block_e, n_used = _route(logits[:, :N_EXPERTS])
    xb = h2[tok_buf]
    yb = _moe_call(xb, block_e, n_used, w1, b1, w2, b2)
    y = yb[dest].astype(F32) * gates[:, :, None]
    return jnp.sum(y, axis=1)


def _reorder_w_in(w):
    o = _ORIG
    seg = lambda a, n: w[:, a:a + n]
    d = D_MODEL
    parts = [seg(o["aq"], d), seg(o["rv"], d), seg(o["rg"], d), seg(o["cu"], d), seg(o["cv"], d),
             seg(o["mg"], d), seg(o["mg"] + d, d), seg(o["mg"] + 2 * d, d),
             seg(o["rq"], 512), seg(o["rk"], 512), seg(o["ak"], 256), seg(o["av"], 256)]
    return jnp.concatenate(parts, axis=1).astype(BF16)


def _gmlp_tables(ws, bs, chunk_len, rows):
    causal = jnp.tril(jnp.ones((C_CHUNK, C_CHUNK), dtype=bool))
    w = jnp.where(causal[None], ws, 0.0)[:, :chunk_len, :chunk_len]
    reps = rows // chunk_len
    eye = jnp.eye(reps, dtype=F32)
    mix = jnp.einsum("ab,gts->gatbs", eye, w).reshape(C_GROUPS, rows, rows).astype(BF16)
    b = jnp.tile(bs[:, :chunk_len].T, (reps, 1))
    mixb = jnp.repeat(b, D_MODEL // C_GROUPS, axis=1)
    return mix, mixb


def kernel(x_prompt, x_sample, c_prompt, c_sample, cache_attn_k, cache_attn_v, state_retention, ln_mix_g, ln_ffn_g, w_ada, b_ada, w_in, b_gate, q_norm_g, k_norm_g, attn_sinks, ret_norm_g, gm_ln_g, gm_ln_b, gm_ws, gm_bs, w_out, router_w, router_b, moe_w1, moe_b1, moe_w2, moe_b2):
    bp, lp, d = x_prompt.shape
    bs, ls, _ = x_sample.shape
    tp, ts = bp * lp, bs * ls
    kvw = A_KV_HEADS * HEAD_DIM
    wb = cache_attn_k.shape[2]

    mod_all = _ada_call(jnp.concatenate([c_prompt, c_sample], axis=0), w_ada, b_ada)

    xp, xs = x_prompt, x_sample
    fp = fs = None
    mod_p = mod_s = None
    pk, pv, ps, sk, sv, ss, sg = [], [], [], [], [], [], []
    for l in range(DEPTH):
        prev_mod_p, prev_mod_s = mod_p, mod_s
        mod_p = mod_all[l, :bp].reshape(bp, 1, 6 * d)
        mod_s = mod_all[l, bp:].reshape(bs, 1, 6 * d)
        w_in_bf = _reorder_w_in(w_in[l])
        w_out_bf = w_out[l].astype(BF16)
        rw = jnp.pad(router_w[l], ((0, 0), (0, ROUTER_PAD - N_EXPERTS)))
        rw_hi, rw_lo = _split_bf16(rw)
        rb = jnp.pad(router_b[l], (0, ROUTER_PAD - N_EXPERTS)).reshape(1, ROUTER_PAD)

        if l == 0:
            zp = _inproj_call(xp, mod_p, ln_mix_g[l], w_in_bf)
            zs = _inproj_call(xs, mod_s, ln_mix_g[l], w_in_bf)
        else:
            zp, xp = _inproj_call(xp, mod_p, ln_mix_g[l], w_in_bf, f=fp, mod_prev=prev_mod_p)
            zs, xs = _inproj_call(xs, mod_s, ln_mix_g[l], w_in_bf, f=fs, mod_prev=prev_mod_s)
        oa_p, k_p, v_p = _attn_prompt_call(zp, attn_sinks[l], q_norm_g[l], k_norm_g[l], bp, lp)
        oa_s, k_s, v_s = _attn_sample_call(zs, cache_attn_k[l].reshape(bs, wb, kvw),
                                           cache_attn_v[l].reshape(bs, wb, kvw),
                                           attn_sinks[l], q_norm_g[l], k_norm_g[l], bs, ls)
        ob_p, s_p = _ret_prompt_call(zp, ret_norm_g[l], bp, lp)
        ob_s, s_s = _ret_sample_call(zs, state_retention[l], ret_norm_g[l], bs, ls)

        mix_p, mixb_p = _gmlp_tables(gm_ws[l], gm_bs[l], C_CHUNK, C_CHUNK)
        mix_s, mixb_s = _gmlp_tables(gm_ws[l], gm_bs[l], ls, C_CHUNK)
        common = (b_gate[l], ln_ffn_g[l], gm_ln_g[l], gm_ln_b[l])
        tail = (w_out_bf, rw_hi, rw_lo, rb)
        xp, h2_p, lg_p = _merge_call(xp, zp, oa_p, ob_p, mod_p, *common, mix_p, mixb_p, *tail, emit_cv=False)
        xs, h2_s, lg_s, cv_s = _merge_call(xs, zs, oa_s, ob_s, mod_s, *common, mix_s, mixb_s, *tail, emit_cv=True)

        f_all = _moe(jnp.concatenate([h2_p, h2_s], axis=0), jnp.concatenate([lg_p, lg_s], axis=0),
                     moe_w1[l], moe_b1[l], moe_w2[l], moe_b2[l])
        fp = f_all[:tp].reshape(bp, lp, d)
        fs = f_all[tp:].reshape(bs, ls, d)

        pk.append(k_p.reshape(bp, WINDOW, A_KV_HEADS, HEAD_DIM))
        pv.append(v_p.reshape(bp, WINDOW, A_KV_HEADS, HEAD_DIM))
        ps.append(s_p)
        sk.append(k_s.reshape(bs, ls, A_KV_HEADS, HEAD_DIM))
        sv.append(v_s.reshape(bs, ls, A_KV_HEADS, HEAD_DIM))
        ss.append(s_s)
        sg.append(cv_s.reshape(bs, ls, d))

    xp = _resid_call(xp, fp, mod_p)
    xs = _resid_call(xs, fs, mod_s)
    return (xp, xs, jnp.stack(pk), jnp.stack(pv), jnp.stack(ps), jnp.stack(sk), jnp.stack(sv),
            jnp.stack(ss), jnp.stack(sg))
```

```python
import functools
import math

import numpy as np
import jax
import jax.numpy as jnp
from jax import lax
from jax.experimental import pallas as pl
from jax.experimental.pallas import tpu as pltpu
from jax.experimental.pallas import tpu_sc as plsc

F32 = jnp.float32
BF16 = jnp.bfloat16
U32 = jnp.uint32

D_MODEL = 1024
DEPTH = 4
PAST_LEN = 8192
HEAD_DIM = 64
A_Q_HEADS = 16
A_KV_HEADS = 4
A_GROUP = 4
WINDOW = 128
R_HEADS = 8
R_DK = 64
R_DV = 128
R_CHUNK = 128
ROPE_BASE = 10000.0
C_CHUNK = 128
C_GROUPS = 8
N_EXPERTS = 32
TOP_K = 4
D_FF = D_MODEL
SWIGLU_LIMIT = 7.0
SWIGLU_ALPHA = 1.702
EPS = 1e-6

Z_AQ, Z_RV, Z_RG, Z_CU, Z_CV, Z_GA, Z_GB, Z_GC = 0, 1024, 2048, 3072, 4096, 5120, 6144, 7168
Z_RQ, Z_RK, Z_AK, Z_AV = 8192, 8704, 9216, 9472
D_IN = 9728
IN_CHUNKS = 4
_ORIG = dict(aq=0, ak=1024, av=1280, rq=1536, rk=2048, rv=2560, rg=3584, cu=4608, cv=5632, mg=6656)

ROW_TILE = 512
MOE_ROWS = 256
ROUTER_PAD = 128
PACK_W = D_MODEL // 4
SC_WINDOW = 128
HI_MASK = 0xFFFF0000
VMEM_LIMIT = 56 * 1024 * 1024
NEG_BIG = -1e30


def _cparams(sem):
    return pltpu.CompilerParams(dimension_semantics=sem, vmem_limit_bytes=VMEM_LIMIT)


def _split_bf16(x):
    hi = x.astype(BF16)
    lo = (x - hi.astype(F32)).astype(BF16)
    return hi, lo


def _dot(a, b):
    return jnp.dot(a, b, preferred_element_type=F32)


def _dot_nt(a, b):
    return lax.dot_general(a, b, (((1,), (1,)), ((), ())), preferred_element_type=F32)


def _dot_tn(a, b):
    return lax.dot_general(a, b, (((0,), (0,)), ((), ())), preferred_element_type=F32)


def _ada_kernel(c_ref, w_ref, b_ref, o_ref):
    c = c_ref[...]
    s_hi, s_lo = _split_bf16(c * jax.nn.sigmoid(c))
    w_hi, w_lo = _split_bf16(w_ref[0])
    acc = _dot(s_hi, w_hi) + _dot(s_lo, w_hi) + _dot(s_hi, w_lo)
    o_ref[0] = acc + b_ref[0]


def _ada_call(c_all, w_ada, b_ada):
    depth, d, n = w_ada.shape
    m = c_all.shape[0]
    tn = 1024
    return pl.pallas_call(
        _ada_kernel,
        grid=(depth, n // tn),
        in_specs=[
            pl.BlockSpec((m, d), lambda l, j: (0, 0)),
            pl.BlockSpec((1, d, tn), lambda l, j: (l, 0, j)),
            pl.BlockSpec((1, 1, tn), lambda l, j: (l, 0, j)),
        ],
        out_specs=pl.BlockSpec((1, m, tn), lambda l, j: (l, 0, j)),
        out_shape=jax.ShapeDtypeStruct((depth, m, n), F32),
        compiler_params=_cparams(("arbitrary", "arbitrary")),
        name="ada",
    )(c_all, w_ada, b_ada.reshape(depth, 1, n))


def _pack_rows(y):
    bits = pltpu.bitcast(y.astype(BF16).astype(F32), U32)
    q = [bits[:, i * PACK_W:(i + 1) * PACK_W] for i in range(4)]
    return (q[0] >> 16) | q[1], (q[2] >> 16) | q[3]


def _unpack_rows(a, b):
    f = lambda w: pltpu.bitcast(w, F32)
    return jnp.concatenate([f(a << 16), f(a & jnp.uint32(HI_MASK)), f(b << 16), f(b & jnp.uint32(HI_MASK))], axis=-1)


def _combine(y4_ref, route_ref):
    route = route_ref[...]
    acc = None
    for k in range(TOP_K):
        term = route[:, k:k + 1] * _unpack_rows(y4_ref[0, k], y4_ref[1, k])
        acc = term if acc is None else acc + term
    return acc


def _rms(x):
    return x * lax.rsqrt(jnp.mean(x * x, axis=-1, keepdims=True) + EPS)


def _inproj_kernel(*refs, has_resid):
    if has_resid:
        x_ref, y4_ref, route_ref, g2_ref, sh_ref, sc_ref, lng_ref, w_ref, z_ref, xo_ref, hb = refs
    else:
        x_ref, sh_ref, sc_ref, lng_ref, w_ref, z_ref, hb = refs
    c = pl.program_id(2)

    @pl.when(c == 0)
    def _():
        x = x_ref[...]
        if has_resid:
            x = x + g2_ref[...] * _combine(y4_ref, route_ref).reshape(x.shape)
            xo_ref[...] = x
        h = _rms(x) * lng_ref[...] * (1.0 + sc_ref[...]) + sh_ref[...]
        hb[...] = h.reshape(hb.shape).astype(BF16)

    h = hb[...]
    n = w_ref.shape[2]
    for c0 in range(0, n, 1024):
        c1 = min(c0 + 1024, n)
        z_ref[:, c0:c1] = _dot(h, w_ref[c, :, c0:c1]).astype(BF16)


def _group_blocks(g, r):
    if r >= ROW_TILE:
        return 1, ROW_TILE
    return ROW_TILE // r, r


def _moe_out_specs(tm, nj, tile_off):
    return [pl.BlockSpec((2, TOP_K, tm, PACK_W), lambda i, j, *_: (0, 0, tile_off + i * nj + j, 0)),
            pl.BlockSpec((tm, ROUTER_PAD), lambda i, j, *_: (tile_off + i * nj + j, 0))]


def _inproj_call(x, mod, ln_g, w_bf, moe_out=None):
    g, r, d = x.shape
    gb, rb = _group_blocks(g, r)
    nj = r // rb
    nc, _, cw = w_bf.shape
    grid = (g // gb, nj, nc)
    xspec = pl.BlockSpec((gb, rb, d), lambda i, j, c: (i, j, 0))
    mspec = lambda col: pl.BlockSpec((gb, 1, d), lambda i, j, c: (i, 0, col))
    in_specs, args = [xspec], [x]
    if moe_out is not None:
        y4, route, mod_prev, tile_off = moe_out
        in_specs += _moe_out_specs(gb * rb, nj, tile_off) + [mspec(5)]
        args += [y4, route, mod_prev]
    in_specs += [mspec(0), mspec(1), pl.BlockSpec((1, 1, d), lambda i, j, c: (0, 0, 0)),
                 pl.BlockSpec((nc, d, cw), lambda i, j, c: (0, 0, 0), pipeline_mode=pl.Buffered(1))]
    args += [mod, mod, ln_g.reshape(1, 1, d), w_bf]
    zspec = pl.BlockSpec((gb * rb, cw), lambda i, j, c: (i * nj + j, c))
    zshape = jax.ShapeDtypeStruct((g * r, nc * cw), BF16)
    if moe_out is not None:
        out_specs, out_shape = [zspec, xspec], [zshape, jax.ShapeDtypeStruct(x.shape, F32)]
    else:
        out_specs, out_shape = zspec, zshape
    return pl.pallas_call(
        functools.partial(_inproj_kernel, has_resid=moe_out is not None),
        grid=grid, in_specs=in_specs, out_specs=out_specs, out_shape=out_shape,
        scratch_shapes=[pltpu.VMEM((gb * rb, d), BF16)],
        compiler_params=_cparams(("arbitrary", "arbitrary", "arbitrary")),
        name="inproj",
    )(*args)


def _resid_kernel(x_ref, y4_ref, route_ref, g2_ref, o_ref):
    x = x_ref[...]
    o_ref[...] = x + g2_ref[...] * _combine(y4_ref, route_ref).reshape(x.shape)


def _resid_call(x, y4, route, mod, tile_off):
    g, r, d = x.shape
    gb, rb = _group_blocks(g, r)
    nj = r // rb
    xspec = pl.BlockSpec((gb, rb, d), lambda i, j: (i, j, 0))
    return pl.pallas_call(
        _resid_kernel,
        grid=(g // gb, nj),
        in_specs=[xspec] + _moe_out_specs(gb * rb, nj, tile_off)
        + [pl.BlockSpec((gb, 1, d), lambda i, j: (i, 0, 5))],
        out_specs=xspec,
        out_shape=jax.ShapeDtypeStruct(x.shape, F32),
        compiler_params=_cparams(("arbitrary", "arbitrary")),
        name="resid",
    )(x, y4, route, mod)


def _head_rms(x, g):
    return x * lax.rsqrt(jnp.mean(x * x, axis=-1, keepdims=True) + EPS) * g


def _stack_q(q, qg, kvh, scale):
    parts = []
    for g in range(A_GROUP):
        c0 = (kvh * A_GROUP + g) * HEAD_DIM
        parts.append(_head_rms(q[:, c0:c0 + HEAD_DIM], qg) * scale)
    return jnp.concatenate(parts, axis=0).astype(BF16)


def _sink_col(sink_ref, kvh, rows):
    return jnp.concatenate(
        [jnp.full((rows, 1), sink_ref[kvh * A_GROUP + g], F32) for g in range(A_GROUP)], axis=0)


def _attn_prompt_kernel(sink_ref, q_ref, k_ref, v_ref, qg_ref, kg_ref, o_ref, nk_ref, nv_ref, kprev, vprev):
    n = pl.program_id(1)
    w = WINDOW

    @pl.when(n == 0)
    def _():
        kprev[...] = jnp.zeros_like(kprev)
        vprev[...] = jnp.zeros_like(vprev)

    q = q_ref[...].astype(F32)
    k = k_ref[...].astype(F32)
    v = v_ref[...]
    qg = qg_ref[...]
    kg = kg_ref[...]
    kn = jnp.concatenate(
        [_head_rms(k[:, h * HEAD_DIM:(h + 1) * HEAD_DIM], kg) for h in range(A_KV_HEADS)], axis=-1)
    knb = kn.astype(BF16)
    kcat = jnp.concatenate([kprev[...], knb], axis=0)
    vcat = jnp.concatenate([vprev[...], v], axis=0)

    i = lax.broadcasted_iota(jnp.int32, (A_GROUP * w, 2 * w), 0) % w
    j = lax.broadcasted_iota(jnp.int32, (A_GROUP * w, 2 * w), 1)
    lo = jnp.where(n == 0, w - 1, -1)
    mask = (j > i) & (j <= i + w) & (j > lo)
    scale = HEAD_DIM ** -0.5
    outs = []
    for h in range(A_KV_HEADS):
        qs = _stack_q(q, qg, h, scale)
        s = _dot_nt(qs, kcat[:, h * HEAD_DIM:(h + 1) * HEAD_DIM])
        s = jnp.where(mask, s, NEG_BIG)
        sink = _sink_col(sink_ref, h, w)
        m = jnp.maximum(jnp.max(s, axis=-1, keepdims=True), sink)
        p = jnp.exp(s - m)
        denom = jnp.sum(p, axis=-1, keepdims=True) + jnp.exp(sink - m)
        o = _dot(p.astype(BF16), vcat[:, h * HEAD_DIM:(h + 1) * HEAD_DIM]) / denom
        outs += [o[g * w:(g + 1) * w] for g in range(A_GROUP)]
    o_ref[...] = jnp.concatenate(outs, axis=-1).astype(o_ref.dtype)

    kprev[...] = knb
    vprev[...] = v

    @pl.when(n == pl.num_programs(1) - 1)
    def _():
        nk_ref[0] = kn
        nv_ref[0] = v.astype(F32)


def _attn_prompt_call(z, sinks, qg, kg, batch, seq):
    nb = seq // WINDOW
    kvw = A_KV_HEADS * HEAD_DIM
    row = lambda b, n, s: b * nb + n
    grid_spec = pltpu.PrefetchScalarGridSpec(
        num_scalar_prefetch=1,
        grid=(batch, nb),
        in_specs=[
            pl.BlockSpec((WINDOW, D_MODEL), lambda b, n, s: (row(b, n, s), Z_AQ // D_MODEL)),
            pl.BlockSpec((WINDOW, kvw), lambda b, n, s: (row(b, n, s), Z_AK // kvw)),
            pl.BlockSpec((WINDOW, kvw), lambda b, n, s: (row(b, n, s), Z_AV // kvw)),
            pl.BlockSpec((1, HEAD_DIM), lambda b, n, s: (0, 0)),
            pl.BlockSpec((1, HEAD_DIM), lambda b, n, s: (0, 0)),
        ],
        out_specs=[
            pl.BlockSpec((WINDOW, D_MODEL), lambda b, n, s: (row(b, n, s), 0)),
            pl.BlockSpec((1, WINDOW, kvw), lambda b, n, s: (b, 0, 0)),
            pl.BlockSpec((1, WINDOW, kvw), lambda b, n, s: (b, 0, 0)),
        ],
        scratch_shapes=[pltpu.VMEM((WINDOW, kvw), BF16), pltpu.VMEM((WINDOW, kvw), BF16)],
    )
    return pl.pallas_call(
        _attn_prompt_kernel,
        grid_spec=grid_spec,
        out_shape=[
            jax.ShapeDtypeStruct((batch * seq, D_MODEL), BF16),
            jax.ShapeDtypeStruct((batch, WINDOW, kvw), F32),
            jax.ShapeDtypeStruct((batch, WINDOW, kvw), F32),
        ],
        compiler_params=_cparams(("arbitrary", "arbitrary")),
        name="attn_prompt",
    )(sinks, z, z, z, qg.reshape(1, HEAD_DIM), kg.reshape(1, HEAD_DIM))


ATTN_S_SEQS = 8


def _attn_sample_kernel(sink_ref, q_ref, k_ref, v_ref, kc_ref, vc_ref, qg_ref, kg_ref, o_ref, nk_ref, nv_ref):
    sb, wb, kvw = kc_ref.shape
    l = q_ref.shape[0] // sb
    rows = sb * l
    q = q_ref[...].astype(F32)
    k = k_ref[...].astype(F32)
    v = v_ref[...]
    qg = qg_ref[...]
    kg = kg_ref[...]
    kn = jnp.concatenate(
        [_head_rms(k[:, h * HEAD_DIM:(h + 1) * HEAD_DIM], kg) for h in range(A_KV_HEADS)], axis=-1)
    nk_ref[...] = kn
    nv_ref[...] = v.astype(F32)
    knb = kn.astype(BF16)
    kc = kc_ref[...].reshape(sb * wb, kvw).astype(BF16)
    vc = vc_ref[...].reshape(sb * wb, kvw).astype(BF16)

    rq = lax.broadcasted_iota(jnp.int32, (A_GROUP * rows, sb * wb), 0) % rows
    cc = lax.broadcasted_iota(jnp.int32, (A_GROUP * rows, sb * wb), 1)
    mask_c = (rq // l == cc // wb) & (cc % wb > rq % l + (wb - WINDOW))
    rq2 = lax.broadcasted_iota(jnp.int32, (A_GROUP * rows, rows), 0) % rows
    cn = lax.broadcasted_iota(jnp.int32, (A_GROUP * rows, rows), 1)
    mask_n = (rq2 // l == cn // l) & (cn % l <= rq2 % l)
    scale = HEAD_DIM ** -0.5
    outs = []
    for h in range(A_KV_HEADS):
        hs = slice(h * HEAD_DIM, (h + 1) * HEAD_DIM)
        qs = _stack_q(q, qg, h, scale)
        s_c = jnp.where(mask_c, _dot_nt(qs, kc[:, hs]), NEG_BIG)
        s_n = jnp.where(mask_n, _dot_nt(qs, knb[:, hs]), NEG_BIG)
        sink = _sink_col(sink_ref, h, rows)
        m = jnp.maximum(jnp.maximum(jnp.max(s_c, axis=-1, keepdims=True),
                                    jnp.max(s_n, axis=-1, keepdims=True)), sink)
        p_c = jnp.exp(s_c - m)
        p_n = jnp.exp(s_n - m)
        denom = (jnp.sum(p_c, axis=-1, keepdims=True) + jnp.sum(p_n, axis=-1, keepdims=True)
                 + jnp.exp(sink - m))
        o = (_dot(p_c.astype(BF16), vc[:, hs]) + _dot(p_n.astype(BF16), v[:, hs])) / denom
        outs += [o[g * rows:(g + 1) * rows] for g in range(A_GROUP)]
    o_ref[...] = jnp.concatenate(outs, axis=-1).astype(o_ref.dtype)


def _attn_sample_call(z, kc, vc, sinks, qg, kg, batch, l):
    kvw = A_KV_HEADS * HEAD_DIM
    sb = ATTN_S_SEQS
    rows = sb * l
    wb = kc.shape[1]
    grid_spec = pltpu.PrefetchScalarGridSpec(
        num_scalar_prefetch=1,
        grid=(batch // sb,),
        in_specs=[
            pl.BlockSpec((rows, D_MODEL), lambda i, s: (i, Z_AQ // D_MODEL)),
            pl.BlockSpec((rows, kvw), lambda i, s: (i, Z_AK // kvw)),
            pl.BlockSpec((rows, kvw), lambda i, s: (i, Z_AV // kvw)),
            pl.BlockSpec((sb, wb, kvw), lambda i, s: (i, 0, 0)),
            pl.BlockSpec((sb, wb, kvw), lambda i, s: (i, 0, 0)),
            pl.BlockSpec((1, HEAD_DIM), lambda i, s: (0, 0)),
            pl.BlockSpec((1, HEAD_DIM), lambda i, s: (0, 0)),
        ],
        out_specs=[
            pl.BlockSpec((rows, D_MODEL), lambda i, s: (i, 0)),
            pl.BlockSpec((rows, kvw), lambda i, s: (i, 0)),
            pl.BlockSpec((rows, kvw), lambda i, s: (i, 0)),
        ],
    )
    return pl.pallas_call(
        _attn_sample_kernel,
        grid_spec=grid_spec,
        out_shape=[
            jax.ShapeDtypeStruct((batch * l, D_MODEL), BF16),
            jax.ShapeDtypeStruct((batch * l, kvw), F32),
            jax.ShapeDtypeStruct((batch * l, kvw), F32),
        ],
        compiler_params=_cparams(("arbitrary",)),
        name="attn_sample",
    )(sinks, z, z, z, kc, vc, qg.reshape(1, HEAD_DIM), kg.reshape(1, HEAD_DIM))


def _ret_tables(chunk, pos0, length):
    h = np.arange(R_HEADS, dtype=np.float64)
    log_gamma = np.log1p(-np.exp2(-5.0 - h))
    idx = np.arange(chunk, dtype=np.float64)
    diff = idx[:, None] - idx[None, :]
    intra = np.where(diff[None] >= 0, np.exp(np.maximum(diff, 0.0)[None] * log_gamma[:, None, None]), 0.0)
    q_decay = np.exp((idx + 1.0)[:, None] * log_gamma[None, :])
    k_decay = np.exp((chunk - 1.0 - idx)[:, None] * log_gamma[None, :])
    c_decay = np.exp(chunk * log_gamma)
    qd = np.repeat(q_decay, R_DK, axis=1)
    kd = np.repeat(k_decay, R_DK, axis=1)
    inv_freq = ROPE_BASE ** (-np.arange(0, R_DK, 2, dtype=np.float64) / R_DK)
    ang = (pos0 + np.arange(length, dtype=np.float64))[:, None] * inv_freq[None, :]
    cos = np.tile(np.concatenate([np.cos(ang), np.cos(ang)], axis=1), (1, R_HEADS))
    sin = np.tile(np.concatenate([-np.sin(ang), np.sin(ang)], axis=1), (1, R_HEADS))
    f = lambda a: jnp.asarray(a, F32)
    return f(intra), f(qd), f(kd), [float(c) for c in c_decay], f(cos), f(sin)


def _rope(x, cos, sin):
    n = x.shape[-1]
    half = R_DK // 2
    lane = lax.broadcasted_iota(jnp.int32, x.shape, 1)
    up = pltpu.roll(x, n - half, axis=1)
    dn = pltpu.roll(x, half, axis=1)
    partner = jnp.where(lane % R_DK < half, up, dn)
    return x * cos + partner * sin


def _head_ln(o, g):
    mu = jnp.mean(o, axis=-1, keepdims=True)
    oc = o - mu
    var = jnp.mean(oc * oc, axis=-1, keepdims=True)
    return oc * lax.rsqrt(var + EPS) * g


def _silu(x):
    return x * jax.nn.sigmoid(x)


def _ret_prompt_kernel(q_ref, k_ref, v_ref, g_ref, cos_ref, sin_ref, intra_ref, qd_ref, kd_ref, ng_ref,
                       o_ref, s_ref, state, *, c_decay):
    n = pl.program_id(1)

    @pl.when(n == 0)
    def _():
        state[...] = jnp.zeros_like(state)

    cos = cos_ref[...]
    sin = sin_ref[...]
    q = _rope(q_ref[...].astype(F32), cos, sin)
    k = _rope(k_ref[...].astype(F32), cos, sin) * (R_DK ** -0.5)
    qb = q.astype(BF16)
    kb = k.astype(BF16)
    qdb = (q * qd_ref[...]).astype(BF16)
    kdb = (k * kd_ref[...]).astype(BF16)
    for h in range(R_HEADS):
        ks = slice(h * R_DK, (h + 1) * R_DK)
        vs = slice(h * R_DV, (h + 1) * R_DV)
        vh = v_ref[:, vs]
        a = _dot_nt(qb[:, ks], kb[:, ks]) * intra_ref[h]
        s_old = state[h]
        o = _dot(a.astype(BF16), vh) + _dot(qdb[:, ks], s_old.astype(BF16))
        state[h] = s_old * c_decay[h] + _dot_tn(kdb[:, ks], vh)
        y = _head_ln(o, ng_ref[:, vs]) * _silu(g_ref[:, vs].astype(F32))
        o_ref[:, vs] = y.astype(o_ref.dtype)

    @pl.when(n == pl.num_programs(1) - 1)
    def _():
        s_ref[0] = state[...]


def _ret_prompt_call(z, ret_norm_g, batch, seq):
    c = R_CHUNK
    nc = seq // c
    intra, qd, kd, c_decay, cos, sin = _ret_tables(c, 0, seq)
    qkw = R_HEADS * R_DK
    row = lambda b, n: b * nc + n
    const2 = lambda b, n: (0, 0)
    return pl.pallas_call(
        functools.partial(_ret_prompt_kernel, c_decay=c_decay),
        grid=(batch, nc),
        in_specs=[
            pl.BlockSpec((c, qkw), lambda b, n: (row(b, n), Z_RQ // qkw)),
            pl.BlockSpec((c, qkw), lambda b, n: (row(b, n), Z_RK // qkw)),
            pl.BlockSpec((c, D_MODEL), lambda b, n: (row(b, n), Z_RV // D_MODEL)),
            pl.BlockSpec((c, D_MODEL), lambda b, n: (row(b, n), Z_RG // D_MODEL)),
            pl.BlockSpec((c, qkw), lambda b, n: (n, 0)),
            pl.BlockSpec((c, qkw), lambda b, n: (n, 0)),
            pl.BlockSpec((R_HEADS, c, c), lambda b, n: (0, 0, 0)),
            pl.BlockSpec((c, qkw), const2),
            pl.BlockSpec((c, qkw), const2),
            pl.BlockSpec((1, D_MODEL), const2),
        ],
        out_specs=[
            pl.BlockSpec((c, D_MODEL), lambda b, n: (row(b, n), 0)),
            pl.BlockSpec((1, R_HEADS, R_DK, R_DV), lambda b, n: (b, 0, 0, 0)),
        ],
        out_shape=[
            jax.ShapeDtypeStruct((batch * seq, D_MODEL), BF16),
            jax.ShapeDtypeStruct((batch, R_HEADS, R_DK, R_DV), F32),
        ],
        scratch_shapes=[pltpu.VMEM((R_HEADS, R_DK, R_DV), F32)],
        compiler_params=_cparams(("arbitrary", "arbitrary")),
        name="ret_prompt",
    )(z, z, z, z, cos, sin, intra, qd, kd, ret_norm_g.reshape(1, D_MODEL))


RET_S_SEQS = 8


def _ret_sample_kernel(q_ref, k_ref, v_ref, g_ref, s0_ref, cos_ref, sin_ref, intra_ref, qd_ref, kd_ref, ng_ref,
                       o_ref, s_ref, *, c_decay, l):
    sb = s0_ref.shape[0]
    rows = sb * l
    cos = cos_ref[...]
    sin = sin_ref[...]
    q = _rope(q_ref[...].astype(F32), cos, sin)
    k = _rope(k_ref[...].astype(F32), cos, sin) * (R_DK ** -0.5)
    qb = q.astype(BF16)
    kb = k.astype(BF16)
    qdb = (q * qd_ref[...]).astype(BF16)
    kdb = (k * kd_ref[...]).astype(BF16)
    for h in range(R_HEADS):
        ks = slice(h * R_DK, (h + 1) * R_DK)
        vs = slice(h * R_DV, (h + 1) * R_DV)
        vh = v_ref[:, vs]
        a = _dot_nt(qb[:, ks], kb[:, ks]) * intra_ref[h]
        o = _dot(a.astype(BF16), vh)
        cross, new_s = [], []
        for b in range(sb):
            rs = slice(b * l, (b + 1) * l)
            s_old = s0_ref[b, h]
            cross.append(_dot(qdb[rs, ks], s_old.astype(BF16)))
            s_ref[b, h] = s_old * c_decay[h] + _dot_tn(kdb[rs, ks], vh[rs])
        o = o + jnp.concatenate(cross, axis=0)
        y = _head_ln(o, ng_ref[:, vs]) * _silu(g_ref[:, vs].astype(F32))
        o_ref[:, vs] = y.astype(o_ref.dtype)


def _ret_sample_call(z, s0, ret_norm_g, batch, l):
    c = math.gcd(l, R_CHUNK)
    assert c == l, "sample step expects a single retention chunk"
    sb = RET_S_SEQS
    rows = sb * l
    intra, qd, kd, c_decay, cos, sin = _ret_tables(c, PAST_LEN, l)
    eye = jnp.eye(sb, dtype=F32)
    intra_bd = jnp.einsum("ab,hij->haibj", eye, intra).reshape(R_HEADS, rows, rows)
    tile = lambda t: jnp.tile(t, (sb, 1))
    qkw = R_HEADS * R_DK
    const2 = lambda i: (0, 0)
    return pl.pallas_call(
        functools.partial(_ret_sample_kernel, c_decay=c_decay, l=l),
        grid=(batch // sb,),
        in_specs=[
            pl.BlockSpec((rows, qkw), lambda i: (i, Z_RQ // qkw)),
            pl.BlockSpec((rows, qkw), lambda i: (i, Z_RK // qkw)),
            pl.BlockSpec((rows, D_MODEL), lambda i: (i, Z_RV // D_MODEL)),
            pl.BlockSpec((rows, D_MODEL), lambda i: (i, Z_RG // D_MODEL)),
            pl.BlockSpec((sb, R_HEADS, R_DK, R_DV), lambda i: (i, 0, 0, 0)),
            pl.BlockSpec((rows, qkw), const2),
            pl.BlockSpec((rows, qkw), const2),
            pl.BlockSpec((R_HEADS, rows, rows), lambda i: (0, 0, 0)),
            pl.BlockSpec((rows, qkw), const2),
            pl.BlockSpec((rows, qkw), const2),
            pl.BlockSpec((1, D_MODEL), const2),
        ],
        out_specs=[
            pl.BlockSpec((rows, D_MODEL), lambda i: (i, 0)),
            pl.BlockSpec((sb, R_HEADS, R_DK, R_DV), lambda i: (i, 0, 0, 0)),
        ],
        out_shape=[
            jax.ShapeDtypeStruct((batch * l, D_MODEL), BF16),
            jax.ShapeDtypeStruct(s0.shape, F32),
        ],
        compiler_params=_cparams(("arbitrary",)),
        name="ret_sample",
    )(z, z, z, z, s0, tile(cos), tile(sin), intra_bd, tile(qd), tile(kd), ret_norm_g.reshape(1, D_MODEL))


def _gelu(x):
    return jax.nn.gelu(x, approximate=True)


def _route_rows(logits, carry):
    tm = logits.shape[0]
    lane = lax.broadcasted_iota(jnp.int32, logits.shape, 1).astype(F32)
    work = logits
    sel = jnp.zeros(logits.shape, F32)
    vals, idxs = [], []
    for _ in range(TOP_K):
        m = jnp.max(work, axis=-1, keepdims=True)
        idx = jnp.min(jnp.where(work == m, lane, float(ROUTER_PAD)), axis=-1, keepdims=True)
        hit = lane == idx
        vals.append(m)
        idxs.append(idx)
        sel = jnp.where(hit, 1.0, sel)
        work = jnp.where(hit, -3e38, work)
    ex = [jnp.exp(v - vals[0]) for v in vals]
    den = ex[0] + ex[1] + ex[2] + ex[3]
    r = lax.broadcasted_iota(jnp.int32, (tm, tm), 0)
    c = lax.broadcasted_iota(jnp.int32, (tm, tm), 1)
    before = jnp.where(c < r, 1.0, 0.0).astype(BF16)
    rank = _dot(before, sel.astype(BF16)) + carry
    route = jnp.zeros(logits.shape, F32)
    for k in range(TOP_K):
        route = jnp.where(lane == float(k), ex[k] / den, route)
        route = jnp.where(lane == float(TOP_K + k), idxs[k], route)
        rk = jnp.sum(jnp.where(lane == idxs[k], rank, 0.0), axis=-1, keepdims=True)
        route = jnp.where(lane == float(2 * TOP_K + k), rk, route)
    return route, carry + jnp.sum(sel, axis=0, keepdims=True)


def _merge_kernel(*refs, emit_cv):
    (x_ref, oa_ref, ob_ref, cu_ref, cv_ref, ga_ref, gb_ref, gc_ref, g1_ref, sh2_ref, sc2_ref,
     bg_ref, lnf_ref, gmg_ref, gmb_ref, mix_ref, mixb_ref, wout_ref, rwh_ref, rwl_ref, rb_ref) = refs[:21]
    cnt_ref = refs[21]
    if emit_cv:
        xo_ref, h2_ref, route_ref, cnto_ref, cvo_ref, carry = refs[22:]
    else:
        xo_ref, h2_ref, route_ref, cnto_ref, carry = refs[22:]
    first = (pl.program_id(0) == 0) & (pl.program_id(1) == 0)

    @pl.when(first)
    def _():
        carry[...] = cnt_ref[...]

    x = x_ref[...]
    gbk, rb, d = x.shape
    tm = gbk * rb
    cw = C_CHUNK
    gw = d // C_GROUPS

    cv = _gelu(cv_ref[...].astype(F32))
    mu = jnp.mean(cv, axis=-1, keepdims=True)
    cvc = cv - mu
    var = jnp.mean(cvc * cvc, axis=-1, keepdims=True)
    cv = cvc * lax.rsqrt(var + EPS) * gmg_ref[...] + gmb_ref[...]
    if emit_cv:
        cvo_ref[...] = cv
    cvb = cv.astype(BF16)
    mixed_rows = []
    for c in range(tm // cw):
        rs = slice(c * cw, (c + 1) * cw)
        cols = [_dot(mix_ref[g], cvb[rs, g * gw:(g + 1) * gw]) for g in range(C_GROUPS)]
        mixed_rows.append(jnp.concatenate(cols, axis=-1) + mixb_ref[...])
    mixed = jnp.concatenate(mixed_rows, axis=0)
    oc = _gelu(cu_ref[...].astype(F32)) * mixed

    bg = bg_ref[...]
    ga = jax.nn.sigmoid(ga_ref[...].astype(F32) + bg[0:1])
    gb = jax.nn.sigmoid(gb_ref[...].astype(F32) + bg[1:2])
    gc = jax.nn.sigmoid(gc_ref[...].astype(F32) + bg[2:3])
    merged = ga * oa_ref[...].astype(F32) + gb * ob_ref[...].astype(F32) + gc * oc
    y = _dot(merged.astype(BF16), wout_ref[...])
    x = x + g1_ref[...] * y.reshape(gbk, rb, d)
    xo_ref[...] = x

    h2 = (_rms(x) * lnf_ref[...] * (1.0 + sc2_ref[...]) + sh2_ref[...]).reshape(tm, d)
    h2b = h2.astype(BF16)
    h2_ref[0], h2_ref[1] = _pack_rows(h2)
    h2l = (h2 - h2b.astype(F32)).astype(BF16)
    rwh = rwh_ref[...]
    logits = _dot(h2b, rwh) + _dot(h2l, rwh) + _dot(h2b, rwl_ref[...]) + rb_ref[...]
    route, counts = _route_rows(logits, carry[...])
    route_ref[...] = route
    carry[...] = counts
    cnto_ref[...] = counts


def _merge_call(x, z, oa, ob, mod, b_gate, ln_ffn_g, gm_ln_g, gm_ln_b, mix, mixb, w_out_bf, rw_hi, rw_lo, rb,
                counts, emit_cv):
    g, r, d = x.shape
    gb, rb_ = _group_blocks(g, r)
    tm = gb * rb_
    nj = r // rb_
    t = g * r
    xspec = pl.BlockSpec((gb, rb_, d), lambda i, j: (i, j, 0))
    rows = lambda col: pl.BlockSpec((tm, d), lambda i, j: (i * nj + j, col))
    mspec = lambda col: pl.BlockSpec((gb, 1, d), lambda i, j: (i, 0, col))
    const = lambda shape: pl.BlockSpec(shape, lambda i, j: (0,) * len(shape))
    in_specs = [
        xspec, rows(0), rows(0),
        rows(Z_CU // d), rows(Z_CV // d), rows(Z_GA // d), rows(Z_GB // d), rows(Z_GC // d),
        mspec(2), mspec(3), mspec(4),
        const((3, d)), const((1, 1, d)), const((1, d)), const((1, d)),
        const((C_GROUPS, C_CHUNK, C_CHUNK)), const((C_CHUNK, d)),
        const((d, d)), const((d, ROUTER_PAD)), const((d, ROUTER_PAD)), const((1, ROUTER_PAD)),
        const((1, ROUTER_PAD)),
    ]
    out_specs = [xspec, pl.BlockSpec((2, tm, PACK_W), lambda i, j: (0, i * nj + j, 0)),
                 pl.BlockSpec((tm, ROUTER_PAD), lambda i, j: (i * nj + j, 0)), const((1, ROUTER_PAD))]
    out_shape = [jax.ShapeDtypeStruct(x.shape, F32), jax.ShapeDtypeStruct((2, t, PACK_W), U32),
                 jax.ShapeDtypeStruct((t, ROUTER_PAD), F32), jax.ShapeDtypeStruct((1, ROUTER_PAD), F32)]
    if emit_cv:
        out_specs.append(rows(0))
        out_shape.append(jax.ShapeDtypeStruct((t, d), F32))
    return pl.pallas_call(
        functools.partial(_merge_kernel, emit_cv=emit_cv),
        grid=(g // gb, nj), in_specs=in_specs, out_specs=out_specs, out_shape=out_shape,
        scratch_shapes=[pltpu.VMEM((1, ROUTER_PAD), F32)],
        compiler_params=_cparams(("arbitrary", "arbitrary")),
        name="merge",
    )(x, oa, ob, z, z, z, z, z, mod, mod, mod,
      b_gate.reshape(3, d), ln_ffn_g.reshape(1, 1, d), gm_ln_g.reshape(1, d), gm_ln_b.reshape(1, d),
      mix, mixb, w_out_bf, rw_hi, rw_lo, rb, counts)


def _moe_kernel(be_ref, nu_ref, x_ref, w1_ref, b1_ref, w2_ref, b2_ref, o_ref, w1b, w2b):
    i = pl.program_id(0)
    e = be_ref[i]
    prev = be_ref[jnp.maximum(i - 1, 0)]

    @pl.when((i == 0) | (e != prev))
    def _():
        w1b[...] = w1_ref[0].astype(BF16)
        w2b[...] = w2_ref[0].astype(BF16)

    @pl.when(i < nu_ref[0])
    def _():
        xb = _unpack_rows(x_ref[0], x_ref[1]).astype(BF16)
        hdn = _dot(xb, w1b[...]) + b1_ref[0]
        g = jnp.minimum(hdn[:, :D_FF], SWIGLU_LIMIT)
        up = jnp.clip(hdn[:, D_FF:], -SWIGLU_LIMIT, SWIGLU_LIMIT)
        act = (up + 1.0) * (g * jax.nn.sigmoid(g * SWIGLU_ALPHA))
        o_ref[0], o_ref[1] = _pack_rows(_dot(act.astype(BF16), w2b[...]) + b2_ref[0])


def _moe_call(xb, block_e, n_used, w1, b1, w2, b2):
    _, n_rows, _ = xb.shape
    nblk = n_rows // MOE_ROWS
    ne, d, f2 = w1.shape
    last = lambda i, be, nu: jnp.minimum(i, nu[0] - 1)
    xspec = pl.BlockSpec((2, MOE_ROWS, PACK_W), lambda i, be, nu: (0, last(i, be, nu), 0))
    grid_spec = pltpu.PrefetchScalarGridSpec(
        num_scalar_prefetch=2,
        grid=(nblk,),
        in_specs=[
            xspec,
            pl.BlockSpec((1, d, f2), lambda i, be, nu: (be[i], 0, 0)),
            pl.BlockSpec((1, 1, f2), lambda i, be, nu: (be[i], 0, 0)),
            pl.BlockSpec((1, f2 // 2, d), lambda i, be, nu: (be[i], 0, 0)),
            pl.BlockSpec((1, 1, d), lambda i, be, nu: (be[i], 0, 0)),
        ],
        out_specs=xspec,
        scratch_shapes=[pltpu.VMEM((d, f2), BF16), pltpu.VMEM((f2 // 2, d), BF16)],
    )
    return pl.pallas_call(
        _moe_kernel,
        grid_spec=grid_spec,
        out_shape=jax.ShapeDtypeStruct(xb.shape, U32),
        compiler_params=_cparams(("arbitrary",)),
        name="moe",
    )(block_e, n_used, xb, w1, b1.reshape(ne, 1, f2), w2, b2.reshape(ne, 1, d))


def _sc_mesh():
    return plsc.VectorSubcoreMesh(core_axis_name="core", subcore_axis_name="subcore")


def _sc_scatter_rows(x, idx, n_out):
    t, c = x.shape
    kk = idx.shape[0]

    @pl.kernel(out_type=jax.ShapeDtypeStruct((n_out, c), x.dtype), mesh=_sc_mesh(), scratch_types=[])
    def scatter(x_hbm, i_hbm, o_hbm):
        def body(x_vmem, i_vmem):
            for k in range(kk):
                pltpu.sync_copy(x_vmem, o_hbm.at[i_vmem.at[k]])

        pltpu.emit_pipeline(
            body,
            grid=(t // SC_WINDOW,),
            in_specs=[pl.BlockSpec((SC_WINDOW, c), index_map=lambda i: (i, 0)),
                      pl.BlockSpec((kk, SC_WINDOW), index_map=lambda i: (0, i))],
            out_specs=[],
            core_axis_name=("core", "subcore"),
            dimension_semantics=(pltpu.PARALLEL,),
        )(x_hbm, i_hbm)

    return scatter(x, idx)


def _sc_gather_rows(data, idx):
    n = idx.shape[0]
    c = data.shape[1]

    @pl.kernel(out_type=jax.ShapeDtypeStruct((n, c), data.dtype), mesh=_sc_mesh(), scratch_types=[])
    def gather(x_hbm, i_hbm, o_hbm):
        def body(i_vmem, o_vmem):
            pltpu.sync_copy(x_hbm.at[i_vmem.at[0]], o_vmem)

        pltpu.emit_pipeline(
            body,
            grid=(n // SC_WINDOW,),
            in_specs=[pl.BlockSpec((1, SC_WINDOW), index_map=lambda i: (0, i))],
            out_specs=[pl.BlockSpec((SC_WINDOW, c), index_map=lambda i: (i, 0))],
            core_axis_name=("core", "subcore"),
            dimension_semantics=(pltpu.PARALLEL,),
        )(i_hbm, o_hbm)

    return gather(data, idx.reshape(1, n))


def _plan(route, counts):
    t = route.shape[0]
    n_blocks = -(-t * TOP_K // MOE_ROWS) + N_EXPERTS
    e4 = route[:, TOP_K:2 * TOP_K].astype(jnp.int32)
    r4 = route[:, 2 * TOP_K:3 * TOP_K].astype(jnp.int32)
    cnt = counts[0, :N_EXPERTS].astype(jnp.int32)
    padded = (cnt + MOE_ROWS - 1) // MOE_ROWS * MOE_ROWS
    pend = jnp.cumsum(padded)
    pstart = pend - padded
    onehot = e4[:, :, None] == jnp.arange(N_EXPERTS, dtype=jnp.int32)[None, None, :]
    dest = (r4 + jnp.sum(jnp.where(onehot, pstart[None, None, :], 0), axis=-1)).T
    blk = jnp.arange(n_blocks, dtype=jnp.int32) * MOE_ROWS
    block_e = jnp.sum((blk[:, None] >= pend[None, :]).astype(jnp.int32), axis=-1)
    n_used = (pend[-1] // MOE_ROWS).astype(jnp.int32).reshape(1)
    block_e = jnp.minimum(block_e, jnp.sum((pend < pend[-1]).astype(jnp.int32)))
    return dest, block_e.astype(jnp.int32), n_used, n_blocks * MOE_ROWS


def _moe(h2p, route, counts, w1, b1, w2, b2):
    _, t, pw = h2p.shape
    dest, block_e, n_used, n_rows = _plan(route, counts)
    idx_s = jnp.concatenate([dest, dest + n_rows], axis=1)
    xb = _sc_scatter_rows(h2p.reshape(2 * t, pw), idx_s, 2 * n_rows).reshape(2, n_rows, pw)
    yb = _moe_call(xb, block_e, n_used, w1, b1, w2, b2)
    idx_g = jnp.concatenate([dest.reshape(-1), dest.reshape(-1) + n_rows])
    y4 = _sc_gather_rows(yb.reshape(2 * n_rows, pw), idx_g)
    return y4.reshape(2, TOP_K, t, pw)


def _reorder_w_in(w):
    o = _ORIG
    seg = lambda a, n: w[:, a:a + n]
    d = D_MODEL
    parts = [seg(o["aq"], d), seg(o["rv"], d), seg(o["rg"], d), seg(o["cu"], d), seg(o["cv"], d),
             seg(o["mg"], d), seg(o["mg"] + d, d), seg(o["mg"] + 2 * d, d),
             seg(o["rq"], 512), seg(o["rk"], 512), seg(o["ak"], 256), seg(o["av"], 256)]
    w = jnp.concatenate(parts, axis=1).astype(BF16)
    return w.reshape(D_MODEL, IN_CHUNKS, D_IN // IN_CHUNKS).transpose(1, 0, 2)


def _gmlp_tables(ws, bs, chunk_len, rows):
    causal = jnp.tril(jnp.ones((C_CHUNK, C_CHUNK), dtype=bool))
    w = jnp.where(causal[None], ws, 0.0)[:, :chunk_len, :chunk_len]
    reps = rows // chunk_len
    eye = jnp.eye(reps, dtype=F32)
    mix = jnp.einsum("ab,gts->gatbs", eye, w).reshape(C_GROUPS, rows, rows).astype(BF16)
    b = jnp.tile(bs[:, :chunk_len].T, (reps, 1))
    mixb = jnp.repeat(b, D_MODEL // C_GROUPS, axis=1)
    return mix, mixb


def kernel(x_prompt, x_sample, c_prompt, c_sample, cache_attn_k, cache_attn_v, state_retention, ln_mix_g, ln_ffn_g, w_ada, b_ada, w_in, b_gate, q_norm_g, k_norm_g, attn_sinks, ret_norm_g, gm_ln_g, gm_ln_b, gm_ws, gm_bs, w_out, router_w, router_b, moe_w1, moe_b1, moe_w2, moe_b2):
    bp, lp, d = x_prompt.shape
    bs, ls, _ = x_sample.shape
    tp, ts = bp * lp, bs * ls
    kvw = A_KV_HEADS * HEAD_DIM
    wb = cache_attn_k.shape[2]

    mod_all = _ada_call(jnp.concatenate([c_prompt, c_sample], axis=0), w_ada, b_ada)

    xp, xs = x_prompt, x_sample
    y4 = route = None
    mod_p = mod_s = None
    pk, pv, ps, sk, sv, ss, sg = [], [], [], [], [], [], []
    for l in range(DEPTH):
        prev_mod_p, prev_mod_s = mod_p, mod_s
        mod_p = mod_all[l, :bp].reshape(bp, 1, 6 * d)
        mod_s = mod_all[l, bp:].reshape(bs, 1, 6 * d)
        w_in_bf = _reorder_w_in(w_in[l])
        w_out_bf = w_out[l].astype(BF16)
        rw = jnp.pad(router_w[l], ((0, 0), (0, ROUTER_PAD - N_EXPERTS)))
        rw_hi, rw_lo = _split_bf16(rw)
        rb = jnp.pad(router_b[l], (0, ROUTER_PAD - N_EXPERTS), constant_values=NEG_BIG).reshape(1, ROUTER_PAD)

        if l == 0:
            zp = _inproj_call(xp, mod_p, ln_mix_g[l], w_in_bf)
            zs = _inproj_call(xs, mod_s, ln_mix_g[l], w_in_bf)
        else:
            zp, xp = _inproj_call(xp, mod_p, ln_mix_g[l], w_in_bf, moe_out=(y4, route, prev_mod_p, 0))
            zs, xs = _inproj_call(xs, mod_s, ln_mix_g[l], w_in_bf, moe_out=(y4, route, prev_mod_s, tp // ROW_TILE))
        oa_p, k_p, v_p = _attn_prompt_call(zp, attn_sinks[l], q_norm_g[l], k_norm_g[l], bp, lp)
        oa_s, k_s, v_s = _attn_sample_call(zs, cache_attn_k[l].reshape(bs, wb, kvw),
                                           cache_attn_v[l].reshape(bs, wb, kvw),
                                           attn_sinks[l], q_norm_g[l], k_norm_g[l], bs, ls)
        ob_p, s_p = _ret_prompt_call(zp, ret_norm_g[l], bp, lp)
        ob_s, s_s = _ret_sample_call(zs, state_retention[l], ret_norm_g[l], bs, ls)

        mix_p, mixb_p = _gmlp_tables(gm_ws[l], gm_bs[l], C_CHUNK, C_CHUNK)
        mix_s, mixb_s = _gmlp_tables(gm_ws[l], gm_bs[l], ls, C_CHUNK)
        common = (b_gate[l], ln_ffn_g[l], gm_ln_g[l], gm_ln_b[l])
        tail = (w_out_bf, rw_hi, rw_lo, rb)
        zero_counts = jnp.zeros((1, ROUTER_PAD), F32)
        xp, h2_p, rt_p, cnt_p = _merge_call(xp, zp, oa_p, ob_p, mod_p, *common, mix_p, mixb_p, *tail,
                                            zero_counts, emit_cv=False)
        xs, h2_s, rt_s, cnt, cv_s = _merge_call(xs, zs, oa_s, ob_s, mod_s, *common, mix_s, mixb_s, *tail,
                                                cnt_p, emit_cv=True)
        route = jnp.concatenate([rt_p, rt_s], axis=0)
        y4 = _moe(jnp.concatenate([h2_p, h2_s], axis=1), route, cnt, moe_w1[l], moe_b1[l], moe_w2[l], moe_b2[l])

        pk.append(k_p.reshape(bp, WINDOW, A_KV_HEADS, HEAD_DIM))
        pv.append(v_p.reshape(bp, WINDOW, A_KV_HEADS, HEAD_DIM))
        ps.append(s_p)
        sk.append(k_s.reshape(bs, ls, A_KV_HEADS, HEAD_DIM))
        sv.append(v_s.reshape(bs, ls, A_KV_HEADS, HEAD_DIM))
        ss.append(s_s)
        sg.append(cv_s.reshape(bs, ls, d))

    xp = _resid_call(xp, y4, route, mod_p, 0)
    xs = _resid_call(xs, y4, route, mod_s, tp // ROW_TILE)
    return (xp, xs, jnp.stack(pk), jnp.stack(pv), jnp.stack(ps), jnp.stack(sk), jnp.stack(sv),
            jnp.stack(ss), jnp.stack(sg))
```

```python
import functools
import math

import numpy as np
import jax
import jax.numpy as jnp
from jax import lax
from jax.experimental import pallas as pl
from jax.experimental.pallas import tpu as pltpu
from jax.experimental.pallas import tpu_sc as plsc

F32 = jnp.float32
BF16 = jnp.bfloat16
U32 = jnp.uint32

D_MODEL = 1024
DEPTH = 4
PAST_LEN = 8192
HEAD_DIM = 64
A_Q_HEADS = 16
A_KV_HEADS = 4
A_GROUP = 4
WINDOW = 128
R_HEADS = 8
R_DK = 64
R_DV = 128
R_CHUNK = 128
ROPE_BASE = 10000.0
C_CHUNK = 128
C_GROUPS = 8
N_EXPERTS = 32
TOP_K = 4
D_FF = D_MODEL
SWIGLU_LIMIT = 7.0
SWIGLU_ALPHA = 1.702
EPS = 1e-6

Z_AQ, Z_RV, Z_RG, Z_CU, Z_CV, Z_GA, Z_GB, Z_GC = 0, 1024, 2048, 3072, 4096, 5120, 6144, 7168
Z_RQ, Z_RK, Z_AK, Z_AV = 8192, 8704, 9216, 9472
D_IN = 9728
IN_CHUNKS = 4
_ORIG = dict(aq=0, ak=1024, av=1280, rq=1536, rk=2048, rv=2560, rg=3584, cu=4608, cv=5632, mg=6656)

ROW_TILE = 512
MOE_ROWS = 256
ROUTER_PAD = 128
PACK_W = D_MODEL // 4
SC_WINDOW = 128
HI_MASK = 0xFFFF0000
VMEM_LIMIT = 56 * 1024 * 1024
NEG_BIG = -1e30


def _cparams(sem):
    return pltpu.CompilerParams(dimension_semantics=sem, vmem_limit_bytes=VMEM_LIMIT)


def _split_bf16(x):
    hi = x.astype(BF16)
    lo = (x - hi.astype(F32)).astype(BF16)
    return hi, lo


def _dot(a, b):
    return jnp.dot(a, b, preferred_element_type=F32)


def _dot_nt(a, b):
    return lax.dot_general(a, b, (((1,), (1,)), ((), ())), preferred_element_type=F32)


def _dot_tn(a, b):
    return lax.dot_general(a, b, (((0,), (0,)), ((), ())), preferred_element_type=F32)


def _ada_kernel(c_ref, w_ref, b_ref, o_ref):
    c = c_ref[...]
    s_hi, s_lo = _split_bf16(c * jax.nn.sigmoid(c))
    w_hi, w_lo = _split_bf16(w_ref[0])
    acc = _dot(s_hi, w_hi) + _dot(s_lo, w_hi) + _dot(s_hi, w_lo)
    o_ref[0] = acc + b_ref[0]


def _ada_call(c_all, w_ada, b_ada):
    depth, d, n = w_ada.shape
    m = c_all.shape[0]
    tn = 1024
    return pl.pallas_call(
        _ada_kernel,
        grid=(depth, n // tn),
        in_specs=[
            pl.BlockSpec((m, d), lambda l, j: (0, 0)),
            pl.BlockSpec((1, d, tn), lambda l, j: (l, 0, j)),
            pl.BlockSpec((1, 1, tn), lambda l, j: (l, 0, j)),
        ],
        out_specs=pl.BlockSpec((1, m, tn), lambda l, j: (l, 0, j)),
        out_shape=jax.ShapeDtypeStruct((depth, m, n), F32),
        compiler_params=_cparams(("arbitrary", "arbitrary")),
        name="ada",
    )(c_all, w_ada, b_ada.reshape(depth, 1, n))


def _pack_rows(y):
    bits = pltpu.bitcast(y.astype(BF16).astype(F32), U32)
    q = [bits[:, i * PACK_W:(i + 1) * PACK_W] for i in range(4)]
    return (q[0] >> 16) | q[1], (q[2] >> 16) | q[3]


def _unpack_rows(a, b):
    f = lambda w: pltpu.bitcast(w, F32)
    return jnp.concatenate([f(a << 16), f(a & jnp.uint32(HI_MASK)), f(b << 16), f(b & jnp.uint32(HI_MASK))], axis=-1)


def _combine(y4_ref, route_ref):
    route = route_ref[...]
    acc = None
    for k in range(TOP_K):
        term = route[:, k:k + 1] * _unpack_rows(y4_ref[0, k], y4_ref[1, k])
        acc = term if acc is None else acc + term
    return acc


def _rms(x):
    return x * lax.rsqrt(jnp.mean(x * x, axis=-1, keepdims=True) + EPS)


def _inproj_kernel(*refs, has_resid):
    if has_resid:
        x_ref, y4_ref, route_ref, g2_ref, sh_ref, sc_ref, lng_ref, w_ref, z_ref, xo_ref, hb = refs
    else:
        x_ref, sh_ref, sc_ref, lng_ref, w_ref, z_ref, hb = refs
    c = pl.program_id(2)

    @pl.when(c == 0)
    def _():
        x = x_ref[...]
        if has_resid:
            x = x + g2_ref[...] * _combine(y4_ref, route_ref).reshape(x.shape)
            xo_ref[...] = x
        h = _rms(x) * lng_ref[...] * (1.0 + sc_ref[...]) + sh_ref[...]
        hb[...] = h.reshape(hb.shape).astype(BF16)

    h = hb[...]
    n = w_ref.shape[2]
    for c0 in range(0, n, 1024):
        c1 = min(c0 + 1024, n)
        z_ref[:, c0:c1] = _dot(h, w_ref[c, :, c0:c1]).astype(BF16)


def _group_blocks(g, r):
    if r >= ROW_TILE:
        return 1, ROW_TILE
    return ROW_TILE // r, r


def _moe_out_specs(tm, nj, tile_off):
    return [pl.BlockSpec((2, TOP_K, tm, PACK_W), lambda i, j, *_: (0, 0, tile_off + i * nj + j, 0)),
            pl.BlockSpec((tm, ROUTER_PAD), lambda i, j, *_: (tile_off + i * nj + j, 0))]


def _inproj_call(x, mod, ln_g, w_bf, moe_out=None):
    g, r, d = x.shape
    gb, rb = _group_blocks(g, r)
    nj = r // rb
    nc, _, cw = w_bf.shape
    grid = (g // gb, nj, nc)
    xspec = pl.BlockSpec((gb, rb, d), lambda i, j, c: (i, j, 0))
    mspec = lambda col: pl.BlockSpec((gb, 1, d), lambda i, j, c: (i, 0, col))
    in_specs, args = [xspec], [x]
    if moe_out is not None:
        y4, route, mod_prev, tile_off = moe_out
        in_specs += _moe_out_specs(gb * rb, nj, tile_off) + [mspec(5)]
        args += [y4, route, mod_prev]
    in_specs += [mspec(0), mspec(1), pl.BlockSpec((1, 1, d), lambda i, j, c: (0, 0, 0)),
                 pl.BlockSpec((nc, d, cw), lambda i, j, c: (0, 0, 0), pipeline_mode=pl.Buffered(1))]
    args += [mod, mod, ln_g.reshape(1, 1, d), w_bf]
    zspec = pl.BlockSpec((gb * rb, cw), lambda i, j, c: (i * nj + j, c))
    zshape = jax.ShapeDtypeStruct((g * r, nc * cw), BF16)
    if moe_out is not None:
        out_specs, out_shape = [zspec, xspec], [zshape, jax.ShapeDtypeStruct(x.shape, F32)]
    else:
        out_specs, out_shape = zspec, zshape
    return pl.pallas_call(
        functools.partial(_inproj_kernel, has_resid=moe_out is not None),
        grid=grid, in_specs=in_specs, out_specs=out_specs, out_shape=out_shape,
        scratch_shapes=[pltpu.VMEM((gb * rb, d), BF16)],
        compiler_params=_cparams(("arbitrary", "arbitrary", "arbitrary")),
        name="inproj",
    )(*args)


def _resid_kernel(x_ref, y4_ref, route_ref, g2_ref, o_ref):
    x = x_ref[...]
    o_ref[...] = x + g2_ref[...] * _combine(y4_ref, route_ref).reshape(x.shape)


def _resid_call(x, y4, route, mod, tile_off):
    g, r, d = x.shape
    gb, rb = _group_blocks(g, r)
    nj = r // rb
    xspec = pl.BlockSpec((gb, rb, d), lambda i, j: (i, j, 0))
    return pl.pallas_call(
        _resid_kernel,
        grid=(g // gb, nj),
        in_specs=[xspec] + _moe_out_specs(gb * rb, nj, tile_off)
        + [pl.BlockSpec((gb, 1, d), lambda i, j: (i, 0, 5))],
        out_specs=xspec,
        out_shape=jax.ShapeDtypeStruct(x.shape, F32),
        compiler_params=_cparams(("arbitrary", "arbitrary")),
        name="resid",
    )(x, y4, route, mod)


def _head_rms(x, g):
    return x * lax.rsqrt(jnp.mean(x * x, axis=-1, keepdims=True) + EPS) * g


def _q_col(kvh, g):
    p, kv_odd = divmod(kvh, 2)
    j, g_odd = divmod(g, 2)
    tile = p * 4 + (kv_odd ^ g_odd) * 2 + j
    return tile * 2 * HEAD_DIM + g_odd * HEAD_DIM


def _stack_q(q, qg, kvh, scale):
    parts = []
    for g in range(A_GROUP):
        c0 = _q_col(kvh, g)
        parts.append(_head_rms(q[:, c0:c0 + HEAD_DIM], qg) * scale)
    return jnp.concatenate(parts, axis=0).astype(BF16)


def _half_mats():
    r = lax.broadcasted_iota(jnp.int32, (2 * HEAD_DIM, 2 * HEAD_DIM), 0)
    c = lax.broadcasted_iota(jnp.int32, (2 * HEAD_DIM, 2 * HEAD_DIM), 1)
    seg = jnp.where(r // HEAD_DIM == c // HEAD_DIM, 1.0, 0.0).astype(BF16)
    swap = jnp.where((r + HEAD_DIM) % (2 * HEAD_DIM) == c, 1.0, 0.0).astype(BF16)
    return seg, swap


def _pair_rms(x, g2, seg):
    outs = []
    for t in range(x.shape[1] // (2 * HEAD_DIM)):
        xt = x[:, t * 2 * HEAD_DIM:(t + 1) * 2 * HEAD_DIM]
        hi, lo = _split_bf16(xt * xt)
        ss = _dot(hi, seg) + _dot(lo, seg)
        outs.append(xt * lax.rsqrt(ss * (1.0 / HEAD_DIM) + EPS) * g2)
    return jnp.concatenate(outs, axis=-1)


def _sink_col(sink_ref, kvh, rows):
    return jnp.concatenate(
        [jnp.full((rows, 1), sink_ref[kvh * A_GROUP + g], F32) for g in range(A_GROUP)], axis=0)


def _attn_prompt_kernel(sink_ref, q_ref, k_ref, v_ref, qg_ref, kg_ref, o_ref, nk_ref, nv_ref, kprev, vprev):
    n = pl.program_id(1)
    w = WINDOW

    @pl.when(n == 0)
    def _():
        kprev[...] = jnp.zeros_like(kprev)
        vprev[...] = jnp.zeros_like(vprev)

    seg, swap = _half_mats()
    tw = 2 * HEAD_DIM
    scale = HEAD_DIM ** -0.5
    v = v_ref[...]
    qn = (_pair_rms(q_ref[...].astype(F32), qg_ref[...], seg) * scale).astype(BF16)
    kn = _pair_rms(k_ref[...].astype(F32), kg_ref[...], seg)
    knb = kn.astype(BF16)
    kcat = jnp.concatenate([kprev[...], knb], axis=0)
    vcat = jnp.concatenate([vprev[...], v], axis=0)

    row = lax.broadcasted_iota(jnp.int32, (2 * w, 2 * w), 0)
    i = row % w
    j = lax.broadcasted_iota(jnp.int32, (2 * w, 2 * w), 1)
    lo = jnp.where(n == 0, w - 1, -1)
    mask = (j > i) & (j <= i + w) & (j > lo)
    sink_top = (j == 0) & (row < w)
    sink_bot = (j == 0) & (row >= w)
    lane_kv = lax.broadcasted_iota(jnp.int32, (2 * w, tw), 1)
    key_kv = lax.broadcasted_iota(jnp.int32, (2 * w, tw), 0)
    first_o = lax.broadcasted_iota(jnp.int32, (w, tw), 1) < HEAD_DIM
    scores, values = [], []
    for p in range(A_KV_HEADS // 2):
        kp = kcat[:, p * tw:(p + 1) * tw]
        vp = vcat[:, p * tw:(p + 1) * tw]
        kv_tiles = ((kp, vp), (_dot(kp, swap).astype(BF16), _dot(vp, swap).astype(BF16)))
        for variant, (kk, vv) in enumerate(kv_tiles):
            t0 = p * 4 + variant * 2
            qs = jnp.concatenate([qn[:, t0 * tw:(t0 + 1) * tw], qn[:, (t0 + 1) * tw:(t0 + 2) * tw]], axis=0)
            for half in range(2):
                keep = (lane_kv < HEAD_DIM) if half == 0 else (lane_kv >= HEAD_DIM)
                kh = jnp.where(keep, kk, jnp.zeros_like(kk))
                values.append(jnp.where(keep & (key_kv > 0), vv, jnp.zeros_like(vv)))
                kvh = 2 * p + (half ^ variant)
                s = jnp.where(mask, _dot_nt(qs, kh), NEG_BIG)
                s = jnp.where(sink_top, sink_ref[kvh * A_GROUP + half], s)
                scores.append(jnp.where(sink_bot, sink_ref[kvh * A_GROUP + 2 + half], s))
    s_all = jnp.concatenate(scores, axis=0)
    p_all = jnp.exp(s_all - jnp.max(s_all, axis=-1, keepdims=True))
    inv = 1.0 / jnp.sum(p_all, axis=-1, keepdims=True)
    p_all = p_all.astype(BF16)
    outs = [_dot(p_all[c * 2 * w:(c + 1) * 2 * w], values[c]) * inv[c * 2 * w:(c + 1) * 2 * w]
            for c in range(len(values))]
    for p in range(A_KV_HEADS // 2):
        o_a = outs[4 * p] + outs[4 * p + 1]
        o_b = outs[4 * p + 2] + outs[4 * p + 3]
        for jr in range(2):
            a = o_a[jr * w:(jr + 1) * w]
            b = o_b[jr * w:(jr + 1) * w]
            c_even = (2 * p * A_GROUP + 2 * jr) * HEAD_DIM
            c_odd = ((2 * p + 1) * A_GROUP + 2 * jr) * HEAD_DIM
            o_ref[:, c_even:c_even + tw] = jnp.where(first_o, a, b).astype(o_ref.dtype)
            o_ref[:, c_odd:c_odd + tw] = jnp.where(first_o, b, a).astype(o_ref.dtype)

    kprev[...] = knb
    vprev[...] = v

    @pl.when(n == pl.num_programs(1) - 1)
    def _():
        nk_ref[0] = kn
        nv_ref[0] = v.astype(F32)


def _attn_prompt_call(z, sinks, qg, kg, batch, seq):
    nb = seq // WINDOW
    kvw = A_KV_HEADS * HEAD_DIM
    row = lambda b, n, s: b * nb + n
    grid_spec = pltpu.PrefetchScalarGridSpec(
        num_scalar_prefetch=1,
        grid=(batch, nb),
        in_specs=[
            pl.BlockSpec((WINDOW, D_MODEL), lambda b, n, s: (row(b, n, s), Z_AQ // D_MODEL)),
            pl.BlockSpec((WINDOW, kvw), lambda b, n, s: (row(b, n, s), Z_AK // kvw)),
            pl.BlockSpec((WINDOW, kvw), lambda b, n, s: (row(b, n, s), Z_AV // kvw)),
            pl.BlockSpec((1, 2 * HEAD_DIM), lambda b, n, s: (0, 0)),
            pl.BlockSpec((1, 2 * HEAD_DIM), lambda b, n, s: (0, 0)),
        ],
        out_specs=[
            pl.BlockSpec((WINDOW, D_MODEL), lambda b, n, s: (row(b, n, s), 0)),
            pl.BlockSpec((1, WINDOW, kvw), lambda b, n, s: (b, 0, 0)),
            pl.BlockSpec((1, WINDOW, kvw), lambda b, n, s: (b, 0, 0)),
        ],
        scratch_shapes=[pltpu.VMEM((WINDOW, kvw), BF16), pltpu.VMEM((WINDOW, kvw), BF16)],
    )
    pair_gain = lambda g: jnp.tile(g, 2).reshape(1, 2 * HEAD_DIM)
    return pl.pallas_call(
        _attn_prompt_kernel,
        grid_spec=grid_spec,
        out_shape=[
            jax.ShapeDtypeStruct((batch * seq, D_MODEL), BF16),
            jax.ShapeDtypeStruct((batch, WINDOW, kvw), F32),
            jax.ShapeDtypeStruct((batch, WINDOW, kvw), F32),
        ],
        compiler_params=_cparams(("arbitrary", "arbitrary")),
        name="attn_prompt",
    )(sinks, z, z, z, pair_gain(qg), pair_gain(kg))


ATTN_S_SEQS = 8


def _attn_sample_kernel(sink_ref, q_ref, k_ref, v_ref, kc_ref, vc_ref, qg_ref, kg_ref, o_ref, nk_ref, nv_ref):
    _, sb, wb, kvw = kc_ref.shape
    l = q_ref.shape[0] // sb
    rows = sb * l
    q = q_ref[...].astype(F32)
    k = k_ref[...].astype(F32)
    v = v_ref[...]
    qg = qg_ref[...]
    kg = kg_ref[...]
    kn = jnp.concatenate(
        [_head_rms(k[:, h * HEAD_DIM:(h + 1) * HEAD_DIM], kg) for h in range(A_KV_HEADS)], axis=-1)
    nk_ref[...] = kn
    nv_ref[...] = v.astype(F32)
    knb = kn.astype(BF16)
    kc = kc_ref[0].reshape(sb * wb, kvw).astype(BF16)
    vc = vc_ref[0].reshape(sb * wb, kvw).astype(BF16)

    rq = lax.broadcasted_iota(jnp.int32, (A_GROUP * rows, sb * wb), 0) % rows
    cc = lax.broadcasted_iota(jnp.int32, (A_GROUP * rows, sb * wb), 1)
    mask_c = (rq // l == cc // wb) & (cc % wb > rq % l + (wb - WINDOW))
    rq2 = lax.broadcasted_iota(jnp.int32, (A_GROUP * rows, rows), 0) % rows
    cn = lax.broadcasted_iota(jnp.int32, (A_GROUP * rows, rows), 1)
    mask_n = (rq2 // l == cn // l) & (cn % l <= rq2 % l)
    scale = HEAD_DIM ** -0.5
    outs = []
    for h in range(A_KV_HEADS):
        hs = slice(h * HEAD_DIM, (h + 1) * HEAD_DIM)
        qs = _stack_q(q, qg, h, scale)
        s_c = jnp.where(mask_c, _dot_nt(qs, kc[:, hs]), NEG_BIG)
        s_n = jnp.where(mask_n, _dot_nt(qs, knb[:, hs]), NEG_BIG)
        sink = _sink_col(sink_ref, h, rows)
        m = jnp.maximum(jnp.maximum(jnp.max(s_c, axis=-1, keepdims=True),
                                    jnp.max(s_n, axis=-1, keepdims=True)), sink)
        p_c = jnp.exp(s_c - m)
        p_n = jnp.exp(s_n - m)
        denom = (jnp.sum(p_c, axis=-1, keepdims=True) + jnp.sum(p_n, axis=-1, keepdims=True)
                 + jnp.exp(sink - m))
        o = (_dot(p_c.astype(BF16), vc[:, hs]) + _dot(p_n.astype(BF16), v[:, hs])) / denom
        outs += [o[g * rows:(g + 1) * rows] for g in range(A_GROUP)]
    o_ref[...] = jnp.concatenate(outs, axis=-1).astype(o_ref.dtype)


def _attn_sample_call(z, kc, vc, sinks, qg, kg, batch, l, layer):
    kvw = A_KV_HEADS * HEAD_DIM
    sb = ATTN_S_SEQS
    rows = sb * l
    wb = kc.shape[1]
    grid_spec = pltpu.PrefetchScalarGridSpec(
        num_scalar_prefetch=1,
        grid=(batch // sb,),
        in_specs=[
            pl.BlockSpec((rows, D_MODEL), lambda i, s: (i, Z_AQ // D_MODEL)),
            pl.BlockSpec((rows, kvw), lambda i, s: (i, Z_AK // kvw)),
            pl.BlockSpec((rows, kvw), lambda i, s: (i, Z_AV // kvw)),
            pl.BlockSpec((1, sb, wb, kvw), lambda i, s: (layer, i, 0, 0)),
            pl.BlockSpec((1, sb, wb, kvw), lambda i, s: (layer, i, 0, 0)),
            pl.BlockSpec((1, HEAD_DIM), lambda i, s: (0, 0)),
            pl.BlockSpec((1, HEAD_DIM), lambda i, s: (0, 0)),
        ],
        out_specs=[
            pl.BlockSpec((rows, D_MODEL), lambda i, s: (i, 0)),
            pl.BlockSpec((rows, kvw), lambda i, s: (i, 0)),
            pl.BlockSpec((rows, kvw), lambda i, s: (i, 0)),
        ],
    )
    return pl.pallas_call(
        _attn_sample_kernel,
        grid_spec=grid_spec,
        out_shape=[
            jax.ShapeDtypeStruct((batch * l, D_MODEL), BF16),
            jax.ShapeDtypeStruct((batch * l, kvw), F32),
            jax.ShapeDtypeStruct((batch * l, kvw), F32),
        ],
        compiler_params=_cparams(("arbitrary",)),
        name="attn_sample",
    )(sinks, z, z, z, kc, vc, qg.reshape(1, HEAD_DIM), kg.reshape(1, HEAD_DIM))


def _ret_tables(chunk, pos0, length):
    h = np.arange(R_HEADS, dtype=np.float64)
    log_gamma = np.log1p(-np.exp2(-5.0 - h))
    idx = np.arange(chunk, dtype=np.float64)
    diff = idx[:, None] - idx[None, :]
    intra = np.where(diff[None] >= 0, np.exp(np.maximum(diff, 0.0)[None] * log_gamma[:, None, None]), 0.0)
    q_decay = np.exp((idx + 1.0)[:, None] * log_gamma[None, :])
    k_decay = np.exp((chunk - 1.0 - idx)[:, None] * log_gamma[None, :])
    c_decay = np.exp(chunk * log_gamma)
    qd = np.repeat(q_decay, R_DK, axis=1)
    kd = np.repeat(k_decay, R_DK, axis=1)
    inv_freq = ROPE_BASE ** (-np.arange(0, R_DK, 2, dtype=np.float64) / R_DK)
    ang = (pos0 + np.arange(length, dtype=np.float64))[:, None] * inv_freq[None, :]
    cos = np.tile(np.concatenate([np.cos(ang), np.cos(ang)], axis=1), (1, R_HEADS))
    sin = np.tile(np.concatenate([-np.sin(ang), np.sin(ang)], axis=1), (1, R_HEADS))
    f = lambda a: jnp.asarray(a, F32)
    return f(intra), f(qd), f(kd), [float(c) for c in c_decay], f(cos), f(sin)


def _rope(x, cos, sin):
    n = x.shape[-1]
    half = R_DK // 2
    lane = lax.broadcasted_iota(jnp.int32, x.shape, 1)
    up = pltpu.roll(x, n - half, axis=1)
    dn = pltpu.roll(x, half, axis=1)
    partner = jnp.where(lane % R_DK < half, up, dn)
    return x * cos + partner * sin


def _head_ln(o, g):
    mu = jnp.mean(o, axis=-1, keepdims=True)
    oc = o - mu
    var = jnp.mean(oc * oc, axis=-1, keepdims=True)
    return oc * lax.rsqrt(var + EPS) * g


def _silu(x):
    return x * jax.nn.sigmoid(x)


def _ret_prompt_kernel(q_ref, k_ref, v_ref, g_ref, cos_ref, sin_ref, intra_ref, qd_ref, kd_ref, ng_ref,
                       o_ref, s_ref, state, *, c_decay):
    n = pl.program_id(1)

    @pl.when(n == 0)
    def _():
        state[...] = jnp.zeros_like(state)

    cos = cos_ref[...]
    sin = sin_ref[...]
    q = _rope(q_ref[...].astype(F32), cos, sin)
    k = _rope(k_ref[...].astype(F32), cos, sin) * (R_DK ** -0.5)
    qb = q.astype(BF16)
    kb = k.astype(BF16)
    qdb = (q * qd_ref[...]).astype(BF16)
    kdb = (k * kd_ref[...]).astype(BF16)
    for h in range(R_HEADS):
        ks = slice(h * R_DK, (h + 1) * R_DK)
        vs = slice(h * R_DV, (h + 1) * R_DV)
        vh = v_ref[:, vs]
        a = _dot_nt(qb[:, ks], kb[:, ks]) * intra_ref[h]
        s_old = state[h]
        o = _dot(a.astype(BF16), vh) + _dot(qdb[:, ks], s_old.astype(BF16))
        state[h] = s_old * c_decay[h] + _dot_tn(kdb[:, ks], vh)
        y = _head_ln(o, ng_ref[:, vs]) * _silu(g_ref[:, vs].astype(F32))
        o_ref[:, vs] = y.astype(o_ref.dtype)

    @pl.when(n == pl.num_programs(1) - 1)
    def _():
        s_ref[0] = state[...]


def _ret_prompt_call(z, ret_norm_g, batch, seq):
    c = R_CHUNK
    nc = seq // c
    intra, qd, kd, c_decay, cos, sin = _ret_tables(c, 0, seq)
    qkw = R_HEADS * R_DK
    row = lambda b, n: b * nc + n
    const2 = lambda b, n: (0, 0)
    return pl.pallas_call(
        functools.partial(_ret_prompt_kernel, c_decay=c_decay),
        grid=(batch, nc),
        in_specs=[
            pl.BlockSpec((c, qkw), lambda b, n: (row(b, n), Z_RQ // qkw)),
            pl.BlockSpec((c, qkw), lambda b, n: (row(b, n), Z_RK // qkw)),
            pl.BlockSpec((c, D_MODEL), lambda b, n: (row(b, n), Z_RV // D_MODEL)),
            pl.BlockSpec((c, D_MODEL), lambda b, n: (row(b, n), Z_RG // D_MODEL)),
            pl.BlockSpec((c, qkw), lambda b, n: (n, 0)),
            pl.BlockSpec((c, qkw), lambda b, n: (n, 0)),
            pl.BlockSpec((R_HEADS, c, c), lambda b, n: (0, 0, 0)),
            pl.BlockSpec((c, qkw), const2),
            pl.BlockSpec((c, qkw), const2),
            pl.BlockSpec((1, D_MODEL), const2),
        ],
        out_specs=[
            pl.BlockSpec((c, D_MODEL), lambda b, n: (row(b, n), 0)),
            pl.BlockSpec((1, R_HEADS, R_DK, R_DV), lambda b, n: (b, 0, 0, 0)),
        ],
        out_shape=[
            jax.ShapeDtypeStruct((batch * seq, D_MODEL), BF16),
            jax.ShapeDtypeStruct((batch, R_HEADS, R_DK, R_DV), F32),
        ],
        scratch_shapes=[pltpu.VMEM((R_HEADS, R_DK, R_DV), F32)],
        compiler_params=_cparams(("arbitrary", "arbitrary")),
        name="ret_prompt",
    )(z, z, z, z, cos, sin, intra, qd, kd, ret_norm_g.reshape(1, D_MODEL))


RET_S_SEQS = 8


def _ret_sample_kernel(q_ref, k_ref, v_ref, g_ref, s0_ref, cos_ref, sin_ref, intra_ref, qd_ref, kd_ref, ng_ref,
                       o_ref, s_ref, *, c_decay, l):
    sb = s0_ref.shape[1]
    rows = sb * l
    cos = cos_ref[...]
    sin = sin_ref[...]
    q = _rope(q_ref[...].astype(F32), cos, sin)
    k = _rope(k_ref[...].astype(F32), cos, sin) * (R_DK ** -0.5)
    qb = q.astype(BF16)
    kb = k.astype(BF16)
    qdb = (q * qd_ref[...]).astype(BF16)
    kdb = (k * kd_ref[...]).astype(BF16)
    for h in range(R_HEADS):
        ks = slice(h * R_DK, (h + 1) * R_DK)
        vs = slice(h * R_DV, (h + 1) * R_DV)
        vh = v_ref[:, vs]
        a = _dot_nt(qb[:, ks], kb[:, ks]) * intra_ref[h]
        o = _dot(a.astype(BF16), vh)
        cross, new_s = [], []
        for b in range(sb):
            rs = slice(b * l, (b + 1) * l)
            s_old = s0_ref[0, b, h]
            cross.append(_dot(qdb[rs, ks], s_old.astype(BF16)))
            s_ref[b, h] = s_old * c_decay[h] + _dot_tn(kdb[rs, ks], vh[rs])
        o = o + jnp.concatenate(cross, axis=0)
        y = _head_ln(o, ng_ref[:, vs]) * _silu(g_ref[:, vs].astype(F32))
        o_ref[:, vs] = y.astype(o_ref.dtype)


def _ret_sample_call(z, s0, ret_norm_g, batch, l, layer):
    c = math.gcd(l, R_CHUNK)
    assert c == l, "sample step expects a single retention chunk"
    sb = RET_S_SEQS
    rows = sb * l
    intra, qd, kd, c_decay, cos, sin = _ret_tables(c, PAST_LEN, l)
    eye = jnp.eye(sb, dtype=F32)
    intra_bd = jnp.einsum("ab,hij->haibj", eye, intra).reshape(R_HEADS, rows, rows)
    tile = lambda t: jnp.tile(t, (sb, 1))
    qkw = R_HEADS * R_DK
    const2 = lambda i: (0, 0)
    return pl.pallas_call(
        functools.partial(_ret_sample_kernel, c_decay=c_decay, l=l),
        grid=(batch // sb,),
        in_specs=[
            pl.BlockSpec((rows, qkw), lambda i: (i, Z_RQ // qkw)),
            pl.BlockSpec((rows, qkw), lambda i: (i, Z_RK // qkw)),
            pl.BlockSpec((rows, D_MODEL), lambda i: (i, Z_RV // D_MODEL)),
            pl.BlockSpec((rows, D_MODEL), lambda i: (i, Z_RG // D_MODEL)),
            pl.BlockSpec((1, sb, R_HEADS, R_DK, R_DV), lambda i: (layer, i, 0, 0, 0)),
            pl.BlockSpec((rows, qkw), const2),
            pl.BlockSpec((rows, qkw), const2),
            pl.BlockSpec((R_HEADS, rows, rows), lambda i: (0, 0, 0)),
            pl.BlockSpec((rows, qkw), const2),
            pl.BlockSpec((rows, qkw), const2),
            pl.BlockSpec((1, D_MODEL), const2),
        ],
        out_specs=[
            pl.BlockSpec((rows, D_MODEL), lambda i: (i, 0)),
            pl.BlockSpec((sb, R_HEADS, R_DK, R_DV), lambda i: (i, 0, 0, 0)),
        ],
        out_shape=[
            jax.ShapeDtypeStruct((batch * l, D_MODEL), BF16),
            jax.ShapeDtypeStruct(s0.shape[1:], F32),
        ],
        compiler_params=_cparams(("arbitrary",)),
        name="ret_sample",
    )(z, z, z, z, s0, tile(cos), tile(sin), intra_bd, tile(qd), tile(kd), ret_norm_g.reshape(1, D_MODEL))


def _gelu(x):
    return jax.nn.gelu(x, approximate=True)


def _route_rows(logits, carry):
    tm = logits.shape[0]
    lane = lax.broadcasted_iota(jnp.int32, logits.shape, 1).astype(F32)
    work = logits
    sel = jnp.zeros(logits.shape, F32)
    vals, idxs = [], []
    for _ in range(TOP_K):
        m = jnp.max(work, axis=-1, keepdims=True)
        idx = jnp.min(jnp.where(work == m, lane, float(ROUTER_PAD)), axis=-1, keepdims=True)
        hit = lane == idx
        vals.append(m)
        idxs.append(idx)
        sel = jnp.where(hit, 1.0, sel)
        work = jnp.where(hit, -3e38, work)
    ex = [jnp.exp(v - vals[0]) for v in vals]
    den = ex[0] + ex[1] + ex[2] + ex[3]
    r = lax.broadcasted_iota(jnp.int32, (tm, tm), 0)
    c = lax.broadcasted_iota(jnp.int32, (tm, tm), 1)
    before = jnp.where(c < r, 1.0, 0.0).astype(BF16)
    rank = _dot(before, sel.astype(BF16)) + carry
    route = jnp.zeros(logits.shape, F32)
    for k in range(TOP_K):
        route = jnp.where(lane == float(k), ex[k] / den, route)
        route = jnp.where(lane == float(TOP_K + k), idxs[k], route)
        rk = jnp.sum(jnp.where(lane == idxs[k], rank, 0.0), axis=-1, keepdims=True)
        route = jnp.where(lane == float(2 * TOP_K + k), rk, route)
    return route, carry + jnp.sum(sel, axis=0, keepdims=True)


def _merge_kernel(*refs, emit_cv):
    (x_ref, oa_ref, ob_ref, cu_ref, cv_ref, ga_ref, gb_ref, gc_ref, g1_ref, sh2_ref, sc2_ref,
     bg_ref, lnf_ref, gmg_ref, gmb_ref, mix_ref, mixb_ref, wout_ref, rwh_ref, rwl_ref, rb_ref) = refs[:21]
    cnt_ref = refs[21]
    if emit_cv:
        xo_ref, h2_ref, route_ref, cnto_ref, cvo_ref, carry = refs[22:]
    else:
        xo_ref, h2_ref, route_ref, cnto_ref, carry = refs[22:]
    first = (pl.program_id(0) == 0) & (pl.program_id(1) == 0)

    @pl.when(first)
    def _():
        carry[...] = cnt_ref[...]

    x = x_ref[...]
    gbk, rb, d = x.shape
    tm = gbk * rb
    cw = C_CHUNK
    gw = d // C_GROUPS

    cv = _gelu(cv_ref[...].astype(F32))
    mu = jnp.mean(cv, axis=-1, keepdims=True)
    cvc = cv - mu
    var = jnp.mean(cvc * cvc, axis=-1, keepdims=True)
    cv = cvc * lax.rsqrt(var + EPS) * gmg_ref[...] + gmb_ref[...]
    if emit_cv:
        cvo_ref[...] = cv
    cvb = cv.astype(BF16)
    mixed_rows = []
    for c in range(tm // cw):
        rs = slice(c * cw, (c + 1) * cw)
        cols = [_dot(mix_ref[g], cvb[rs, g * gw:(g + 1) * gw]) for g in range(C_GROUPS)]
        mixed_rows.append(jnp.concatenate(cols, axis=-1) + mixb_ref[...])
    mixed = jnp.concatenate(mixed_rows, axis=0)
    oc = _gelu(cu_ref[...].astype(F32)) * mixed

    bg = bg_ref[...]
    ga = jax.nn.sigmoid(ga_ref[...].astype(F32) + bg[0:1])
    gb = jax.nn.sigmoid(gb_ref[...].astype(F32) + bg[1:2])
    gc = jax.nn.sigmoid(gc_ref[...].astype(F32) + bg[2:3])
    merged = ga * oa_ref[...].astype(F32) + gb * ob_ref[...].astype(F32) + gc * oc
    y = _dot(merged.astype(BF16), wout_ref[...])
    x = x + g1_ref[...] * y.reshape(gbk, rb, d)
    xo_ref[...] = x

    h2 = (_rms(x) * lnf_ref[...] * (1.0 + sc2_ref[...]) + sh2_ref[...]).reshape(tm, d)
    h2b = h2.astype(BF16)
    h2_ref[0], h2_ref[1] = _pack_rows(h2)
    h2l = (h2 - h2b.astype(F32)).astype(BF16)
    rwh = rwh_ref[...]
    logits = _dot(h2b, rwh) + _dot(h2l, rwh) + _dot(h2b, rwl_ref[...]) + rb_ref[...]
    route, counts = _route_rows(logits, carry[...])
    route_ref[...] = route
    carry[...] = counts
    cnto_ref[...] = counts


def _merge_call(x, z, oa, ob, mod, b_gate, ln_ffn_g, gm_ln_g, gm_ln_b, mix, mixb, w_out_bf, rw_hi, rw_lo, rb,
                counts, emit_cv):
    g, r, d = x.shape
    gb, rb_ = _group_blocks(g, r)
    tm = gb * rb_
    nj = r // rb_
    t = g * r
    xspec = pl.BlockSpec((gb, rb_, d), lambda i, j: (i, j, 0))
    rows = lambda col: pl.BlockSpec((tm, d), lambda i, j: (i * nj + j, col))
    mspec = lambda col: pl.BlockSpec((gb, 1, d), lambda i, j: (i, 0, col))
    const = lambda shape: pl.BlockSpec(shape, lambda i, j: (0,) * len(shape))
    in_specs = [
        xspec, rows(0), rows(0),
        rows(Z_CU // d), rows(Z_CV // d), rows(Z_GA // d), rows(Z_GB // d), rows(Z_GC // d),
        mspec(2), mspec(3), mspec(4),
        const((3, d)), const((1, 1, d)), const((1, d)), const((1, d)),
        const((C_GROUPS, C_CHUNK, C_CHUNK)), const((C_CHUNK, d)),
        const((d, d)), const((d, ROUTER_PAD)), const((d, ROUTER_PAD)), const((1, ROUTER_PAD)),
        const((1, ROUTER_PAD)),
    ]
    out_specs = [xspec, pl.BlockSpec((2, tm, PACK_W), lambda i, j: (0, i * nj + j, 0)),
                 pl.BlockSpec((tm, ROUTER_PAD), lambda i, j: (i * nj + j, 0)), const((1, ROUTER_PAD))]
    out_shape = [jax.ShapeDtypeStruct(x.shape, F32), jax.ShapeDtypeStruct((2, t, PACK_W), U32),
                 jax.ShapeDtypeStruct((t, ROUTER_PAD), F32), jax.ShapeDtypeStruct((1, ROUTER_PAD), F32)]
    if emit_cv:
        out_specs.append(rows(0))
        out_shape.append(jax.ShapeDtypeStruct((t, d), F32))
    return pl.pallas_call(
        functools.partial(_merge_kernel, emit_cv=emit_cv),
        grid=(g // gb, nj), in_specs=in_specs, out_specs=out_specs, out_shape=out_shape,
        scratch_shapes=[pltpu.VMEM((1, ROUTER_PAD), F32)],
        compiler_params=_cparams(("arbitrary", "arbitrary")),
        name="merge",
    )(x, oa, ob, z, z, z, z, z, mod, mod, mod,
      b_gate.reshape(3, d), ln_ffn_g.reshape(1, 1, d), gm_ln_g.reshape(1, d), gm_ln_b.reshape(1, d),
      mix, mixb, w_out_bf, rw_hi, rw_lo, rb, counts)


def _moe_kernel(be_ref, nx_ref, nu_ref, x_ref, w1_hbm, b1_ref, w2_hbm, b2_ref, o_ref,
                w1s, w2s, w1b, w2b, sem, *, layer):
    i = pl.program_id(0)
    e = be_ref[i]
    prev = be_ref[jnp.maximum(i - 1, 0)]

    def weight_copies(ex):
        return (pltpu.make_async_copy(w1_hbm.at[layer, ex], w1s, sem.at[0]),
                pltpu.make_async_copy(w2_hbm.at[layer, ex], w2s, sem.at[1]))

    @pl.when(i == 0)
    def _():
        for cp in weight_copies(e):
            cp.start()

    @pl.when((i == 0) | (e != prev))
    def _():
        for cp in weight_copies(e):
            cp.wait()
        w1b[...] = w1s[...].astype(BF16)
        w2b[...] = w2s[...].astype(BF16)
        nxt = nx_ref[i]

        @pl.when(nxt >= 0)
        def _():
            for cp in weight_copies(nxt):
                cp.start()

    @pl.when(i < nu_ref[0])
    def _():
        xb = _unpack_rows(x_ref[0], x_ref[1]).astype(BF16)
        hdn = _dot(xb, w1b[...]) + b1_ref[0, 0]
        g = jnp.minimum(hdn[:, :D_FF], SWIGLU_LIMIT)
        up = jnp.clip(hdn[:, D_FF:], -SWIGLU_LIMIT, SWIGLU_LIMIT)
        act = (up + 1.0) * (g * jax.nn.sigmoid(g * SWIGLU_ALPHA))
        o_ref[0], o_ref[1] = _pack_rows(_dot(act.astype(BF16), w2b[...]) + b2_ref[0, 0])


def _moe_call(xb, block_e, next_e, n_used, w1, b1, w2, b2, layer):
    _, n_rows, _ = xb.shape
    nblk = n_rows // MOE_ROWS
    depth, ne, d, f2 = w1.shape
    last = lambda i, be, nx, nu: jnp.minimum(i, nu[0] - 1)
    xspec = pl.BlockSpec((2, MOE_ROWS, PACK_W), lambda i, be, nx, nu: (0, last(i, be, nx, nu), 0))
    grid_spec = pltpu.PrefetchScalarGridSpec(
        num_scalar_prefetch=3,
        grid=(nblk,),
        in_specs=[
            xspec,
            pl.BlockSpec(memory_space=pl.ANY),
            pl.BlockSpec((1, 1, 1, f2), lambda i, be, nx, nu: (layer, be[i], 0, 0)),
            pl.BlockSpec(memory_space=pl.ANY),
            pl.BlockSpec((1, 1, 1, d), lambda i, be, nx, nu: (layer, be[i], 0, 0)),
        ],
        out_specs=xspec,
        scratch_shapes=[pltpu.VMEM((d, f2), F32), pltpu.VMEM((f2 // 2, d), F32),
                        pltpu.VMEM((d, f2), BF16), pltpu.VMEM((f2 // 2, d), BF16),
                        pltpu.SemaphoreType.DMA((2,))],
    )
    return pl.pallas_call(
        functools.partial(_moe_kernel, layer=layer),
        grid_spec=grid_spec,
        out_shape=jax.ShapeDtypeStruct(xb.shape, U32),
        compiler_params=_cparams(("arbitrary",)),
        name="moe",
    )(block_e, next_e, n_used, xb, w1, b1.reshape(depth, ne, 1, f2), w2, b2.reshape(depth, ne, 1, d))


def _sc_mesh():
    return plsc.VectorSubcoreMesh(core_axis_name="core", subcore_axis_name="subcore")


def _sc_scatter_rows(x, idx, n_out):
    t, c = x.shape
    kk = idx.shape[0]

    @pl.kernel(out_type=jax.ShapeDtypeStruct((n_out, c), x.dtype), mesh=_sc_mesh(), scratch_types=[])
    def scatter(x_hbm, i_hbm, o_hbm):
        def body(x_vmem, i_vmem):
            for k in range(kk):
                pltpu.sync_copy(x_vmem, o_hbm.at[i_vmem.at[k]])

        pltpu.emit_pipeline(
            body,
            grid=(t // SC_WINDOW,),
            in_specs=[pl.BlockSpec((SC_WINDOW, c), index_map=lambda i: (i, 0)),
                      pl.BlockSpec((kk, SC_WINDOW), index_map=lambda i: (0, i))],
            out_specs=[],
            core_axis_name=("core", "subcore"),
            dimension_semantics=(pltpu.PARALLEL,),
        )(x_hbm, i_hbm)

    return scatter(x, idx)


def _sc_gather_rows(data, idx):
    n = idx.shape[0]
    c = data.shape[1]

    @pl.kernel(out_type=jax.ShapeDtypeStruct((n, c), data.dtype), mesh=_sc_mesh(), scratch_types=[])
    def gather(x_hbm, i_hbm, o_hbm):
        def body(i_vmem, o_vmem):
            pltpu.sync_copy(x_hbm.at[i_vmem.at[0]], o_vmem)

        pltpu.emit_pipeline(
            body,
            grid=(n // SC_WINDOW,),
            in_specs=[pl.BlockSpec((1, SC_WINDOW), index_map=lambda i: (0, i))],
            out_specs=[pl.BlockSpec((SC_WINDOW, c), index_map=lambda i: (i, 0))],
            core_axis_name=("core", "subcore"),
            dimension_semantics=(pltpu.PARALLEL,),
        )(i_hbm, o_hbm)

    return gather(data, idx.reshape(1, n))


def _plan(route, counts):
    t = route.shape[0]
    n_blocks = -(-t * TOP_K // MOE_ROWS) + N_EXPERTS
    e4 = route[:, TOP_K:2 * TOP_K].astype(jnp.int32)
    r4 = route[:, 2 * TOP_K:3 * TOP_K].astype(jnp.int32)
    cnt = counts[0, :N_EXPERTS].astype(jnp.int32)
    padded = (cnt + MOE_ROWS - 1) // MOE_ROWS * MOE_ROWS
    pend = jnp.cumsum(padded)
    pstart = pend - padded
    onehot = e4[:, :, None] == jnp.arange(N_EXPERTS, dtype=jnp.int32)[None, None, :]
    dest = (r4 + jnp.sum(jnp.where(onehot, pstart[None, None, :], 0), axis=-1)).T
    blk = jnp.arange(n_blocks, dtype=jnp.int32) * MOE_ROWS
    block_e = jnp.sum((blk[:, None] >= pend[None, :]).astype(jnp.int32), axis=-1)
    n_used = (pend[-1] // MOE_ROWS).astype(jnp.int32).reshape(1)
    block_e = jnp.minimum(block_e, jnp.sum((pend < pend[-1]).astype(jnp.int32))).astype(jnp.int32)
    after = pend[block_e] // MOE_ROWS
    next_e = jnp.where(after < n_used[0], block_e[jnp.minimum(after, n_blocks - 1)], -1).astype(jnp.int32)
    return dest, block_e, next_e, n_used, n_blocks * MOE_ROWS


def _moe(h2p, route, counts, w1, b1, w2, b2, layer):
    _, t, pw = h2p.shape
    dest, block_e, next_e, n_used, n_rows = _plan(route, counts)
    idx_s = jnp.concatenate([dest, dest + n_rows], axis=1)
    xb = _sc_scatter_rows(h2p.reshape(2 * t, pw), idx_s, 2 * n_rows).reshape(2, n_rows, pw)
    yb = _moe_call(xb, block_e, next_e, n_used, w1, b1, w2, b2, layer)
    idx_g = jnp.concatenate([dest.reshape(-1), dest.reshape(-1) + n_rows])
    y4 = _sc_gather_rows(yb.reshape(2 * n_rows, pw), idx_g)
    return y4.reshape(2, TOP_K, t, pw)


def _reorder_w_in(w):
    o = _ORIG
    seg = lambda a, n: w[:, a:a + n]
    d = D_MODEL
    q_heads = sorted(((_q_col(kv, g), kv * A_GROUP + g) for kv in range(A_KV_HEADS) for g in range(A_GROUP)))
    aq = jnp.concatenate([seg(o["aq"] + h * HEAD_DIM, HEAD_DIM) for _, h in q_heads], axis=1)
    parts = [aq, seg(o["rv"], d), seg(o["rg"], d), seg(o["cu"], d), seg(o["cv"], d),
             seg(o["mg"], d), seg(o["mg"] + d, d), seg(o["mg"] + 2 * d, d),
             seg(o["rq"], 512), seg(o["rk"], 512), seg(o["ak"], 256), seg(o["av"], 256)]
    w = jnp.concatenate(parts, axis=1).astype(BF16)
    return w.reshape(D_MODEL, IN_CHUNKS, D_IN // IN_CHUNKS).transpose(1, 0, 2)


def _gmlp_tables(ws, bs, chunk_len, rows):
    causal = jnp.tril(jnp.ones((C_CHUNK, C_CHUNK), dtype=bool))
    w = jnp.where(causal[None], ws, 0.0)[:, :chunk_len, :chunk_len]
    reps = rows // chunk_len
    eye = jnp.eye(reps, dtype=F32)
    mix = jnp.einsum("ab,gts->gatbs", eye, w).reshape(C_GROUPS, rows, rows).astype(BF16)
    b = jnp.tile(bs[:, :chunk_len].T, (reps, 1))
    mixb = jnp.repeat(b, D_MODEL // C_GROUPS, axis=1)
    return mix, mixb


def kernel(x_prompt, x_sample, c_prompt, c_sample, cache_attn_k, cache_attn_v, state_retention, ln_mix_g, ln_ffn_g, w_ada, b_ada, w_in, b_gate, q_norm_g, k_norm_g, attn_sinks, ret_norm_g, gm_ln_g, gm_ln_b, gm_ws, gm_bs, w_out, router_w, router_b, moe_w1, moe_b1, moe_w2, moe_b2):
    bp, lp, d = x_prompt.shape
    bs, ls, _ = x_sample.shape
    tp, ts = bp * lp, bs * ls
    kvw = A_KV_HEADS * HEAD_DIM
    wb = cache_attn_k.shape[2]

    mod_all = _ada_call(jnp.concatenate([c_prompt, c_sample], axis=0), w_ada, b_ada)

    cache_k = cache_attn_k.reshape(DEPTH, bs, wb, kvw)
    cache_v = cache_attn_v.reshape(DEPTH, bs, wb, kvw)
    xp, xs = x_prompt, x_sample
    y4 = route = None
    mod_p = mod_s = None
    pk, pv, ps, sk, sv, ss, sg = [], [], [], [], [], [], []
    for l in range(DEPTH):
        prev_mod_p, prev_mod_s = mod_p, mod_s
        mod_p = mod_all[l, :bp].reshape(bp, 1, 6 * d)
        mod_s = mod_all[l, bp:].reshape(bs, 1, 6 * d)
        w_in_bf = _reorder_w_in(w_in[l])
        w_out_bf = w_out[l].astype(BF16)
        rw = jnp.pad(router_w[l], ((0, 0), (0, ROUTER_PAD - N_EXPERTS)))
        rw_hi, rw_lo = _split_bf16(rw)
        rb = jnp.pad(router_b[l], (0, ROUTER_PAD - N_EXPERTS), constant_values=NEG_BIG).reshape(1, ROUTER_PAD)

        if l == 0:
            zp = _inproj_call(xp, mod_p, ln_mix_g[l], w_in_bf)
            zs = _inproj_call(xs, mod_s, ln_mix_g[l], w_in_bf)
        else:
            zp, xp = _inproj_call(xp, mod_p, ln_mix_g[l], w_in_bf, moe_out=(y4, route, prev_mod_p, 0))
            zs, xs = _inproj_call(xs, mod_s, ln_mix_g[l], w_in_bf, moe_out=(y4, route, prev_mod_s, tp // ROW_TILE))
        oa_p, k_p, v_p = _attn_prompt_call(zp, attn_sinks[l], q_norm_g[l], k_norm_g[l], bp, lp)
        oa_s, k_s, v_s = _attn_sample_call(zs, cache_k, cache_v, attn_sinks[l], q_norm_g[l], k_norm_g[l], bs, ls, l)
        ob_p, s_p = _ret_prompt_call(zp, ret_norm_g[l], bp, lp)
        ob_s, s_s = _ret_sample_call(zs, state_retention, ret_norm_g[l], bs, ls, l)

        mix_p, mixb_p = _gmlp_tables(gm_ws[l], gm_bs[l], C_CHUNK, C_CHUNK)
        mix_s, mixb_s = _gmlp_tables(gm_ws[l], gm_bs[l], ls, C_CHUNK)
        common = (b_gate[l], ln_ffn_g[l], gm_ln_g[l], gm_ln_b[l])
        tail = (w_out_bf, rw_hi, rw_lo, rb)
        zero_counts = jnp.zeros((1, ROUTER_PAD), F32)
        xp, h2_p, rt_p, cnt_p = _merge_call(xp, zp, oa_p, ob_p, mod_p, *common, mix_p, mixb_p, *tail,
                                            zero_counts, emit_cv=False)
        xs, h2_s, rt_s, cnt, cv_s = _merge_call(xs, zs, oa_s, ob_s, mod_s, *common, mix_s, mixb_s, *tail,
                                                cnt_p, emit_cv=True)
        route = jnp.concatenate([rt_p, rt_s], axis=0)
        y4 = _moe(jnp.concatenate([h2_p, h2_s], axis=1), route, cnt, moe_w1, moe_b1, moe_w2, moe_b2, l)

        pk.append(k_p.reshape(bp, WINDOW, A_KV_HEADS, HEAD_DIM))
        pv.append(v_p.reshape(bp, WINDOW, A_KV_HEADS, HEAD_DIM))
        ps.append(s_p)
        sk.append(k_s.reshape(bs, ls, A_KV_HEADS, HEAD_DIM))
        sv.append(v_s.reshape(bs, ls, A_KV_HEADS, HEAD_DIM))
        ss.append(s_s)
        sg.append(cv_s.reshape(bs, ls, d))

    xp = _resid_call(xp, y4, route, mod_p, 0)
    xs = _resid_call(xs, y4, route, mod_s, tp // ROW_TILE)
    return (xp, xs, jnp.stack(pk), jnp.stack(pv), jnp.stack(ps), jnp.stack(sk), jnp.stack(sv),
            jnp.stack(ss), jnp.stack(sg))
```

```python
import functools
import math

import numpy as np
import jax
import jax.numpy as jnp
from jax import lax
from jax.experimental import pallas as pl
from jax.experimental.pallas import tpu as pltpu
from jax.experimental.pallas import tpu_sc as plsc

F32 = jnp.float32
BF16 = jnp.bfloat16
U32 = jnp.uint32

D_MODEL = 1024
DEPTH = 4
PAST_LEN = 8192
HEAD_DIM = 64
A_Q_HEADS = 16
A_KV_HEADS = 4
A_GROUP = 4
WINDOW = 128
R_HEADS = 8
R_DK = 64
R_DV = 128
R_CHUNK = 128
ROPE_BASE = 10000.0
C_CHUNK = 128
C_GROUPS = 8
N_EXPERTS = 32
TOP_K = 4
D_FF = D_MODEL
SWIGLU_LIMIT = 7.0
SWIGLU_ALPHA = 1.702
EPS = 1e-6

Z_AQ, Z_RV, Z_RG, Z_CU, Z_CV, Z_GA, Z_GB, Z_GC = 0, 1024, 2048, 3072, 4096, 5120, 6144, 7168
Z_RQ, Z_RK, Z_AK, Z_AV = 8192, 8704, 9216, 9472
D_IN = 9728
IN_CHUNKS = 2
_ORIG = dict(aq=0, ak=1024, av=1280, rq=1536, rk=2048, rv=2560, rg=3584, cu=4608, cv=5632, mg=6656)

ROW_TILE = 512
MOE_ROWS = 256
MOE_HALVES = 2
ROUTER_PAD = 128
PACK_W = D_MODEL // 4
SC_WINDOW = 128
HI_MASK = 0xFFFF0000
VMEM_LIMIT = 56 * 1024 * 1024
NEG_BIG = -1e30


def _cparams(sem):
    return pltpu.CompilerParams(dimension_semantics=sem, vmem_limit_bytes=VMEM_LIMIT)


def _split_bf16(x):
    hi = x.astype(BF16)
    lo = (x - hi.astype(F32)).astype(BF16)
    return hi, lo


def _dot(a, b):
    return jnp.dot(a, b, preferred_element_type=F32)


def _dot_nt(a, b):
    return lax.dot_general(a, b, (((1,), (1,)), ((), ())), preferred_element_type=F32)


def _dot_tn(a, b):
    return lax.dot_general(a, b, (((0,), (0,)), ((), ())), preferred_element_type=F32)


def _ada_kernel(c_ref, w_ref, b_ref, o_ref):
    c = c_ref[...]
    s_hi, s_lo = _split_bf16(c * jax.nn.sigmoid(c))
    w_hi, w_lo = _split_bf16(w_ref[0])
    acc = _dot(s_hi, w_hi) + _dot(s_lo, w_hi) + _dot(s_hi, w_lo)
    o_ref[0] = acc + b_ref[0]


def _ada_call(c_all, w_ada, b_ada):
    depth, d, n = w_ada.shape
    m = c_all.shape[0]
    tn = 1024
    return pl.pallas_call(
        _ada_kernel,
        grid=(depth, n // tn),
        in_specs=[
            pl.BlockSpec((m, d), lambda l, j: (0, 0)),
            pl.BlockSpec((1, d, tn), lambda l, j: (l, 0, j)),
            pl.BlockSpec((1, 1, tn), lambda l, j: (l, 0, j)),
        ],
        out_specs=pl.BlockSpec((1, m, tn), lambda l, j: (l, 0, j)),
        out_shape=jax.ShapeDtypeStruct((depth, m, n), F32),
        compiler_params=_cparams(("arbitrary", "arbitrary")),
        name="ada",
    )(c_all, w_ada, b_ada.reshape(depth, 1, n))


def _pack_rows(y):
    bits = pltpu.bitcast(y.astype(BF16).astype(F32), U32)
    q = [bits[:, i * PACK_W:(i + 1) * PACK_W] for i in range(4)]
    return (q[0] >> 16) | q[1], (q[2] >> 16) | q[3]


def _unpack_rows(a, b):
    f = lambda w: pltpu.bitcast(w, F32)
    return jnp.concatenate([f(a << 16), f(a & jnp.uint32(HI_MASK)), f(b << 16), f(b & jnp.uint32(HI_MASK))], axis=-1)


def _combine(y4_ref, route_ref):
    route = route_ref[...]
    acc = None
    for k in range(TOP_K):
        term = route[:, k:k + 1] * _unpack_rows(y4_ref[0, k], y4_ref[1, k])
        acc = term if acc is None else acc + term
    return acc


def _rms(x):
    return x * lax.rsqrt(jnp.mean(x * x, axis=-1, keepdims=True) + EPS)


def _inproj_kernel(*refs, has_resid):
    if has_resid:
        x_ref, y4_ref, route_ref, g2_ref, sh_ref, sc_ref, lng_ref, w_ref, z_ref, xo_ref, hb = refs
    else:
        x_ref, sh_ref, sc_ref, lng_ref, w_ref, z_ref, hb = refs
    c = pl.program_id(2)

    @pl.when(c == 0)
    def _():
        x = x_ref[...]
        if has_resid:
            x = x + g2_ref[...] * _combine(y4_ref, route_ref).reshape(x.shape)
            xo_ref[...] = x
        h = _rms(x) * lng_ref[...] * (1.0 + sc_ref[...]) + sh_ref[...]
        hb[...] = h.reshape(hb.shape).astype(BF16)

    h = hb[...]
    n = w_ref.shape[2]
    for c0 in range(0, n, 1024):
        c1 = min(c0 + 1024, n)
        z_ref[:, c0:c1] = _dot(h, w_ref[c, :, c0:c1]).astype(BF16)


def _group_blocks(g, r):
    if r >= ROW_TILE:
        return 1, ROW_TILE
    return ROW_TILE // r, r


def _moe_out_specs(tm, nj, tile_off):
    return [pl.BlockSpec((2, TOP_K, tm, PACK_W), lambda i, j, *_: (0, 0, tile_off + i * nj + j, 0)),
            pl.BlockSpec((tm, ROUTER_PAD), lambda i, j, *_: (tile_off + i * nj + j, 0))]


def _inproj_call(x, mod, ln_g, w_bf, moe_out=None):
    g, r, d = x.shape
    gb, rb = _group_blocks(g, r)
    nj = r // rb
    nc, _, cw = w_bf.shape
    grid = (g // gb, nj, nc)
    xspec = pl.BlockSpec((gb, rb, d), lambda i, j, c: (i, j, 0))
    mspec = lambda col: pl.BlockSpec((gb, 1, d), lambda i, j, c: (i, 0, col))
    in_specs, args = [xspec], [x]
    if moe_out is not None:
        y4, route, mod_prev, tile_off = moe_out
        in_specs += _moe_out_specs(gb * rb, nj, tile_off) + [mspec(5)]
        args += [y4, route, mod_prev]
    in_specs += [mspec(0), mspec(1), pl.BlockSpec((1, 1, d), lambda i, j, c: (0, 0, 0)),
                 pl.BlockSpec((nc, d, cw), lambda i, j, c: (0, 0, 0), pipeline_mode=pl.Buffered(1))]
    args += [mod, mod, ln_g.reshape(1, 1, d), w_bf]
    zspec = pl.BlockSpec((gb * rb, cw), lambda i, j, c: (i * nj + j, c))
    zshape = jax.ShapeDtypeStruct((g * r, nc * cw), BF16)
    if moe_out is not None:
        out_specs, out_shape = [zspec, xspec], [zshape, jax.ShapeDtypeStruct(x.shape, F32)]
    else:
        out_specs, out_shape = zspec, zshape
    return pl.pallas_call(
        functools.partial(_inproj_kernel, has_resid=moe_out is not None),
        grid=grid, in_specs=in_specs, out_specs=out_specs, out_shape=out_shape,
        scratch_shapes=[pltpu.VMEM((gb * rb, d), BF16)],
        compiler_params=_cparams(("arbitrary", "arbitrary", "arbitrary")),
        name="inproj",
    )(*args)


def _resid_kernel(x_ref, y4_ref, route_ref, g2_ref, o_ref):
    x = x_ref[...]
    o_ref[...] = x + g2_ref[...] * _combine(y4_ref, route_ref).reshape(x.shape)


def _resid_call(x, y4, route, mod, tile_off):
    g, r, d = x.shape
    gb, rb = _group_blocks(g, r)
    nj = r // rb
    xspec = pl.BlockSpec((gb, rb, d), lambda i, j: (i, j, 0))
    return pl.pallas_call(
        _resid_kernel,
        grid=(g // gb, nj),
        in_specs=[xspec] + _moe_out_specs(gb * rb, nj, tile_off)
        + [pl.BlockSpec((gb, 1, d), lambda i, j: (i, 0, 5))],
        out_specs=xspec,
        out_shape=jax.ShapeDtypeStruct(x.shape, F32),
        compiler_params=_cparams(("arbitrary", "arbitrary")),
        name="resid",
    )(x, y4, route, mod)


def _head_rms(x, g):
    return x * lax.rsqrt(jnp.mean(x * x, axis=-1, keepdims=True) + EPS) * g


def _q_col(kvh, g):
    p, kv_odd = divmod(kvh, 2)
    j, g_odd = divmod(g, 2)
    tile = p * 4 + (kv_odd ^ g_odd) * 2 + j
    return tile * 2 * HEAD_DIM + g_odd * HEAD_DIM


def _stack_q(q, qg, kvh, scale):
    parts = []
    for g in range(A_GROUP):
        c0 = _q_col(kvh, g)
        parts.append(_head_rms(q[:, c0:c0 + HEAD_DIM], qg) * scale)
    return jnp.concatenate(parts, axis=0).astype(BF16)


def _half_mats():
    r = lax.broadcasted_iota(jnp.int32, (2 * HEAD_DIM, 2 * HEAD_DIM), 0)
    c = lax.broadcasted_iota(jnp.int32, (2 * HEAD_DIM, 2 * HEAD_DIM), 1)
    seg = jnp.where(r // HEAD_DIM == c // HEAD_DIM, 1.0, 0.0).astype(BF16)
    swap = jnp.where((r + HEAD_DIM) % (2 * HEAD_DIM) == c, 1.0, 0.0).astype(BF16)
    return seg, swap


def _pair_rms(x, g2, seg):
    outs = []
    for t in range(x.shape[1] // (2 * HEAD_DIM)):
        xt = x[:, t * 2 * HEAD_DIM:(t + 1) * 2 * HEAD_DIM]
        hi, lo = _split_bf16(xt * xt)
        ss = _dot(hi, seg) + _dot(lo, seg)
        outs.append(xt * lax.rsqrt(ss * (1.0 / HEAD_DIM) + EPS) * g2)
    return jnp.concatenate(outs, axis=-1)


def _sink_col(sink_ref, kvh, rows):
    return jnp.concatenate(
        [jnp.full((rows, 1), sink_ref[kvh * A_GROUP + g], F32) for g in range(A_GROUP)], axis=0)


def _attn_prompt_kernel(sink_ref, q_ref, k_ref, v_ref, qg_ref, kg_ref, o_ref, nk_ref, nv_ref, kprev, vprev):
    n = pl.program_id(1)
    w = WINDOW

    @pl.when(n == 0)
    def _():
        kprev[...] = jnp.zeros_like(kprev)
        vprev[...] = jnp.zeros_like(vprev)

    seg, swap = _half_mats()
    tw = 2 * HEAD_DIM
    scale = HEAD_DIM ** -0.5
    v = v_ref[...]
    qn = (_pair_rms(q_ref[...].astype(F32), qg_ref[...], seg) * scale).astype(BF16)
    kn = _pair_rms(k_ref[...].astype(F32), kg_ref[...], seg)
    knb = kn.astype(BF16)
    kcat = jnp.concatenate([kprev[...], knb], axis=0)
    vcat = jnp.concatenate([vprev[...], v], axis=0)

    row = lax.broadcasted_iota(jnp.int32, (2 * w, 2 * w), 0)
    i = row % w
    j = lax.broadcasted_iota(jnp.int32, (2 * w, 2 * w), 1)
    lo = jnp.where(n == 0, w - 1, -1)
    mask = (j > i) & (j <= i + w) & (j > lo)
    sink_top = (j == 0) & (row < w)
    sink_bot = (j == 0) & (row >= w)
    lane_kv = lax.broadcasted_iota(jnp.int32, (2 * w, tw), 1)
    key_kv = lax.broadcasted_iota(jnp.int32, (2 * w, tw), 0)
    first_o = lax.broadcasted_iota(jnp.int32, (w, tw), 1) < HEAD_DIM
    scores, values = [], []
    for p in range(A_KV_HEADS // 2):
        kp = kcat[:, p * tw:(p + 1) * tw]
        vp = vcat[:, p * tw:(p + 1) * tw]
        kv_tiles = ((kp, vp), (_dot(kp, swap).astype(BF16), _dot(vp, swap).astype(BF16)))
        for variant, (kk, vv) in enumerate(kv_tiles):
            t0 = p * 4 + variant * 2
            qs = jnp.concatenate([qn[:, t0 * tw:(t0 + 1) * tw], qn[:, (t0 + 1) * tw:(t0 + 2) * tw]], axis=0)
            for half in range(2):
                keep = (lane_kv < HEAD_DIM) if half == 0 else (lane_kv >= HEAD_DIM)
                kh = jnp.where(keep, kk, jnp.zeros_like(kk))
                values.append(jnp.where(keep & (key_kv > 0), vv, jnp.zeros_like(vv)))
                kvh = 2 * p + (half ^ variant)
                s = jnp.where(mask, _dot_nt(qs, kh), NEG_BIG)
                s = jnp.where(sink_top, sink_ref[kvh * A_GROUP + half], s)
                scores.append(jnp.where(sink_bot, sink_ref[kvh * A_GROUP + 2 + half], s))
    s_all = jnp.concatenate(scores, axis=0)
    p_all = jnp.exp(s_all - jnp.max(s_all, axis=-1, keepdims=True))
    inv = 1.0 / jnp.sum(p_all, axis=-1, keepdims=True)
    p_all = p_all.astype(BF16)
    outs = [_dot(p_all[c * 2 * w:(c + 1) * 2 * w], values[c]) * inv[c * 2 * w:(c + 1) * 2 * w]
            for c in range(len(values))]
    for p in range(A_KV_HEADS // 2):
        o_a = outs[4 * p] + outs[4 * p + 1]
        o_b = outs[4 * p + 2] + outs[4 * p + 3]
        for jr in range(2):
            a = o_a[jr * w:(jr + 1) * w]
            b = o_b[jr * w:(jr + 1) * w]
            c_even = (2 * p * A_GROUP + 2 * jr) * HEAD_DIM
            c_odd = ((2 * p + 1) * A_GROUP + 2 * jr) * HEAD_DIM
            o_ref[:, c_even:c_even + tw] = jnp.where(first_o, a, b).astype(o_ref.dtype)
            o_ref[:, c_odd:c_odd + tw] = jnp.where(first_o, b, a).astype(o_ref.dtype)

    kprev[...] = knb
    vprev[...] = v

    @pl.when(n == pl.num_programs(1) - 1)
    def _():
        nk_ref[0] = kn
        nv_ref[0] = v.astype(F32)


def _attn_prompt_call(z, sinks, qg, kg, batch, seq):
    nb = seq // WINDOW
    kvw = A_KV_HEADS * HEAD_DIM
    row = lambda b, n, s: b * nb + n
    grid_spec = pltpu.PrefetchScalarGridSpec(
        num_scalar_prefetch=1,
        grid=(batch, nb),
        in_specs=[
            pl.BlockSpec((WINDOW, D_MODEL), lambda b, n, s: (row(b, n, s), Z_AQ // D_MODEL)),
            pl.BlockSpec((WINDOW, kvw), lambda b, n, s: (row(b, n, s), Z_AK // kvw)),
            pl.BlockSpec((WINDOW, kvw), lambda b, n, s: (row(b, n, s), Z_AV // kvw)),
            pl.BlockSpec((1, 2 * HEAD_DIM), lambda b, n, s: (0, 0)),
            pl.BlockSpec((1, 2 * HEAD_DIM), lambda b, n, s: (0, 0)),
        ],
        out_specs=[
            pl.BlockSpec((WINDOW, D_MODEL), lambda b, n, s: (row(b, n, s), 0)),
            pl.BlockSpec((1, WINDOW, kvw), lambda b, n, s: (b, 0, 0)),
            pl.BlockSpec((1, WINDOW, kvw), lambda b, n, s: (b, 0, 0)),
        ],
        scratch_shapes=[pltpu.VMEM((WINDOW, kvw), BF16), pltpu.VMEM((WINDOW, kvw), BF16)],
    )
    pair_gain = lambda g: jnp.tile(g, 2).reshape(1, 2 * HEAD_DIM)
    return pl.pallas_call(
        _attn_prompt_kernel,
        grid_spec=grid_spec,
        out_shape=[
            jax.ShapeDtypeStruct((batch * seq, D_MODEL), BF16),
            jax.ShapeDtypeStruct((batch, WINDOW, kvw), F32),
            jax.ShapeDtypeStruct((batch, WINDOW, kvw), F32),
        ],
        compiler_params=_cparams(("arbitrary", "arbitrary")),
        name="attn_prompt",
    )(sinks, z, z, z, pair_gain(qg), pair_gain(kg))


ATTN_S_SEQS = 8


def _attn_sample_kernel(sink_ref, q_ref, k_ref, v_ref, kc_ref, vc_ref, qg_ref, kg_ref, o_ref, nk_ref, nv_ref):
    _, sb, wb, kvw = kc_ref.shape
    l = q_ref.shape[0] // sb
    rows = sb * l
    q = q_ref[...].astype(F32)
    k = k_ref[...].astype(F32)
    v = v_ref[...]
    qg = qg_ref[...]
    kg = kg_ref[...]
    kn = jnp.concatenate(
        [_head_rms(k[:, h * HEAD_DIM:(h + 1) * HEAD_DIM], kg) for h in range(A_KV_HEADS)], axis=-1)
    nk_ref[...] = kn
    nv_ref[...] = v.astype(F32)
    knb = kn.astype(BF16)
    kc = kc_ref[0].reshape(sb * wb, kvw).astype(BF16)
    vc = vc_ref[0].reshape(sb * wb, kvw).astype(BF16)

    rq = lax.broadcasted_iota(jnp.int32, (A_GROUP * rows, sb * wb), 0) % rows
    cc = lax.broadcasted_iota(jnp.int32, (A_GROUP * rows, sb * wb), 1)
    mask_c = (rq // l == cc // wb) & (cc % wb > rq % l + (wb - WINDOW))
    rq2 = lax.broadcasted_iota(jnp.int32, (A_GROUP * rows, rows), 0) % rows
    cn = lax.broadcasted_iota(jnp.int32, (A_GROUP * rows, rows), 1)
    mask_n = (rq2 // l == cn // l) & (cn % l <= rq2 % l)
    scale = HEAD_DIM ** -0.5
    outs = []
    for h in range(A_KV_HEADS):
        hs = slice(h * HEAD_DIM, (h + 1) * HEAD_DIM)
        qs = _stack_q(q, qg, h, scale)
        s_c = jnp.where(mask_c, _dot_nt(qs, kc[:, hs]), NEG_BIG)
        s_n = jnp.where(mask_n, _dot_nt(qs, knb[:, hs]), NEG_BIG)
        sink = _sink_col(sink_ref, h, rows)
        m = jnp.maximum(jnp.maximum(jnp.max(s_c, axis=-1, keepdims=True),
                                    jnp.max(s_n, axis=-1, keepdims=True)), sink)
        p_c = jnp.exp(s_c - m)
        p_n = jnp.exp(s_n - m)
        denom = (jnp.sum(p_c, axis=-1, keepdims=True) + jnp.sum(p_n, axis=-1, keepdims=True)
                 + jnp.exp(sink - m))
        o = (_dot(p_c.astype(BF16), vc[:, hs]) + _dot(p_n.astype(BF16), v[:, hs])) / denom
        outs += [o[g * rows:(g + 1) * rows] for g in range(A_GROUP)]
    o_ref[...] = jnp.concatenate(outs, axis=-1).astype(o_ref.dtype)


def _attn_sample_call(z, kc, vc, sinks, qg, kg, batch, l, layer):
    kvw = A_KV_HEADS * HEAD_DIM
    sb = ATTN_S_SEQS
    rows = sb * l
    wb = kc.shape[1]
    grid_spec = pltpu.PrefetchScalarGridSpec(
        num_scalar_prefetch=1,
        grid=(batch // sb,),
        in_specs=[
            pl.BlockSpec((rows, D_MODEL), lambda i, s: (i, Z_AQ // D_MODEL)),
            pl.BlockSpec((rows, kvw), lambda i, s: (i, Z_AK // kvw)),
            pl.BlockSpec((rows, kvw), lambda i, s: (i, Z_AV // kvw)),
            pl.BlockSpec((1, sb, wb, kvw), lambda i, s: (layer, i, 0, 0)),
            pl.BlockSpec((1, sb, wb, kvw), lambda i, s: (layer, i, 0, 0)),
            pl.BlockSpec((1, HEAD_DIM), lambda i, s: (0, 0)),
            pl.BlockSpec((1, HEAD_DIM), lambda i, s: (0, 0)),
        ],
        out_specs=[
            pl.BlockSpec((rows, D_MODEL), lambda i, s: (i, 0)),
            pl.BlockSpec((rows, kvw), lambda i, s: (i, 0)),
            pl.BlockSpec((rows, kvw), lambda i, s: (i, 0)),
        ],
    )
    return pl.pallas_call(
        _attn_sample_kernel,
        grid_spec=grid_spec,
        out_shape=[
            jax.ShapeDtypeStruct((batch * l, D_MODEL), BF16),
            jax.ShapeDtypeStruct((batch * l, kvw), F32),
            jax.ShapeDtypeStruct((batch * l, kvw), F32),
        ],
        compiler_params=_cparams(("arbitrary",)),
        name="attn_sample",
    )(sinks, z, z, z, kc, vc, qg.reshape(1, HEAD_DIM), kg.reshape(1, HEAD_DIM))


def _ret_tables(chunk, pos0, length):
    h = np.arange(R_HEADS, dtype=np.float64)
    log_gamma = np.log1p(-np.exp2(-5.0 - h))
    idx = np.arange(chunk, dtype=np.float64)
    diff = idx[:, None] - idx[None, :]
    intra = np.where(diff[None] >= 0, np.exp(np.maximum(diff, 0.0)[None] * log_gamma[:, None, None]), 0.0)
    q_decay = np.exp((idx + 1.0)[:, None] * log_gamma[None, :])
    k_decay = np.exp((chunk - 1.0 - idx)[:, None] * log_gamma[None, :])
    c_decay = np.exp(chunk * log_gamma)
    qd = np.repeat(q_decay, R_DK, axis=1)
    kd = np.repeat(k_decay, R_DK, axis=1)
    inv_freq = ROPE_BASE ** (-np.arange(0, R_DK, 2, dtype=np.float64) / R_DK)
    ang = (pos0 + np.arange(length, dtype=np.float64))[:, None] * inv_freq[None, :]
    cos = np.tile(np.concatenate([np.cos(ang), np.cos(ang)], axis=1), (1, R_HEADS))
    sin = np.tile(np.concatenate([-np.sin(ang), np.sin(ang)], axis=1), (1, R_HEADS))
    f = lambda a: jnp.asarray(a, F32)
    return f(intra), f(qd), f(kd), [float(c) for c in c_decay], f(cos), f(sin)


def _rope(x, cos, sin):
    n = x.shape[-1]
    half = R_DK // 2
    lane = lax.broadcasted_iota(jnp.int32, x.shape, 1)
    up = pltpu.roll(x, n - half, axis=1)
    dn = pltpu.roll(x, half, axis=1)
    partner = jnp.where(lane % R_DK < half, up, dn)
    return x * cos + partner * sin


def _rope_mxu(x, cos, sin):
    tw = 2 * R_DK
    half = R_DK // 2
    r = lax.broadcasted_iota(jnp.int32, (tw, tw), 0)
    c = lax.broadcasted_iota(jnp.int32, (tw, tw), 1)
    perm = jnp.where((r // R_DK == c // R_DK) & ((r + half) % R_DK == c % R_DK), 1.0, 0.0).astype(BF16)
    parts = []
    for t in range(x.shape[1] // tw):
        hi, lo = _split_bf16(x[:, t * tw:(t + 1) * tw])
        parts.append(_dot(hi, perm) + _dot(lo, perm))
    return x * cos + jnp.concatenate(parts, axis=-1) * sin


def _head_ln(o, g):
    mu = jnp.mean(o, axis=-1, keepdims=True)
    oc = o - mu
    var = jnp.mean(oc * oc, axis=-1, keepdims=True)
    return oc * lax.rsqrt(var + EPS) * g


def _silu(x):
    return x * jax.nn.sigmoid(x)


def _ret_prompt_kernel(q_ref, k_ref, v_ref, g_ref, cos_ref, sin_ref, intra_ref, qd_ref, kd_ref, ng_ref,
                       o_ref, s_ref, state, *, c_decay):
    n = pl.program_id(1)

    @pl.when(n == 0)
    def _():
        state[...] = jnp.zeros_like(state)

    cos = cos_ref[...]
    sin = sin_ref[...]
    q = _rope_mxu(q_ref[...].astype(F32), cos, sin)
    k = _rope_mxu(k_ref[...].astype(F32), cos, sin) * (R_DK ** -0.5)
    qb = q.astype(BF16)
    kb = k.astype(BF16)
    qdb = (q * qd_ref[...]).astype(BF16)
    kdb = (k * kd_ref[...]).astype(BF16)
    c = q.shape[0]
    ks = [slice(h * R_DK, (h + 1) * R_DK) for h in range(R_HEADS)]
    vs = [slice(h * R_DV, (h + 1) * R_DV) for h in range(R_HEADS)]
    a = jnp.concatenate([_dot_nt(qb[:, ks[h]], kb[:, ks[h]]) for h in range(R_HEADS)], axis=0)
    ab = (a * intra_ref[...].reshape(R_HEADS * c, c)).astype(BF16)
    outs = []
    for h in range(R_HEADS):
        vh = v_ref[:, vs[h]]
        s_old = state[h]
        outs.append(_dot(ab[h * c:(h + 1) * c], vh) + _dot(qdb[:, ks[h]], s_old.astype(BF16)))
        state[h] = s_old * c_decay[h] + _dot_tn(kdb[:, ks[h]], vh)
    y = _head_ln(jnp.concatenate(outs, axis=0), 1.0)
    gate = _silu(g_ref[...].astype(F32)) * ng_ref[...]
    for h in range(R_HEADS):
        o_ref[:, vs[h]] = (y[h * c:(h + 1) * c] * gate[:, vs[h]]).astype(o_ref.dtype)

    @pl.when(n == pl.num_programs(1) - 1)
    def _():
        s_ref[0] = state[...]


def _ret_prompt_call(z, ret_norm_g, batch, seq):
    c = R_CHUNK
    nc = seq // c
    intra, qd, kd, c_decay, cos, sin = _ret_tables(c, 0, seq)
    qkw = R_HEADS * R_DK
    row = lambda b, n: b * nc + n
    const2 = lambda b, n: (0, 0)
    return pl.pallas_call(
        functools.partial(_ret_prompt_kernel, c_decay=c_decay),
        grid=(batch, nc),
        in_specs=[
            pl.BlockSpec((c, qkw), lambda b, n: (row(b, n), Z_RQ // qkw)),
            pl.BlockSpec((c, qkw), lambda b, n: (row(b, n), Z_RK // qkw)),
            pl.BlockSpec((c, D_MODEL), lambda b, n: (row(b, n), Z_RV // D_MODEL)),
            pl.BlockSpec((c, D_MODEL), lambda b, n: (row(b, n), Z_RG // D_MODEL)),
            pl.BlockSpec((c, qkw), lambda b, n: (n, 0)),
            pl.BlockSpec((c, qkw), lambda b, n: (n, 0)),
            pl.BlockSpec((R_HEADS, c, c), lambda b, n: (0, 0, 0)),
            pl.BlockSpec((c, qkw), const2),
            pl.BlockSpec((c, qkw), const2),
            pl.BlockSpec((1, D_MODEL), const2),
        ],
        out_specs=[
            pl.BlockSpec((c, D_MODEL), lambda b, n: (row(b, n), 0)),
            pl.BlockSpec((1, R_HEADS, R_DK, R_DV), lambda b, n: (b, 0, 0, 0)),
        ],
        out_shape=[
            jax.ShapeDtypeStruct((batch * seq, D_MODEL), BF16),
            jax.ShapeDtypeStruct((batch, R_HEADS, R_DK, R_DV), F32),
        ],
        scratch_shapes=[pltpu.VMEM((R_HEADS, R_DK, R_DV), F32)],
        compiler_params=_cparams(("arbitrary", "arbitrary")),
        name="ret_prompt",
    )(z, z, z, z, cos, sin, intra, qd, kd, ret_norm_g.reshape(1, D_MODEL))


RET_S_SEQS = 8


def _ret_sample_kernel(q_ref, k_ref, v_ref, g_ref, s0_ref, cos_ref, sin_ref, intra_ref, qd_ref, kd_ref, ng_ref,
                       o_ref, s_ref, *, c_decay, l):
    sb = s0_ref.shape[1]
    rows = sb * l
    cos = cos_ref[...]
    sin = sin_ref[...]
    q = _rope(q_ref[...].astype(F32), cos, sin)
    k = _rope(k_ref[...].astype(F32), cos, sin) * (R_DK ** -0.5)
    qb = q.astype(BF16)
    kb = k.astype(BF16)
    qdb = (q * qd_ref[...]).astype(BF16)
    kdb = (k * kd_ref[...]).astype(BF16)
    for h in range(R_HEADS):
        ks = slice(h * R_DK, (h + 1) * R_DK)
        vs = slice(h * R_DV, (h + 1) * R_DV)
        vh = v_ref[:, vs]
        a = _dot_nt(qb[:, ks], kb[:, ks]) * intra_ref[h]
        o = _dot(a.astype(BF16), vh)
        cross, new_s = [], []
        for b in range(sb):
            rs = slice(b * l, (b + 1) * l)
            s_old = s0_ref[0, b, h]
            cross.append(_dot(qdb[rs, ks], s_old.astype(BF16)))
            s_ref[b, h] = s_old * c_decay[h] + _dot_tn(kdb[rs, ks], vh[rs])
        o = o + jnp.concatenate(cross, axis=0)
        y = _head_ln(o, ng_ref[:, vs]) * _silu(g_ref[:, vs].astype(F32))
        o_ref[:, vs] = y.astype(o_ref.dtype)


def _ret_sample_call(z, s0, ret_norm_g, batch, l, layer):
    c = math.gcd(l, R_CHUNK)
    assert c == l, "sample step expects a single retention chunk"
    sb = RET_S_SEQS
    rows = sb * l
    intra, qd, kd, c_decay, cos, sin = _ret_tables(c, PAST_LEN, l)
    eye = jnp.eye(sb, dtype=F32)
    intra_bd = jnp.einsum("ab,hij->haibj", eye, intra).reshape(R_HEADS, rows, rows)
    tile = lambda t: jnp.tile(t, (sb, 1))
    qkw = R_HEADS * R_DK
    const2 = lambda i: (0, 0)
    return pl.pallas_call(
        functools.partial(_ret_sample_kernel, c_decay=c_decay, l=l),
        grid=(batch // sb,),
        in_specs=[
            pl.BlockSpec((rows, qkw), lambda i: (i, Z_RQ // qkw)),
            pl.BlockSpec((rows, qkw), lambda i: (i, Z_RK // qkw)),
            pl.BlockSpec((rows, D_MODEL), lambda i: (i, Z_RV // D_MODEL)),
            pl.BlockSpec((rows, D_MODEL), lambda i: (i, Z_RG // D_MODEL)),
            pl.BlockSpec((1, sb, R_HEADS, R_DK, R_DV), lambda i: (layer, i, 0, 0, 0)),
            pl.BlockSpec((rows, qkw), const2),
            pl.BlockSpec((rows, qkw), const2),
            pl.BlockSpec((R_HEADS, rows, rows), lambda i: (0, 0, 0)),
            pl.BlockSpec((rows, qkw), const2),
            pl.BlockSpec((rows, qkw), const2),
            pl.BlockSpec((1, D_MODEL), const2),
        ],
        out_specs=[
            pl.BlockSpec((rows, D_MODEL), lambda i: (i, 0)),
            pl.BlockSpec((sb, R_HEADS, R_DK, R_DV), lambda i: (i, 0, 0, 0)),
        ],
        out_shape=[
            jax.ShapeDtypeStruct((batch * l, D_MODEL), BF16),
            jax.ShapeDtypeStruct(s0.shape[1:], F32),
        ],
        compiler_params=_cparams(("arbitrary",)),
        name="ret_sample",
    )(z, z, z, z, s0, tile(cos), tile(sin), intra_bd, tile(qd), tile(kd), ret_norm_g.reshape(1, D_MODEL))


def _gelu(x):
    return jax.nn.gelu(x, approximate=True)


def _route_rows(logits, carry):
    tm = logits.shape[0]
    lane = lax.broadcasted_iota(jnp.int32, logits.shape, 1).astype(F32)
    work = logits
    sel = jnp.zeros(logits.shape, F32)
    vals, idxs = [], []
    for _ in range(TOP_K):
        m = jnp.max(work, axis=-1, keepdims=True)
        idx = jnp.min(jnp.where(work == m, lane, float(ROUTER_PAD)), axis=-1, keepdims=True)
        hit = lane == idx
        vals.append(m)
        idxs.append(idx)
        sel = jnp.where(hit, 1.0, sel)
        work = jnp.where(hit, -3e38, work)
    ex = [jnp.exp(v - vals[0]) for v in vals]
    den = ex[0] + ex[1] + ex[2] + ex[3]
    r = lax.broadcasted_iota(jnp.int32, (tm, tm), 0)
    c = lax.broadcasted_iota(jnp.int32, (tm, tm), 1)
    before = jnp.where(c < r, 1.0, 0.0).astype(BF16)
    rank = _dot(before, sel.astype(BF16)) + carry
    route = jnp.zeros(logits.shape, F32)
    for k in range(TOP_K):
        route = jnp.where(lane == float(k), ex[k] / den, route)
        route = jnp.where(lane == float(TOP_K + k), idxs[k], route)
        rk = jnp.sum(jnp.where(lane == idxs[k], rank, 0.0), axis=-1, keepdims=True)
        route = jnp.where(lane == float(2 * TOP_K + k), rk, route)
    return route, carry + jnp.sum(sel, axis=0, keepdims=True)


def _merge_kernel(*refs, emit_cv):
    (x_ref, oa_ref, ob_ref, cu_ref, cv_ref, ga_ref, gb_ref, gc_ref, g1_ref, sh2_ref, sc2_ref,
     bg_ref, lnf_ref, gmg_ref, gmb_ref, mix_ref, mixb_ref, wout_ref, rwh_ref, rwl_ref, rb_ref) = refs[:21]
    cnt_ref = refs[21]
    if emit_cv:
        xo_ref, h2_ref, route_ref, cnto_ref, cvo_ref, carry = refs[22:]
    else:
        xo_ref, h2_ref, route_ref, cnto_ref, carry = refs[22:]
    first = (pl.program_id(0) == 0) & (pl.program_id(1) == 0)

    @pl.when(first)
    def _():
        carry[...] = cnt_ref[...]

    x = x_ref[...]
    gbk, rb, d = x.shape
    tm = gbk * rb
    cw = C_CHUNK
    gw = d // C_GROUPS

    cv = _gelu(cv_ref[...].astype(F32))
    mu = jnp.mean(cv, axis=-1, keepdims=True)
    cvc = cv - mu
    var = jnp.mean(cvc * cvc, axis=-1, keepdims=True)
    cv = cvc * lax.rsqrt(var + EPS) * gmg_ref[...] + gmb_ref[...]
    if emit_cv:
        cvo_ref[...] = cv
    cvb = cv.astype(BF16)
    mixed_rows = []
    for c in range(tm // cw):
        rs = slice(c * cw, (c + 1) * cw)
        cols = [_dot(mix_ref[g], cvb[rs, g * gw:(g + 1) * gw]) for g in range(C_GROUPS)]
        mixed_rows.append(jnp.concatenate(cols, axis=-1) + mixb_ref[...])
    mixed = jnp.concatenate(mixed_rows, axis=0)
    oc = _gelu(cu_ref[...].astype(F32)) * mixed

    bg = bg_ref[...]
    ga = jax.nn.sigmoid(ga_ref[...].astype(F32) + bg[0:1])
    gb = jax.nn.sigmoid(gb_ref[...].astype(F32) + bg[1:2])
    gc = jax.nn.sigmoid(gc_ref[...].astype(F32) + bg[2:3])
    merged = ga * oa_ref[...].astype(F32) + gb * ob_ref[...].astype(F32) + gc * oc
    y = _dot(merged.astype(BF16), wout_ref[...])
    x = x + g1_ref[...] * y.reshape(gbk, rb, d)
    xo_ref[...] = x

    h2 = (_rms(x) * lnf_ref[...] * (1.0 + sc2_ref[...]) + sh2_ref[...]).reshape(tm, d)
    h2b = h2.astype(BF16)
    h2_ref[0], h2_ref[1] = _pack_rows(h2)
    h2l = (h2 - h2b.astype(F32)).astype(BF16)
    rwh = rwh_ref[...]
    logits = _dot(h2b, rwh) + _dot(h2l, rwh) + _dot(h2b, rwl_ref[...]) + rb_ref[...]
    route, counts = _route_rows(logits, carry[...])
    route_ref[...] = route
    carry[...] = counts
    cnto_ref[...] = counts


def _merge_call(x, z, oa, ob, mod, b_gate, ln_ffn_g, gm_ln_g, gm_ln_b, mix, mixb, w_out_bf, rw_hi, rw_lo, rb,
                counts, emit_cv):
    g, r, d = x.shape
    gb, rb_ = _group_blocks(g, r)
    tm = gb * rb_
    nj = r // rb_
    t = g * r
    xspec = pl.BlockSpec((gb, rb_, d), lambda i, j: (i, j, 0))
    rows = lambda col: pl.BlockSpec((tm, d), lambda i, j: (i * nj + j, col))
    mspec = lambda col: pl.BlockSpec((gb, 1, d), lambda i, j: (i, 0, col))
    const = lambda shape: pl.BlockSpec(shape, lambda i, j: (0,) * len(shape))
    in_specs = [
        xspec, rows(0), rows(0),
        rows(Z_CU // d), rows(Z_CV // d), rows(Z_GA // d), rows(Z_GB // d), rows(Z_GC // d),
        mspec(2), mspec(3), mspec(4),
        const((3, d)), const((1, 1, d)), const((1, d)), const((1, d)),
        const((C_GROUPS, C_CHUNK, C_CHUNK)), const((C_CHUNK, d)),
        const((d, d)), const((d, ROUTER_PAD)), const((d, ROUTER_PAD)), const((1, ROUTER_PAD)),
        const((1, ROUTER_PAD)),
    ]
    out_specs = [xspec, pl.BlockSpec((2, tm, PACK_W), lambda i, j: (0, i * nj + j, 0)),
                 pl.BlockSpec((tm, ROUTER_PAD), lambda i, j: (i * nj + j, 0)), const((1, ROUTER_PAD))]
    out_shape = [jax.ShapeDtypeStruct(x.shape, F32), jax.ShapeDtypeStruct((2, t, PACK_W), U32),
                 jax.ShapeDtypeStruct((t, ROUTER_PAD), F32), jax.ShapeDtypeStruct((1, ROUTER_PAD), F32)]
    if emit_cv:
        out_specs.append(rows(0))
        out_shape.append(jax.ShapeDtypeStruct((t, d), F32))
    return pl.pallas_call(
        functools.partial(_merge_kernel, emit_cv=emit_cv),
        grid=(g // gb, nj), in_specs=in_specs, out_specs=out_specs, out_shape=out_shape,
        scratch_shapes=[pltpu.VMEM((1, ROUTER_PAD), F32)],
        compiler_params=_cparams(("arbitrary", "arbitrary")),
        name="merge",
    )(x, oa, ob, z, z, z, z, z, mod, mod, mod,
      b_gate.reshape(3, d), ln_ffn_g.reshape(1, 1, d), gm_ln_g.reshape(1, d), gm_ln_b.reshape(1, d),
      mix, mixb, w_out_bf, rw_hi, rw_lo, rb, counts)


def _moe_kernel(be_ref, nx_ref, nh_ref, nu_ref, x_ref, w1_hbm, b1_ref, w2_hbm, b2_ref, o_ref,
                w1s, w2s, w1b, w2b, sem, *, layer):
    i = pl.program_id(0)
    e = be_ref[i]
    prev = be_ref[jnp.maximum(i - 1, 0)]

    def weight_copies(ex):
        return (pltpu.make_async_copy(w1_hbm.at[layer, ex], w1s, sem.at[0]),
                pltpu.make_async_copy(w2_hbm.at[layer, ex], w2s, sem.at[1]))

    @pl.when(i == 0)
    def _():
        for cp in weight_copies(e):
            cp.start()

    @pl.when((i == 0) | (e != prev))
    def _():
        for cp in weight_copies(e):
            cp.wait()
        w1b[...] = w1s[...].astype(BF16)
        w2b[...] = w2s[...].astype(BF16)
        nxt = nx_ref[i]

        @pl.when(nxt >= 0)
        def _():
            for cp in weight_copies(nxt):
                cp.start()

    def experts(rows):
        rs = slice(0, rows)
        xb = _unpack_rows(x_ref[0, rs], x_ref[1, rs]).astype(BF16)
        hdn = _dot(xb, w1b[...]) + b1_ref[0, 0]
        g = jnp.minimum(hdn[:, :D_FF], SWIGLU_LIMIT)
        up = jnp.clip(hdn[:, D_FF:], -SWIGLU_LIMIT, SWIGLU_LIMIT)
        act = (up + 1.0) * (g * jax.nn.sigmoid(g * SWIGLU_ALPHA))
        o_ref[0, rs], o_ref[1, rs] = _pack_rows(_dot(act.astype(BF16), w2b[...]) + b2_ref[0, 0])

    for nh in range(1, MOE_HALVES + 1):
        pl.when(nh_ref[i] == nh)(functools.partial(experts, nh * MOE_ROWS))


def _moe_call(xb, block_e, next_e, n_halves, n_used, w1, b1, w2, b2, layer):
    _, n_rows, _ = xb.shape
    step_rows = MOE_HALVES * MOE_ROWS
    nblk = n_rows // step_rows
    depth, ne, d, f2 = w1.shape
    last = lambda i, be, nx, nh, nu: jnp.minimum(i, nu[0] - 1)
    xspec = pl.BlockSpec((2, step_rows, PACK_W), lambda i, be, nx, nh, nu: (0, last(i, be, nx, nh, nu), 0))
    grid_spec = pltpu.PrefetchScalarGridSpec(
        num_scalar_prefetch=4,
        grid=(nblk,),
        in_specs=[
            xspec,
            pl.BlockSpec(memory_space=pl.ANY),
            pl.BlockSpec((1, 1, 1, f2), lambda i, be, nx, nh, nu: (layer, be[i], 0, 0)),
            pl.BlockSpec(memory_space=pl.ANY),
            pl.BlockSpec((1, 1, 1, d), lambda i, be, nx, nh, nu: (layer, be[i], 0, 0)),
        ],
        out_specs=xspec,
        scratch_shapes=[pltpu.VMEM((d, f2), F32), pltpu.VMEM((f2 // 2, d), F32),
                        pltpu.VMEM((d, f2), BF16), pltpu.VMEM((f2 // 2, d), BF16),
                        pltpu.SemaphoreType.DMA((2,))],
    )
    return pl.pallas_call(
        functools.partial(_moe_kernel, layer=layer),
        grid_spec=grid_spec,
        out_shape=jax.ShapeDtypeStruct(xb.shape, U32),
        compiler_params=_cparams(("arbitrary",)),
        name="moe",
    )(block_e, next_e, n_halves, n_used, xb, w1, b1.reshape(depth, ne, 1, f2), w2, b2.reshape(depth, ne, 1, d))


def _sc_mesh():
    return plsc.VectorSubcoreMesh(core_axis_name="core", subcore_axis_name="subcore")


def _sc_scatter_rows(x, idx, n_out):
    t, c = x.shape
    kk = idx.shape[0]

    @pl.kernel(out_type=jax.ShapeDtypeStruct((n_out, c), x.dtype), mesh=_sc_mesh(), scratch_types=[])
    def scatter(x_hbm, i_hbm, o_hbm):
        def body(x_vmem, i_vmem):
            for k in range(kk):
                pltpu.sync_copy(x_vmem, o_hbm.at[i_vmem.at[k]])

        pltpu.emit_pipeline(
            body,
            grid=(t // SC_WINDOW,),
            in_specs=[pl.BlockSpec((SC_WINDOW, c), index_map=lambda i: (i, 0)),
                      pl.BlockSpec((kk, SC_WINDOW), index_map=lambda i: (0, i))],
            out_specs=[],
            core_axis_name=("core", "subcore"),
            dimension_semantics=(pltpu.PARALLEL,),
        )(x_hbm, i_hbm)

    return scatter(x, idx)


def _sc_gather_rows(data, idx):
    n = idx.shape[0]
    c = data.shape[1]

    @pl.kernel(out_type=jax.ShapeDtypeStruct((n, c), data.dtype), mesh=_sc_mesh(), scratch_types=[])
    def gather(x_hbm, i_hbm, o_hbm):
        def body(i_vmem, o_vmem):
            pltpu.sync_copy(x_hbm.at[i_vmem.at[0]], o_vmem)

        pltpu.emit_pipeline(
            body,
            grid=(n // SC_WINDOW,),
            in_specs=[pl.BlockSpec((1, SC_WINDOW), index_map=lambda i: (0, i))],
            out_specs=[pl.BlockSpec((SC_WINDOW, c), index_map=lambda i: (i, 0))],
            core_axis_name=("core", "subcore"),
            dimension_semantics=(pltpu.PARALLEL,),
        )(i_hbm, o_hbm)

    return gather(data, idx.reshape(1, n))


def _plan(route, counts):
    t = route.shape[0]
    step_rows = MOE_HALVES * MOE_ROWS
    n_steps = -(-t * TOP_K // step_rows) + N_EXPERTS
    e4 = route[:, TOP_K:2 * TOP_K].astype(jnp.int32)
    r4 = route[:, 2 * TOP_K:3 * TOP_K].astype(jnp.int32)
    cnt = counts[0, :N_EXPERTS].astype(jnp.int32)
    halves = (cnt + MOE_ROWS - 1) // MOE_ROWS
    steps = (halves + MOE_HALVES - 1) // MOE_HALVES
    send = jnp.cumsum(steps)
    sstart = send - steps
    onehot = e4[:, :, None] == jnp.arange(N_EXPERTS, dtype=jnp.int32)[None, None, :]
    dest = (r4 + jnp.sum(jnp.where(onehot, (sstart * step_rows)[None, None, :], 0), axis=-1)).T
    step = jnp.arange(n_steps, dtype=jnp.int32)
    step_e = jnp.sum((step[:, None] >= send[None, :]).astype(jnp.int32), axis=-1)
    n_used = send[-1].astype(jnp.int32).reshape(1)
    step_e = jnp.minimum(step_e, jnp.sum((send < send[-1]).astype(jnp.int32))).astype(jnp.int32)
    n_halves = jnp.clip(halves[step_e] - MOE_HALVES * (step - sstart[step_e]), 0, MOE_HALVES)
    n_halves = jnp.where(step < n_used[0], n_halves, 0).astype(jnp.int32)
    after = send[step_e]
    next_e = jnp.where(after < n_used[0], step_e[jnp.minimum(after, n_steps - 1)], -1).astype(jnp.int32)
    return dest, step_e, next_e, n_halves, n_used, n_steps * step_rows


def _moe(h2p, route, counts, w1, b1, w2, b2, layer):
    _, t, pw = h2p.shape
    dest, step_e, next_e, n_halves, n_used, n_rows = _plan(route, counts)
    idx_s = jnp.concatenate([dest, dest + n_rows], axis=1)
    xb = _sc_scatter_rows(h2p.reshape(2 * t, pw), idx_s, 2 * n_rows).reshape(2, n_rows, pw)
    yb = _moe_call(xb, step_e, next_e, n_halves, n_used, w1, b1, w2, b2, layer)
    idx_g = jnp.concatenate([dest.reshape(-1), dest.reshape(-1) + n_rows])
    y4 = _sc_gather_rows(yb.reshape(2 * n_rows, pw), idx_g)
    return y4.reshape(2, TOP_K, t, pw)


def _reorder_w_in(w):
    o = _ORIG
    seg = lambda a, n: w[:, a:a + n]
    d = D_MODEL
    q_heads = sorted(((_q_col(kv, g), kv * A_GROUP + g) for kv in range(A_KV_HEADS) for g in range(A_GROUP)))
    aq = jnp.concatenate([seg(o["aq"] + h * HEAD_DIM, HEAD_DIM) for _, h in q_heads], axis=1)
    parts = [aq, seg(o["rv"], d), seg(o["rg"], d), seg(o["cu"], d), seg(o["cv"], d),
             seg(o["mg"], d), seg(o["mg"] + d, d), seg(o["mg"] + 2 * d, d),
             seg(o["rq"], 512), seg(o["rk"], 512), seg(o["ak"], 256), seg(o["av"], 256)]
    w = jnp.concatenate(parts, axis=1).astype(BF16)
    return w.reshape(D_MODEL, IN_CHUNKS, D_IN // IN_CHUNKS).transpose(1, 0, 2)


def _gmlp_tables(ws, bs, chunk_len, rows):
    causal = jnp.tril(jnp.ones((C_CHUNK, C_CHUNK), dtype=bool))
    w = jnp.where(causal[None], ws, 0.0)[:, :chunk_len, :chunk_len]
    reps = rows // chunk_len
    eye = jnp.eye(reps, dtype=F32)
    mix = jnp.einsum("ab,gts->gatbs", eye, w).reshape(C_GROUPS, rows, rows).astype(BF16)
    b = jnp.tile(bs[:, :chunk_len].T, (reps, 1))
    mixb = jnp.repeat(b, D_MODEL // C_GROUPS, axis=1)
    return mix, mixb


def kernel(x_prompt, x_sample, c_prompt, c_sample, cache_attn_k, cache_attn_v, state_retention, ln_mix_g, ln_ffn_g, w_ada, b_ada, w_in, b_gate, q_norm_g, k_norm_g, attn_sinks, ret_norm_g, gm_ln_g, gm_ln_b, gm_ws, gm_bs, w_out, router_w, router_b, moe_w1, moe_b1, moe_w2, moe_b2):
    bp, lp, d = x_prompt.shape
    bs, ls, _ = x_sample.shape
    tp, ts = bp * lp, bs * ls
    kvw = A_KV_HEADS * HEAD_DIM
    wb = cache_attn_k.shape[2]

    mod_all = _ada_call(jnp.concatenate([c_prompt, c_sample], axis=0), w_ada, b_ada)

    cache_k = cache_attn_k.reshape(DEPTH, bs, wb, kvw)
    cache_v = cache_attn_v.reshape(DEPTH, bs, wb, kvw)
    xp, xs = x_prompt, x_sample
    y4 = route = None
    mod_p = mod_s = None
    pk, pv, ps, sk, sv, ss, sg = [], [], [], [], [], [], []
    for l in range(DEPTH):
        prev_mod_p, prev_mod_s = mod_p, mod_s
        mod_p = mod_all[l, :bp].reshape(bp, 1, 6 * d)
        mod_s = mod_all[l, bp:].reshape(bs, 1, 6 * d)
        w_in_bf = _reorder_w_in(w_in[l])
        w_out_bf = w_out[l].astype(BF16)
        rw = jnp.pad(router_w[l], ((0, 0), (0, ROUTER_PAD - N_EXPERTS)))
        rw_hi, rw_lo = _split_bf16(rw)
        rb = jnp.pad(router_b[l], (0, ROUTER_PAD - N_EXPERTS), constant_values=NEG_BIG).reshape(1, ROUTER_PAD)

        if l == 0:
            zp = _inproj_call(xp, mod_p, ln_mix_g[l], w_in_bf)
            zs = _inproj_call(xs, mod_s, ln_mix_g[l], w_in_bf)
        else:
            zp, xp = _inproj_call(xp, mod_p, ln_mix_g[l], w_in_bf, moe_out=(y4, route, prev_mod_p, 0))
            zs, xs = _inproj_call(xs, mod_s, ln_mix_g[l], w_in_bf, moe_out=(y4, route, prev_mod_s, tp // ROW_TILE))
        oa_p, k_p, v_p = _attn_prompt_call(zp, attn_sinks[l], q_norm_g[l], k_norm_g[l], bp, lp)
        oa_s, k_s, v_s = _attn_sample_call(zs, cache_k, cache_v, attn_sinks[l], q_norm_g[l], k_norm_g[l], bs, ls, l)
        ob_p, s_p = _ret_prompt_call(zp, ret_norm_g[l], bp, lp)
        ob_s, s_s = _ret_sample_call(zs, state_retention, ret_norm_g[l], bs, ls, l)

        mix_p, mixb_p = _gmlp_tables(gm_ws[l], gm_bs[l], C_CHUNK, C_CHUNK)
        mix_s, mixb_s = _gmlp_tables(gm_ws[l], gm_bs[l], ls, C_CHUNK)
        common = (b_gate[l], ln_ffn_g[l], gm_ln_g[l], gm_ln_b[l])
        tail = (w_out_bf, rw_hi, rw_lo, rb)
        zero_counts = jnp.zeros((1, ROUTER_PAD), F32)
        xp, h2_p, rt_p, cnt_p = _merge_call(xp, zp, oa_p, ob_p, mod_p, *common, mix_p, mixb_p, *tail,
                                            zero_counts, emit_cv=False)
        xs, h2_s, rt_s, cnt, cv_s = _merge_call(xs, zs, oa_s, ob_s, mod_s, *common, mix_s, mixb_s, *tail,
                                                cnt_p, emit_cv=True)
        route = jnp.concatenate([rt_p, rt_s], axis=0)
        y4 = _moe(jnp.concatenate([h2_p, h2_s], axis=1), route, cnt, moe_w1, moe_b1, moe_w2, moe_b2, l)

        pk.append(k_p.reshape(bp, WINDOW, A_KV_HEADS, HEAD_DIM))
        pv.append(v_p.reshape(bp, WINDOW, A_KV_HEADS, HEAD_DIM))
        ps.append(s_p)
        sk.append(k_s.reshape(bs, ls, A_KV_HEADS, HEAD_DIM))
        sv.append(v_s.reshape(bs, ls, A_KV_HEADS, HEAD_DIM))
        ss.append(s_s)
        sg.append(cv_s.reshape(bs, ls, d))

    xp = _resid_call(xp, y4, route, mod_p, 0)
    xs = _resid_call(xs, y4, route, mod_s, tp // ROW_TILE)
    return (xp, xs, jnp.stack(pk), jnp.stack(pv), jnp.stack(ps), jnp.stack(sk), jnp.stack(sv),
            jnp.stack(ss), jnp.stack(sg))
```

```python
import functools
import math

import numpy as np
import jax
import jax.numpy as jnp
from jax import lax
from jax.experimental import pallas as pl
from jax.experimental.pallas import tpu as pltpu
from jax.experimental.pallas import tpu_sc as plsc

F32 = jnp.float32
BF16 = jnp.bfloat16
U32 = jnp.uint32

D_MODEL = 1024
DEPTH = 4
PAST_LEN = 8192
HEAD_DIM = 64
A_Q_HEADS = 16
A_KV_HEADS = 4
A_GROUP = 4
WINDOW = 128
R_HEADS = 8
R_DK = 64
R_DV = 128
R_CHUNK = 128
ROPE_BASE = 10000.0
C_CHUNK = 128
C_GROUPS = 8
N_EXPERTS = 32
TOP_K = 4
D_FF = D_MODEL
SWIGLU_LIMIT = 7.0
SWIGLU_ALPHA = 1.702
EPS = 1e-6

Z_AQ, Z_RV, Z_RG, Z_CU, Z_CV, Z_GA, Z_GB, Z_GC = 0, 1024, 2048, 3072, 4096, 5120, 6144, 7168
Z_RQ, Z_RK, Z_AK, Z_AV = 8192, 8704, 9216, 9472
D_IN = 9728
IN_CHUNKS = 2
_ORIG = dict(aq=0, ak=1024, av=1280, rq=1536, rk=2048, rv=2560, rg=3584, cu=4608, cv=5632, mg=6656)

ROW_TILE = 512
MOE_ROWS = 256
MOE_HALVES = 2
ROUTER_PAD = 128
PACK_W = D_MODEL // 4
SC_WINDOW = 128
HI_MASK = 0xFFFF0000
VMEM_LIMIT = 56 * 1024 * 1024
NEG_BIG = -1e30


def _cparams(sem):
    return pltpu.CompilerParams(dimension_semantics=sem, vmem_limit_bytes=VMEM_LIMIT)


def _split_bf16(x):
    hi = x.astype(BF16)
    lo = (x - hi.astype(F32)).astype(BF16)
    return hi, lo


def _dot(a, b):
    return jnp.dot(a, b, preferred_element_type=F32)


def _dot_nt(a, b):
    return lax.dot_general(a, b, (((1,), (1,)), ((), ())), preferred_element_type=F32)


def _dot_tn(a, b):
    return lax.dot_general(a, b, (((0,), (0,)), ((), ())), preferred_element_type=F32)


def _ada_kernel(c_ref, w_ref, b_ref, o_ref):
    c = c_ref[...]
    s_hi, s_lo = _split_bf16(c * jax.nn.sigmoid(c))
    w_hi, w_lo = _split_bf16(w_ref[0])
    acc = _dot(s_hi, w_hi) + _dot(s_lo, w_hi) + _dot(s_hi, w_lo)
    o_ref[0] = acc + b_ref[0]


def _ada_call(c_all, w_ada, b_ada):
    depth, d, n = w_ada.shape
    m = c_all.shape[0]
    tn = 1024
    return pl.pallas_call(
        _ada_kernel,
        grid=(depth, n // tn),
        in_specs=[
            pl.BlockSpec((m, d), lambda l, j: (0, 0)),
            pl.BlockSpec((1, d, tn), lambda l, j: (l, 0, j)),
            pl.BlockSpec((1, 1, tn), lambda l, j: (l, 0, j)),
        ],
        out_specs=pl.BlockSpec((1, m, tn), lambda l, j: (l, 0, j)),
        out_shape=jax.ShapeDtypeStruct((depth, m, n), F32),
        compiler_params=_cparams(("arbitrary", "arbitrary")),
        name="ada",
    )(c_all, w_ada, b_ada.reshape(depth, 1, n))


def _pack_rows(y):
    bits = pltpu.bitcast(y.astype(BF16).astype(F32), U32)
    q = [bits[:, i * PACK_W:(i + 1) * PACK_W] for i in range(4)]
    return (q[0] >> 16) | q[1], (q[2] >> 16) | q[3]


def _unpack_rows(a, b):
    f = lambda w: pltpu.bitcast(w, F32)
    return jnp.concatenate([f(a << 16), f(a & jnp.uint32(HI_MASK)), f(b << 16), f(b & jnp.uint32(HI_MASK))], axis=-1)


def _combine(y4_ref, route_ref):
    route = route_ref[...]
    acc = None
    for k in range(TOP_K):
        term = route[:, k:k + 1] * _unpack_rows(y4_ref[0, k], y4_ref[1, k])
        acc = term if acc is None else acc + term
    return acc


def _rms(x):
    return x * lax.rsqrt(jnp.mean(x * x, axis=-1, keepdims=True) + EPS)


def _inproj_kernel(*refs, has_resid):
    if has_resid:
        x_ref, y4_ref, route_ref, g2_ref, sh_ref, sc_ref, lng_ref, w_ref, z_ref, xo_ref, hb = refs
    else:
        x_ref, sh_ref, sc_ref, lng_ref, w_ref, z_ref, hb = refs
    c = pl.program_id(2)

    @pl.when(c == 0)
    def _():
        x = x_ref[...]
        if has_resid:
            x = x + g2_ref[...] * _combine(y4_ref, route_ref).reshape(x.shape)
            xo_ref[...] = x
        h = _rms(x) * lng_ref[...] * (1.0 + sc_ref[...]) + sh_ref[...]
        hb[...] = h.reshape(hb.shape).astype(BF16)

    h = hb[...]
    n = w_ref.shape[2]
    for c0 in range(0, n, 1024):
        c1 = min(c0 + 1024, n)
        z_ref[:, c0:c1] = _dot(h, w_ref[c, :, c0:c1]).astype(BF16)


def _group_blocks(g, r):
    if r >= ROW_TILE:
        return 1, ROW_TILE
    return ROW_TILE // r, r


def _moe_out_specs(tm, nj, tile_off):
    return [pl.BlockSpec((2, TOP_K, tm, PACK_W), lambda i, j, *_: (0, 0, tile_off + i * nj + j, 0)),
            pl.BlockSpec((tm, ROUTER_PAD), lambda i, j, *_: (tile_off + i * nj + j, 0))]


def _inproj_call(x, mod, ln_g, w_bf, moe_out=None):
    g, r, d = x.shape
    gb, rb = _group_blocks(g, r)
    nj = r // rb
    nc, _, cw = w_bf.shape
    grid = (g // gb, nj, nc)
    xspec = pl.BlockSpec((gb, rb, d), lambda i, j, c: (i, j, 0))
    mspec = lambda col: pl.BlockSpec((gb, 1, d), lambda i, j, c: (i, 0, col))
    in_specs, args = [xspec], [x]
    if moe_out is not None:
        y4, route, mod_prev, tile_off = moe_out
        in_specs += _moe_out_specs(gb * rb, nj, tile_off) + [mspec(5)]
        args += [y4, route, mod_prev]
    in_specs += [mspec(0), mspec(1), pl.BlockSpec((1, 1, d), lambda i, j, c: (0, 0, 0)),
                 pl.BlockSpec((nc, d, cw), lambda i, j, c: (0, 0, 0), pipeline_mode=pl.Buffered(1))]
    args += [mod, mod, ln_g.reshape(1, 1, d), w_bf]
    zspec = pl.BlockSpec((gb * rb, cw), lambda i, j, c: (i * nj + j, c))
    zshape = jax.ShapeDtypeStruct((g * r, nc * cw), BF16)
    if moe_out is not None:
        out_specs, out_shape = [zspec, xspec], [zshape, jax.ShapeDtypeStruct(x.shape, F32)]
    else:
        out_specs, out_shape = zspec, zshape
    return pl.pallas_call(
        functools.partial(_inproj_kernel, has_resid=moe_out is not None),
        grid=grid, in_specs=in_specs, out_specs=out_specs, out_shape=out_shape,
        scratch_shapes=[pltpu.VMEM((gb * rb, d), BF16)],
        compiler_params=_cparams(("arbitrary", "arbitrary", "arbitrary")),
        name="inproj",
    )(*args)


def _resid_kernel(x_ref, y4_ref, route_ref, g2_ref, o_ref):
    x = x_ref[...]
    o_ref[...] = x + g2_ref[...] * _combine(y4_ref, route_ref).reshape(x.shape)


def _resid_call(x, y4, route, mod, tile_off):
    g, r, d = x.shape
    gb, rb = _group_blocks(g, r)
    nj = r // rb
    xspec = pl.BlockSpec((gb, rb, d), lambda i, j: (i, j, 0))
    return pl.pallas_call(
        _resid_kernel,
        grid=(g // gb, nj),
        in_specs=[xspec] + _moe_out_specs(gb * rb, nj, tile_off)
        + [pl.BlockSpec((gb, 1, d), lambda i, j: (i, 0, 5))],
        out_specs=xspec,
        out_shape=jax.ShapeDtypeStruct(x.shape, F32),
        compiler_params=_cparams(("arbitrary", "arbitrary")),
        name="resid",
    )(x, y4, route, mod)


def _head_rms(x, g):
    return x * lax.rsqrt(jnp.mean(x * x, axis=-1, keepdims=True) + EPS) * g


def _q_col(kvh, g):
    p, kv_odd = divmod(kvh, 2)
    j, g_odd = divmod(g, 2)
    tile = p * 4 + (kv_odd ^ g_odd) * 2 + j
    return tile * 2 * HEAD_DIM + g_odd * HEAD_DIM


def _stack_q(q, qg, kvh, scale):
    parts = []
    for g in range(A_GROUP):
        c0 = _q_col(kvh, g)
        parts.append(_head_rms(q[:, c0:c0 + HEAD_DIM], qg) * scale)
    return jnp.concatenate(parts, axis=0).astype(BF16)


def _half_mats():
    r = lax.broadcasted_iota(jnp.int32, (2 * HEAD_DIM, 2 * HEAD_DIM), 0)
    c = lax.broadcasted_iota(jnp.int32, (2 * HEAD_DIM, 2 * HEAD_DIM), 1)
    seg = jnp.where(r // HEAD_DIM == c // HEAD_DIM, 1.0, 0.0).astype(BF16)
    swap = jnp.where((r + HEAD_DIM) % (2 * HEAD_DIM) == c, 1.0, 0.0).astype(BF16)
    return seg, swap


def _pair_rms(x, g2, seg):
    outs = []
    for t in range(x.shape[1] // (2 * HEAD_DIM)):
        xt = x[:, t * 2 * HEAD_DIM:(t + 1) * 2 * HEAD_DIM]
        hi, lo = _split_bf16(xt * xt)
        ss = _dot(hi, seg) + _dot(lo, seg)
        outs.append(xt * lax.rsqrt(ss * (1.0 / HEAD_DIM) + EPS) * g2)
    return jnp.concatenate(outs, axis=-1)


def _sink_col(sink_ref, kvh, rows):
    return jnp.concatenate(
        [jnp.full((rows, 1), sink_ref[kvh * A_GROUP + g], F32) for g in range(A_GROUP)], axis=0)


def _attn_prompt_kernel(sink_ref, q_ref, k_ref, v_ref, qg_ref, kg_ref, o_ref, nk_ref, nv_ref, kprev, vprev):
    n = pl.program_id(1)
    w = WINDOW

    @pl.when(n == 0)
    def _():
        kprev[...] = jnp.zeros_like(kprev)
        vprev[...] = jnp.zeros_like(vprev)

    seg, swap = _half_mats()
    tw = 2 * HEAD_DIM
    scale = HEAD_DIM ** -0.5
    v = v_ref[...]
    qn = (_pair_rms(q_ref[...].astype(F32), qg_ref[...], seg) * scale).astype(BF16)
    kn = _pair_rms(k_ref[...].astype(F32), kg_ref[...], seg)
    knb = kn.astype(BF16)
    kcat = jnp.concatenate([kprev[...], knb], axis=0)
    vcat = jnp.concatenate([vprev[...], v], axis=0)

    row = lax.broadcasted_iota(jnp.int32, (2 * w, 2 * w), 0)
    i = row % w
    j = lax.broadcasted_iota(jnp.int32, (2 * w, 2 * w), 1)
    lo = jnp.where(n == 0, w - 1, -1)
    mask = (j > i) & (j <= i + w) & (j > lo)
    sink_top = (j == 0) & (row < w)
    sink_bot = (j == 0) & (row >= w)
    lane_kv = lax.broadcasted_iota(jnp.int32, (2 * w, tw), 1)
    key_kv = lax.broadcasted_iota(jnp.int32, (2 * w, tw), 0)
    first_o = lax.broadcasted_iota(jnp.int32, (w, tw), 1) < HEAD_DIM
    scores, values = [], []
    for p in range(A_KV_HEADS // 2):
        kp = kcat[:, p * tw:(p + 1) * tw]
        vp = vcat[:, p * tw:(p + 1) * tw]
        kv_tiles = ((kp, vp), (_dot(kp, swap).astype(BF16), _dot(vp, swap).astype(BF16)))
        for variant, (kk, vv) in enumerate(kv_tiles):
            t0 = p * 4 + variant * 2
            qs = jnp.concatenate([qn[:, t0 * tw:(t0 + 1) * tw], qn[:, (t0 + 1) * tw:(t0 + 2) * tw]], axis=0)
            for half in range(2):
                keep = (lane_kv < HEAD_DIM) if half == 0 else (lane_kv >= HEAD_DIM)
                kh = jnp.where(keep, kk, jnp.zeros_like(kk))
                values.append(jnp.where(keep & (key_kv > 0), vv, jnp.zeros_like(vv)))
                kvh = 2 * p + (half ^ variant)
                s = jnp.where(mask, _dot_nt(qs, kh), NEG_BIG)
                s = jnp.where(sink_top, sink_ref[kvh * A_GROUP + half], s)
                scores.append(jnp.where(sink_bot, sink_ref[kvh * A_GROUP + 2 + half], s))
    s_all = jnp.concatenate(scores, axis=0)
    p_all = jnp.exp(s_all - jnp.max(s_all, axis=-1, keepdims=True))
    inv = 1.0 / jnp.sum(p_all, axis=-1, keepdims=True)
    p_all = p_all.astype(BF16)
    outs = [_dot(p_all[c * 2 * w:(c + 1) * 2 * w], values[c]) * inv[c * 2 * w:(c + 1) * 2 * w]
            for c in range(len(values))]
    for p in range(A_KV_HEADS // 2):
        o_a = outs[4 * p] + outs[4 * p + 1]
        o_b = outs[4 * p + 2] + outs[4 * p + 3]
        for jr in range(2):
            a = o_a[jr * w:(jr + 1) * w]
            b = o_b[jr * w:(jr + 1) * w]
            c_even = (2 * p * A_GROUP + 2 * jr) * HEAD_DIM
            c_odd = ((2 * p + 1) * A_GROUP + 2 * jr) * HEAD_DIM
            o_ref[:, c_even:c_even + tw] = jnp.where(first_o, a, b).astype(o_ref.dtype)
            o_ref[:, c_odd:c_odd + tw] = jnp.where(first_o, b, a).astype(o_ref.dtype)

    kprev[...] = knb
    vprev[...] = v

    @pl.when(n == pl.num_programs(1) - 1)
    def _():
        nk_ref[0] = kn
        nv_ref[0] = v.astype(F32)


def _attn_prompt_call(z, sinks, qg, kg, batch, seq):
    nb = seq // WINDOW
    kvw = A_KV_HEADS * HEAD_DIM
    row = lambda b, n, s: b * nb + n
    grid_spec = pltpu.PrefetchScalarGridSpec(
        num_scalar_prefetch=1,
        grid=(batch, nb),
        in_specs=[
            pl.BlockSpec((WINDOW, D_MODEL), lambda b, n, s: (row(b, n, s), Z_AQ // D_MODEL)),
            pl.BlockSpec((WINDOW, kvw), lambda b, n, s: (row(b, n, s), Z_AK // kvw)),
            pl.BlockSpec((WINDOW, kvw), lambda b, n, s: (row(b, n, s), Z_AV // kvw)),
            pl.BlockSpec((1, 2 * HEAD_DIM), lambda b, n, s: (0, 0)),
            pl.BlockSpec((1, 2 * HEAD_DIM), lambda b, n, s: (0, 0)),
        ],
        out_specs=[
            pl.BlockSpec((WINDOW, D_MODEL), lambda b, n, s: (row(b, n, s), 0)),
            pl.BlockSpec((1, WINDOW, kvw), lambda b, n, s: (b, 0, 0)),
            pl.BlockSpec((1, WINDOW, kvw), lambda b, n, s: (b, 0, 0)),
        ],
        scratch_shapes=[pltpu.VMEM((WINDOW, kvw), BF16), pltpu.VMEM((WINDOW, kvw), BF16)],
    )
    pair_gain = lambda g: jnp.tile(g, 2).reshape(1, 2 * HEAD_DIM)
    return pl.pallas_call(
        _attn_prompt_kernel,
        grid_spec=grid_spec,
        out_shape=[
            jax.ShapeDtypeStruct((batch * seq, D_MODEL), BF16),
            jax.ShapeDtypeStruct((batch, WINDOW, kvw), F32),
            jax.ShapeDtypeStruct((batch, WINDOW, kvw), F32),
        ],
        compiler_params=_cparams(("arbitrary", "arbitrary")),
        name="attn_prompt",
    )(sinks, z, z, z, pair_gain(qg), pair_gain(kg))


ATTN_S_SEQS = 8


def _attn_sample_kernel(sink_ref, q_ref, k_ref, v_ref, kc_ref, vc_ref, qg_ref, kg_ref, o_ref, nk_ref, nv_ref):
    _, sb, wb, kvw = kc_ref.shape
    l = q_ref.shape[0] // sb
    rows = sb * l
    q = q_ref[...].astype(F32)
    k = k_ref[...].astype(F32)
    v = v_ref[...]
    qg = qg_ref[...]
    kg = kg_ref[...]
    kn = jnp.concatenate(
        [_head_rms(k[:, h * HEAD_DIM:(h + 1) * HEAD_DIM], kg) for h in range(A_KV_HEADS)], axis=-1)
    nk_ref[...] = kn
    nv_ref[...] = v.astype(F32)
    knb = kn.astype(BF16)
    kc = kc_ref[0].reshape(sb * wb, kvw).astype(BF16)
    vc = vc_ref[0].reshape(sb * wb, kvw).astype(BF16)

    rq = lax.broadcasted_iota(jnp.int32, (A_GROUP * rows, sb * wb), 0) % rows
    cc = lax.broadcasted_iota(jnp.int32, (A_GROUP * rows, sb * wb), 1)
    mask_c = (rq // l == cc // wb) & (cc % wb > rq % l + (wb - WINDOW))
    rq2 = lax.broadcasted_iota(jnp.int32, (A_GROUP * rows, rows), 0) % rows
    cn = lax.broadcasted_iota(jnp.int32, (A_GROUP * rows, rows), 1)
    mask_n = (rq2 // l == cn // l) & (cn % l <= rq2 % l)
    scale = HEAD_DIM ** -0.5
    outs = []
    for h in range(A_KV_HEADS):
        hs = slice(h * HEAD_DIM, (h + 1) * HEAD_DIM)
        qs = _stack_q(q, qg, h, scale)
        s_c = jnp.where(mask_c, _dot_nt(qs, kc[:, hs]), NEG_BIG)
        s_n = jnp.where(mask_n, _dot_nt(qs, knb[:, hs]), NEG_BIG)
        sink = _sink_col(sink_ref, h, rows)
        m = jnp.maximum(jnp.maximum(jnp.max(s_c, axis=-1, keepdims=True),
                                    jnp.max(s_n, axis=-1, keepdims=True)), sink)
        p_c = jnp.exp(s_c - m)
        p_n = jnp.exp(s_n - m)
        denom = (jnp.sum(p_c, axis=-1, keepdims=True) + jnp.sum(p_n, axis=-1, keepdims=True)
                 + jnp.exp(sink - m))
        o = (_dot(p_c.astype(BF16), vc[:, hs]) + _dot(p_n.astype(BF16), v[:, hs])) / denom
        outs += [o[g * rows:(g + 1) * rows] for g in range(A_GROUP)]
    o_ref[...] = jnp.concatenate(outs, axis=-1).astype(o_ref.dtype)


def _attn_sample_call(z, kc, vc, sinks, qg, kg, batch, l, layer):
    kvw = A_KV_HEADS * HEAD_DIM
    sb = ATTN_S_SEQS
    rows = sb * l
    wb = kc.shape[1]
    grid_spec = pltpu.PrefetchScalarGridSpec(
        num_scalar_prefetch=1,
        grid=(batch // sb,),
        in_specs=[
            pl.BlockSpec((rows, D_MODEL), lambda i, s: (i, Z_AQ // D_MODEL)),
            pl.BlockSpec((rows, kvw), lambda i, s: (i, Z_AK // kvw)),
            pl.BlockSpec((rows, kvw), lambda i, s: (i, Z_AV // kvw)),
            pl.BlockSpec((1, sb, wb, kvw), lambda i, s: (layer, i, 0, 0)),
            pl.BlockSpec((1, sb, wb, kvw), lambda i, s: (layer, i, 0, 0)),
            pl.BlockSpec((1, HEAD_DIM), lambda i, s: (0, 0)),
            pl.BlockSpec((1, HEAD_DIM), lambda i, s: (0, 0)),
        ],
        out_specs=[
            pl.BlockSpec((rows, D_MODEL), lambda i, s: (i, 0)),
            pl.BlockSpec((rows, kvw), lambda i, s: (i, 0)),
            pl.BlockSpec((rows, kvw), lambda i, s: (i, 0)),
        ],
    )
    return pl.pallas_call(
        _attn_sample_kernel,
        grid_spec=grid_spec,
        out_shape=[
            jax.ShapeDtypeStruct((batch * l, D_MODEL), BF16),
            jax.ShapeDtypeStruct((batch * l, kvw), F32),
            jax.ShapeDtypeStruct((batch * l, kvw), F32),
        ],
        compiler_params=_cparams(("arbitrary",)),
        name="attn_sample",
    )(sinks, z, z, z, kc, vc, qg.reshape(1, HEAD_DIM), kg.reshape(1, HEAD_DIM))


def _ret_tables(chunk, pos0, length):
    h = np.arange(R_HEADS, dtype=np.float64)
    log_gamma = np.log1p(-np.exp2(-5.0 - h))
    idx = np.arange(chunk, dtype=np.float64)
    diff = idx[:, None] - idx[None, :]
    intra = np.where(diff[None] >= 0, np.exp(np.maximum(diff, 0.0)[None] * log_gamma[:, None, None]), 0.0)
    q_decay = np.exp((idx + 1.0)[:, None] * log_gamma[None, :])
    k_decay = np.exp((chunk - 1.0 - idx)[:, None] * log_gamma[None, :])
    c_decay = np.exp(chunk * log_gamma)
    qd = np.repeat(q_decay, R_DK, axis=1)
    kd = np.repeat(k_decay, R_DK, axis=1)
    inv_freq = ROPE_BASE ** (-np.arange(0, R_DK, 2, dtype=np.float64) / R_DK)
    ang = (pos0 + np.arange(length, dtype=np.float64))[:, None] * inv_freq[None, :]
    cos = np.tile(np.concatenate([np.cos(ang), np.cos(ang)], axis=1), (1, R_HEADS))
    sin = np.tile(np.concatenate([-np.sin(ang), np.sin(ang)], axis=1), (1, R_HEADS))
    f = lambda a: jnp.asarray(a, F32)
    return f(intra), f(qd), f(kd), [float(c) for c in c_decay], f(cos), f(sin)


def _rope(x, cos, sin):
    n = x.shape[-1]
    half = R_DK // 2
    lane = lax.broadcasted_iota(jnp.int32, x.shape, 1)
    up = pltpu.roll(x, n - half, axis=1)
    dn = pltpu.roll(x, half, axis=1)
    partner = jnp.where(lane % R_DK < half, up, dn)
    return x * cos + partner * sin


def _rope_mxu(x, cos, sin):
    tw = 2 * R_DK
    half = R_DK // 2
    r = lax.broadcasted_iota(jnp.int32, (tw, tw), 0)
    c = lax.broadcasted_iota(jnp.int32, (tw, tw), 1)
    perm = jnp.where((r // R_DK == c // R_DK) & ((r + half) % R_DK == c % R_DK), 1.0, 0.0).astype(BF16)
    parts = []
    for t in range(x.shape[1] // tw):
        hi, lo = _split_bf16(x[:, t * tw:(t + 1) * tw])
        parts.append(_dot(hi, perm) + _dot(lo, perm))
    return x * cos + jnp.concatenate(parts, axis=-1) * sin


def _head_ln(o, g):
    mu = jnp.mean(o, axis=-1, keepdims=True)
    oc = o - mu
    var = jnp.mean(oc * oc, axis=-1, keepdims=True)
    return oc * lax.rsqrt(var + EPS) * g


def _silu(x):
    return x * jax.nn.sigmoid(x)


def _ret_prompt_kernel(q_ref, k_ref, v_ref, g_ref, cos_ref, sin_ref, intra_ref, qd_ref, kd_ref, ng_ref,
                       o_ref, s_ref, state, *, c_decay):
    n = pl.program_id(1)

    @pl.when(n == 0)
    def _():
        state[...] = jnp.zeros_like(state)

    cos = cos_ref[...]
    sin = sin_ref[...]
    q = _rope_mxu(q_ref[...].astype(F32), cos, sin)
    k = _rope_mxu(k_ref[...].astype(F32), cos, sin) * (R_DK ** -0.5)
    qb = q.astype(BF16)
    kb = k.astype(BF16)
    qdb = (q * qd_ref[...]).astype(BF16)
    kdb = (k * kd_ref[...]).astype(BF16)
    c = q.shape[0]
    ks = [slice(h * R_DK, (h + 1) * R_DK) for h in range(R_HEADS)]
    vs = [slice(h * R_DV, (h + 1) * R_DV) for h in range(R_HEADS)]
    a = jnp.concatenate([_dot_nt(qb[:, ks[h]], kb[:, ks[h]]) for h in range(R_HEADS)], axis=0)
    ab = (a * intra_ref[...].reshape(R_HEADS * c, c)).astype(BF16)
    outs = []
    for h in range(R_HEADS):
        vh = v_ref[:, vs[h]]
        s_old = state[h]
        outs.append(_dot(ab[h * c:(h + 1) * c], vh) + _dot(qdb[:, ks[h]], s_old.astype(BF16)))
        state[h] = s_old * c_decay[h] + _dot_tn(kdb[:, ks[h]], vh)
    y = _head_ln(jnp.concatenate(outs, axis=0), 1.0)
    gate = _silu(g_ref[...].astype(F32)) * ng_ref[...]
    for h in range(R_HEADS):
        o_ref[:, vs[h]] = (y[h * c:(h + 1) * c] * gate[:, vs[h]]).astype(o_ref.dtype)

    @pl.when(n == pl.num_programs(1) - 1)
    def _():
        s_ref[0] = state[...]


def _ret_prompt_call(z, ret_norm_g, batch, seq):
    c = R_CHUNK
    nc = seq // c
    intra, qd, kd, c_decay, cos, sin = _ret_tables(c, 0, seq)
    qkw = R_HEADS * R_DK
    row = lambda b, n: b * nc + n
    const2 = lambda b, n: (0, 0)
    return pl.pallas_call(
        functools.partial(_ret_prompt_kernel, c_decay=c_decay),
        grid=(batch, nc),
        in_specs=[
            pl.BlockSpec((c, qkw), lambda b, n: (row(b, n), Z_RQ // qkw)),
            pl.BlockSpec((c, qkw), lambda b, n: (row(b, n), Z_RK // qkw)),
            pl.BlockSpec((c, D_MODEL), lambda b, n: (row(b, n), Z_RV // D_MODEL)),
            pl.BlockSpec((c, D_MODEL), lambda b, n: (row(b, n), Z_RG // D_MODEL)),
            pl.BlockSpec((c, qkw), lambda b, n: (n, 0)),
            pl.BlockSpec((c, qkw), lambda b, n: (n, 0)),
            pl.BlockSpec((R_HEADS, c, c), lambda b, n: (0, 0, 0)),
            pl.BlockSpec((c, qkw), const2),
            pl.BlockSpec((c, qkw), const2),
            pl.BlockSpec((1, D_MODEL), const2),
        ],
        out_specs=[
            pl.BlockSpec((c, D_MODEL), lambda b, n: (row(b, n), 0)),
            pl.BlockSpec((1, R_HEADS, R_DK, R_DV), lambda b, n: (b, 0, 0, 0)),
        ],
        out_shape=[
            jax.ShapeDtypeStruct((batch * seq, D_MODEL), BF16),
            jax.ShapeDtypeStruct((batch, R_HEADS, R_DK, R_DV), F32),
        ],
        scratch_shapes=[pltpu.VMEM((R_HEADS, R_DK, R_DV), F32)],
        compiler_params=_cparams(("arbitrary", "arbitrary")),
        name="ret_prompt",
    )(z, z, z, z, cos, sin, intra, qd, kd, ret_norm_g.reshape(1, D_MODEL))


RET_S_SEQS = 8


def _ret_sample_kernel(q_ref, k_ref, v_ref, g_ref, s0_ref, cos_ref, sin_ref, intra_ref, qd_ref, kd_ref, ng_ref,
                       *rest, c_decay, l):
    o_ref, s_ref = rest[-2:]
    sb = s0_ref.shape[1]
    rows = sb * l
    cos = cos_ref[...]
    sin = sin_ref[...]
    q = _rope(q_ref[...].astype(F32), cos, sin)
    k = _rope(k_ref[...].astype(F32), cos, sin) * (R_DK ** -0.5)
    qb = q.astype(BF16)
    kb = k.astype(BF16)
    qdb = (q * qd_ref[...]).astype(BF16)
    kdb = (k * kd_ref[...]).astype(BF16)
    for h in range(R_HEADS):
        ks = slice(h * R_DK, (h + 1) * R_DK)
        vs = slice(h * R_DV, (h + 1) * R_DV)
        vh = v_ref[:, vs]
        a = _dot_nt(qb[:, ks], kb[:, ks]) * intra_ref[h]
        o = _dot(a.astype(BF16), vh)
        cross, new_s = [], []
        for b in range(sb):
            rs = slice(b * l, (b + 1) * l)
            s_old = s0_ref[0, b, h]
            cross.append(_dot(qdb[rs, ks], s_old.astype(BF16)))
            s_ref[0, b, h] = s_old * c_decay[h] + _dot_tn(kdb[rs, ks], vh[rs])
        o = o + jnp.concatenate(cross, axis=0)
        y = _head_ln(o, ng_ref[:, vs]) * _silu(g_ref[:, vs].astype(F32))
        o_ref[:, vs] = y.astype(o_ref.dtype)


def _ret_sample_call(z, s0, ret_norm_g, batch, l, layer, stack):
    c = math.gcd(l, R_CHUNK)
    assert c == l, "sample step expects a single retention chunk"
    sb = RET_S_SEQS
    rows = sb * l
    intra, qd, kd, c_decay, cos, sin = _ret_tables(c, PAST_LEN, l)
    eye = jnp.eye(sb, dtype=F32)
    intra_bd = jnp.einsum("ab,hij->haibj", eye, intra).reshape(R_HEADS, rows, rows)
    tile = lambda t: jnp.tile(t, (sb, 1))
    qkw = R_HEADS * R_DK
    const2 = lambda i: (0, 0)
    return pl.pallas_call(
        functools.partial(_ret_sample_kernel, c_decay=c_decay, l=l),
        grid=(batch // sb,),
        in_specs=[
            pl.BlockSpec((rows, qkw), lambda i: (i, Z_RQ // qkw)),
            pl.BlockSpec((rows, qkw), lambda i: (i, Z_RK // qkw)),
            pl.BlockSpec((rows, D_MODEL), lambda i: (i, Z_RV // D_MODEL)),
            pl.BlockSpec((rows, D_MODEL), lambda i: (i, Z_RG // D_MODEL)),
            pl.BlockSpec((1, sb, R_HEADS, R_DK, R_DV), lambda i: (layer, i, 0, 0, 0)),
            pl.BlockSpec((rows, qkw), const2),
            pl.BlockSpec((rows, qkw), const2),
            pl.BlockSpec((R_HEADS, rows, rows), lambda i: (0, 0, 0)),
            pl.BlockSpec((rows, qkw), const2),
            pl.BlockSpec((rows, qkw), const2),
            pl.BlockSpec((1, D_MODEL), const2),
            pl.BlockSpec(memory_space=pl.ANY),
        ],
        out_specs=[
            pl.BlockSpec((rows, D_MODEL), lambda i: (i, 0)),
            pl.BlockSpec((1, sb, R_HEADS, R_DK, R_DV), lambda i: (layer, i, 0, 0, 0)),
        ],
        out_shape=[
            jax.ShapeDtypeStruct((batch * l, D_MODEL), BF16),
            jax.ShapeDtypeStruct(s0.shape, F32),
        ],
        input_output_aliases={11: 1},
        compiler_params=_cparams(("arbitrary",)),
        name="ret_sample",
    )(z, z, z, z, s0, tile(cos), tile(sin), intra_bd, tile(qd), tile(kd), ret_norm_g.reshape(1, D_MODEL), stack)


def _gelu(x):
    return jax.nn.gelu(x, approximate=True)


def _route_rows(logits, carry):
    tm = logits.shape[0]
    lane = lax.broadcasted_iota(jnp.int32, logits.shape, 1).astype(F32)
    work = logits
    sel = jnp.zeros(logits.shape, F32)
    vals, idxs = [], []
    for _ in range(TOP_K):
        m = jnp.max(work, axis=-1, keepdims=True)
        idx = jnp.min(jnp.where(work == m, lane, float(ROUTER_PAD)), axis=-1, keepdims=True)
        hit = lane == idx
        vals.append(m)
        idxs.append(idx)
        sel = jnp.where(hit, 1.0, sel)
        work = jnp.where(hit, -3e38, work)
    ex = [jnp.exp(v - vals[0]) for v in vals]
    den = ex[0] + ex[1] + ex[2] + ex[3]
    r = lax.broadcasted_iota(jnp.int32, (tm, tm), 0)
    c = lax.broadcasted_iota(jnp.int32, (tm, tm), 1)
    before = jnp.where(c < r, 1.0, 0.0).astype(BF16)
    rank = _dot(before, sel.astype(BF16)) + carry
    route = jnp.zeros(logits.shape, F32)
    for k in range(TOP_K):
        route = jnp.where(lane == float(k), ex[k] / den, route)
        route = jnp.where(lane == float(TOP_K + k), idxs[k], route)
        rk = jnp.sum(jnp.where(lane == idxs[k], rank, 0.0), axis=-1, keepdims=True)
        route = jnp.where(lane == float(2 * TOP_K + k), rk, route)
    return route, carry + jnp.sum(sel, axis=0, keepdims=True)


def _merge_kernel(*refs, emit_cv):
    (x_ref, oa_ref, ob_ref, cu_ref, cv_ref, ga_ref, gb_ref, gc_ref, g1_ref, sh2_ref, sc2_ref,
     bg_ref, lnf_ref, gmg_ref, gmb_ref, mix_ref, mixb_ref, wout_ref, rwh_ref, rwl_ref, rb_ref) = refs[:21]
    cnt_ref = refs[21]
    n_out = 5 if emit_cv else 4
    outs = refs[-(n_out + 1):]
    if emit_cv:
        xo_ref, h2_ref, route_ref, cnto_ref, cvo_ref, carry = outs
    else:
        xo_ref, h2_ref, route_ref, cnto_ref, carry = outs
    first = (pl.program_id(0) == 0) & (pl.program_id(1) == 0)

    @pl.when(first)
    def _():
        carry[...] = cnt_ref[...]

    x = x_ref[...]
    gbk, rb, d = x.shape
    tm = gbk * rb
    cw = C_CHUNK
    gw = d // C_GROUPS

    cv = _gelu(cv_ref[...].astype(F32))
    mu = jnp.mean(cv, axis=-1, keepdims=True)
    cvc = cv - mu
    var = jnp.mean(cvc * cvc, axis=-1, keepdims=True)
    cv = cvc * lax.rsqrt(var + EPS) * gmg_ref[...] + gmb_ref[...]
    if emit_cv:
        cvo_ref[...] = cv
    cvb = cv.astype(BF16)
    mixed_rows = []
    for c in range(tm // cw):
        rs = slice(c * cw, (c + 1) * cw)
        cols = [_dot(mix_ref[g], cvb[rs, g * gw:(g + 1) * gw]) for g in range(C_GROUPS)]
        mixed_rows.append(jnp.concatenate(cols, axis=-1) + mixb_ref[...])
    mixed = jnp.concatenate(mixed_rows, axis=0)
    oc = _gelu(cu_ref[...].astype(F32)) * mixed

    bg = bg_ref[...]
    ga = jax.nn.sigmoid(ga_ref[...].astype(F32) + bg[0:1])
    gb = jax.nn.sigmoid(gb_ref[...].astype(F32) + bg[1:2])
    gc = jax.nn.sigmoid(gc_ref[...].astype(F32) + bg[2:3])
    merged = ga * oa_ref[...].astype(F32) + gb * ob_ref[...].astype(F32) + gc * oc
    y = _dot(merged.astype(BF16), wout_ref[...])
    x = x + g1_ref[...] * y.reshape(gbk, rb, d)
    xo_ref[...] = x

    h2 = (_rms(x) * lnf_ref[...] * (1.0 + sc2_ref[...]) + sh2_ref[...]).reshape(tm, d)
    h2b = h2.astype(BF16)
    h2_ref[0], h2_ref[1] = _pack_rows(h2)
    h2l = (h2 - h2b.astype(F32)).astype(BF16)
    rwh = rwh_ref[...]
    logits = _dot(h2b, rwh) + _dot(h2l, rwh) + _dot(h2b, rwl_ref[...]) + rb_ref[...]
    route, counts = _route_rows(logits, carry[...])
    route_ref[...] = route
    carry[...] = counts
    cnto_ref[...] = counts


def _merge_call(x, z, oa, ob, mod, b_gate, ln_ffn_g, gm_ln_g, gm_ln_b, mix, mixb, w_out_bf, rw_hi, rw_lo, rb,
                counts, t_all, tile_off, shared, emit_cv):
    g, r, d = x.shape
    gb, rb_ = _group_blocks(g, r)
    tm = gb * rb_
    nj = r // rb_
    t = g * r
    xspec = pl.BlockSpec((gb, rb_, d), lambda i, j: (i, j, 0))
    rows = lambda col: pl.BlockSpec((tm, d), lambda i, j: (i * nj + j, col))
    mspec = lambda col: pl.BlockSpec((gb, 1, d), lambda i, j: (i, 0, col))
    const = lambda shape: pl.BlockSpec(shape, lambda i, j: (0,) * len(shape))
    in_specs = [
        xspec, rows(0), rows(0),
        rows(Z_CU // d), rows(Z_CV // d), rows(Z_GA // d), rows(Z_GB // d), rows(Z_GC // d),
        mspec(2), mspec(3), mspec(4),
        const((3, d)), const((1, 1, d)), const((1, d)), const((1, d)),
        const((C_GROUPS, C_CHUNK, C_CHUNK)), const((C_CHUNK, d)),
        const((d, d)), const((d, ROUTER_PAD)), const((d, ROUTER_PAD)), const((1, ROUTER_PAD)),
        const((1, ROUTER_PAD)),
    ]
    aliases = {len(in_specs): 1, len(in_specs) + 1: 2}
    in_specs += [pl.BlockSpec(memory_space=pl.ANY), pl.BlockSpec(memory_space=pl.ANY)]
    out_specs = [xspec, pl.BlockSpec((2, tm, PACK_W), lambda i, j: (0, tile_off + i * nj + j, 0)),
                 pl.BlockSpec((tm, ROUTER_PAD), lambda i, j: (tile_off + i * nj + j, 0)), const((1, ROUTER_PAD))]
    out_shape = [jax.ShapeDtypeStruct(x.shape, F32), jax.ShapeDtypeStruct((2, t_all, PACK_W), U32),
                 jax.ShapeDtypeStruct((t_all, ROUTER_PAD), F32), jax.ShapeDtypeStruct((1, ROUTER_PAD), F32)]
    if emit_cv:
        out_specs.append(rows(0))
        out_shape.append(jax.ShapeDtypeStruct((t, d), F32))
    return pl.pallas_call(
        functools.partial(_merge_kernel, emit_cv=emit_cv),
        grid=(g // gb, nj), in_specs=in_specs, out_specs=out_specs, out_shape=out_shape,
        scratch_shapes=[pltpu.VMEM((1, ROUTER_PAD), F32)],
        input_output_aliases=aliases,
        compiler_params=_cparams(("arbitrary", "arbitrary")),
        name="merge",
    )(x, oa, ob, z, z, z, z, z, mod, mod, mod,
      b_gate.reshape(3, d), ln_ffn_g.reshape(1, 1, d), gm_ln_g.reshape(1, d), gm_ln_b.reshape(1, d),
      mix, mixb, w_out_bf, rw_hi, rw_lo, rb, counts, *shared)


def _moe_kernel(be_ref, nx_ref, nh_ref, nu_ref, x_ref, w1_hbm, b1_ref, w2_hbm, b2_ref, o_ref,
                w1s, w2s, w1b, w2b, sem, *, layer):
    i = pl.program_id(0)
    e = be_ref[i]
    prev = be_ref[jnp.maximum(i - 1, 0)]

    def weight_copies(ex):
        return (pltpu.make_async_copy(w1_hbm.at[layer, ex], w1s, sem.at[0]),
                pltpu.make_async_copy(w2_hbm.at[layer, ex], w2s, sem.at[1]))

    @pl.when(i == 0)
    def _():
        for cp in weight_copies(e):
            cp.start()

    @pl.when((i == 0) | (e != prev))
    def _():
        for cp in weight_copies(e):
            cp.wait()
        w1b[...] = w1s[...].astype(BF16)
        w2b[...] = w2s[...].astype(BF16)
        nxt = nx_ref[i]

        @pl.when(nxt >= 0)
        def _():
            for cp in weight_copies(nxt):
                cp.start()

    def experts(rows):
        rs = slice(0, rows)
        xb = _unpack_rows(x_ref[0, rs], x_ref[1, rs]).astype(BF16)
        hdn = _dot(xb, w1b[...]) + b1_ref[0, 0]
        g = jnp.minimum(hdn[:, :D_FF], SWIGLU_LIMIT)
        up = jnp.clip(hdn[:, D_FF:], -SWIGLU_LIMIT, SWIGLU_LIMIT)
        act = (up + 1.0) * (g * jax.nn.sigmoid(g * SWIGLU_ALPHA))
        o_ref[0, rs], o_ref[1, rs] = _pack_rows(_dot(act.astype(BF16), w2b[...]) + b2_ref[0, 0])

    for nh in range(1, MOE_HALVES + 1):
        pl.when(nh_ref[i] == nh)(functools.partial(experts, nh * MOE_ROWS))


def _moe_call(xb, block_e, next_e, n_halves, n_used, w1, b1, w2, b2, layer):
    _, n_rows, _ = xb.shape
    step_rows = MOE_HALVES * MOE_ROWS
    nblk = n_rows // step_rows
    depth, ne, d, f2 = w1.shape
    last = lambda i, be, nx, nh, nu: jnp.minimum(i, nu[0] - 1)
    xspec = pl.BlockSpec((2, step_rows, PACK_W), lambda i, be, nx, nh, nu: (0, last(i, be, nx, nh, nu), 0))
    grid_spec = pltpu.PrefetchScalarGridSpec(
        num_scalar_prefetch=4,
        grid=(nblk,),
        in_specs=[
            xspec,
            pl.BlockSpec(memory_space=pl.ANY),
            pl.BlockSpec((1, 1, 1, f2), lambda i, be, nx, nh, nu: (layer, be[i], 0, 0)),
            pl.BlockSpec(memory_space=pl.ANY),
            pl.BlockSpec((1, 1, 1, d), lambda i, be, nx, nh, nu: (layer, be[i], 0, 0)),
        ],
        out_specs=xspec,
        scratch_shapes=[pltpu.VMEM((d, f2), F32), pltpu.VMEM((f2 // 2, d), F32),
                        pltpu.VMEM((d, f2), BF16), pltpu.VMEM((f2 // 2, d), BF16),
                        pltpu.SemaphoreType.DMA((2,))],
    )
    return pl.pallas_call(
        functools.partial(_moe_kernel, layer=layer),
        grid_spec=grid_spec,
        out_shape=jax.ShapeDtypeStruct(xb.shape, U32),
        compiler_params=_cparams(("arbitrary",)),
        name="moe",
    )(block_e, next_e, n_halves, n_used, xb, w1, b1.reshape(depth, ne, 1, f2), w2, b2.reshape(depth, ne, 1, d))


def _sc_mesh():
    return plsc.VectorSubcoreMesh(core_axis_name="core", subcore_axis_name="subcore")


def _sc_scatter_rows(x, idx, n_out):
    t, c = x.shape
    kk = idx.shape[0]

    @pl.kernel(out_type=jax.ShapeDtypeStruct((n_out, c), x.dtype), mesh=_sc_mesh(), scratch_types=[])
    def scatter(x_hbm, i_hbm, o_hbm):
        def body(x_vmem, i_vmem):
            for k in range(kk):
                pltpu.sync_copy(x_vmem, o_hbm.at[i_vmem.at[k]])

        pltpu.emit_pipeline(
            body,
            grid=(t // SC_WINDOW,),
            in_specs=[pl.BlockSpec((SC_WINDOW, c), index_map=lambda i: (i, 0)),
                      pl.BlockSpec((kk, SC_WINDOW), index_map=lambda i: (0, i))],
            out_specs=[],
            core_axis_name=("core", "subcore"),
            dimension_semantics=(pltpu.PARALLEL,),
        )(x_hbm, i_hbm)

    return scatter(x, idx)


def _sc_gather_rows(data, idx):
    n = idx.shape[0]
    c = data.shape[1]

    @pl.kernel(out_type=jax.ShapeDtypeStruct((n, c), data.dtype), mesh=_sc_mesh(), scratch_types=[])
    def gather(x_hbm, i_hbm, o_hbm):
        def body(i_vmem, o_vmem):
            pltpu.sync_copy(x_hbm.at[i_vmem.at[0]], o_vmem)

        pltpu.emit_pipeline(
            body,
            grid=(n // SC_WINDOW,),
            in_specs=[pl.BlockSpec((1, SC_WINDOW), index_map=lambda i: (0, i))],
            out_specs=[pl.BlockSpec((SC_WINDOW, c), index_map=lambda i: (i, 0))],
            core_axis_name=("core", "subcore"),
            dimension_semantics=(pltpu.PARALLEL,),
        )(i_hbm, o_hbm)

    return gather(data, idx.reshape(1, n))


def _plan(route, counts):
    t = route.shape[0]
    step_rows = MOE_HALVES * MOE_ROWS
    n_steps = -(-t * TOP_K // step_rows) + N_EXPERTS
    e4 = route[:, TOP_K:2 * TOP_K].astype(jnp.int32)
    r4 = route[:, 2 * TOP_K:3 * TOP_K].astype(jnp.int32)
    cnt = counts[0, :N_EXPERTS].astype(jnp.int32)
    halves = (cnt + MOE_ROWS - 1) // MOE_ROWS
    steps = (halves + MOE_HALVES - 1) // MOE_HALVES
    send = jnp.cumsum(steps)
    sstart = send - steps
    onehot = e4[:, :, None] == jnp.arange(N_EXPERTS, dtype=jnp.int32)[None, None, :]
    dest = (r4 + jnp.sum(jnp.where(onehot, (sstart * step_rows)[None, None, :], 0), axis=-1)).T
    step = jnp.arange(n_steps, dtype=jnp.int32)
    step_e = jnp.sum((step[:, None] >= send[None, :]).astype(jnp.int32), axis=-1)
    n_used = send[-1].astype(jnp.int32).reshape(1)
    step_e = jnp.minimum(step_e, jnp.sum((send < send[-1]).astype(jnp.int32))).astype(jnp.int32)
    mine = step_e[:, None] == jnp.arange(N_EXPERTS, dtype=jnp.int32)[None, :]
    pick = lambda table: jnp.sum(jnp.where(mine, table[None, :], 0), axis=-1)
    n_halves = jnp.clip(pick(halves) - MOE_HALVES * (step - pick(sstart)), 0, MOE_HALVES)
    n_halves = jnp.where(step < n_used[0], n_halves, 0).astype(jnp.int32)
    after = pick(send)
    next_e = jnp.sum((after[:, None] >= send[None, :]).astype(jnp.int32), axis=-1)
    next_e = jnp.where(after < n_used[0], next_e, -1).astype(jnp.int32)
    return dest, step_e, next_e, n_halves, n_used, n_steps * step_rows


def _moe(h2p, route, counts, w1, b1, w2, b2, layer):
    _, t, pw = h2p.shape
    dest, step_e, next_e, n_halves, n_used, n_rows = _plan(route, counts)
    idx_s = jnp.concatenate([dest, dest + n_rows], axis=1)
    xb = _sc_scatter_rows(h2p.reshape(2 * t, pw), idx_s, 2 * n_rows).reshape(2, n_rows, pw)
    yb = _moe_call(xb, step_e, next_e, n_halves, n_used, w1, b1, w2, b2, layer)
    idx_g = jnp.concatenate([dest.reshape(-1), dest.reshape(-1) + n_rows])
    y4 = _sc_gather_rows(yb.reshape(2 * n_rows, pw), idx_g)
    return y4.reshape(2, TOP_K, t, pw)


def _reorder_w_in(w):
    o = _ORIG
    seg = lambda a, n: w[:, a:a + n]
    d = D_MODEL
    q_heads = sorted(((_q_col(kv, g), kv * A_GROUP + g) for kv in range(A_KV_HEADS) for g in range(A_GROUP)))
    aq = jnp.concatenate([seg(o["aq"] + h * HEAD_DIM, HEAD_DIM) for _, h in q_heads], axis=1)
    parts = [aq, seg(o["rv"], d), seg(o["rg"], d), seg(o["cu"], d), seg(o["cv"], d),
             seg(o["mg"], d), seg(o["mg"] + d, d), seg(o["mg"] + 2 * d, d),
             seg(o["rq"], 512), seg(o["rk"], 512), seg(o["ak"], 256), seg(o["av"], 256)]
    w = jnp.concatenate(parts, axis=1).astype(BF16)
    return w.reshape(D_MODEL, IN_CHUNKS, D_IN // IN_CHUNKS).transpose(1, 0, 2)


def _gmlp_tables(ws, bs, chunk_len, rows):
    causal = jnp.tril(jnp.ones((C_CHUNK, C_CHUNK), dtype=bool))
    w = jnp.where(causal[None], ws, 0.0)[:, :chunk_len, :chunk_len]
    reps = rows // chunk_len
    eye = jnp.eye(reps, dtype=F32)
    mix = jnp.einsum("ab,gts->gatbs", eye, w).reshape(C_GROUPS, rows, rows).astype(BF16)
    b = jnp.tile(bs[:, :chunk_len].T, (reps, 1))
    mixb = jnp.repeat(b, D_MODEL // C_GROUPS, axis=1)
    return mix, mixb


def kernel(x_prompt, x_sample, c_prompt, c_sample, cache_attn_k, cache_attn_v, state_retention, ln_mix_g, ln_ffn_g, w_ada, b_ada, w_in, b_gate, q_norm_g, k_norm_g, attn_sinks, ret_norm_g, gm_ln_g, gm_ln_b, gm_ws, gm_bs, w_out, router_w, router_b, moe_w1, moe_b1, moe_w2, moe_b2):
    bp, lp, d = x_prompt.shape
    bs, ls, _ = x_sample.shape
    tp, ts = bp * lp, bs * ls
    kvw = A_KV_HEADS * HEAD_DIM
    wb = cache_attn_k.shape[2]

    mod_all = _ada_call(jnp.concatenate([c_prompt, c_sample], axis=0), w_ada, b_ada)

    cache_k = cache_attn_k.reshape(DEPTH, bs, wb, kvw)
    cache_v = cache_attn_v.reshape(DEPTH, bs, wb, kvw)
    xp, xs = x_prompt, x_sample
    y4 = None
    mod_p = mod_s = None
    s_all = jnp.zeros(state_retention.shape, F32)
    h2 = jnp.zeros((2, tp + ts, PACK_W), U32)
    route = jnp.zeros((tp + ts, ROUTER_PAD), F32)
    pk, pv, ps, sk, sv, sg = [], [], [], [], [], []
    for l in range(DEPTH):
        prev_mod_p, prev_mod_s = mod_p, mod_s
        mod_p = mod_all[l, :bp].reshape(bp, 1, 6 * d)
        mod_s = mod_all[l, bp:].reshape(bs, 1, 6 * d)
        w_in_bf = _reorder_w_in(w_in[l])
        w_out_bf = w_out[l].astype(BF16)
        rw = jnp.pad(router_w[l], ((0, 0), (0, ROUTER_PAD - N_EXPERTS)))
        rw_hi, rw_lo = _split_bf16(rw)
        rb = jnp.pad(router_b[l], (0, ROUTER_PAD - N_EXPERTS), constant_values=NEG_BIG).reshape(1, ROUTER_PAD)

        if l == 0:
            zp = _inproj_call(xp, mod_p, ln_mix_g[l], w_in_bf)
            zs = _inproj_call(xs, mod_s, ln_mix_g[l], w_in_bf)
        else:
            zp, xp = _inproj_call(xp, mod_p, ln_mix_g[l], w_in_bf, moe_out=(y4, route, prev_mod_p, 0))
            zs, xs = _inproj_call(xs, mod_s, ln_mix_g[l], w_in_bf, moe_out=(y4, route, prev_mod_s, tp // ROW_TILE))
        oa_p, k_p, v_p = _attn_prompt_call(zp, attn_sinks[l], q_norm_g[l], k_norm_g[l], bp, lp)
        oa_s, k_s, v_s = _attn_sample_call(zs, cache_k, cache_v, attn_sinks[l], q_norm_g[l], k_norm_g[l], bs, ls, l)
        ob_p, s_p = _ret_prompt_call(zp, ret_norm_g[l], bp, lp)
        ob_s, s_all = _ret_sample_call(zs, state_retention, ret_norm_g[l], bs, ls, l, stack=s_all)

        mix_p, mixb_p = _gmlp_tables(gm_ws[l], gm_bs[l], C_CHUNK, C_CHUNK)
        mix_s, mixb_s = _gmlp_tables(gm_ws[l], gm_bs[l], ls, C_CHUNK)
        common = (b_gate[l], ln_ffn_g[l], gm_ln_g[l], gm_ln_b[l])
        tail = (w_out_bf, rw_hi, rw_lo, rb)
        zero_counts = jnp.zeros((1, ROUTER_PAD), F32)
        xp, h2, route, cnt_p = _merge_call(xp, zp, oa_p, ob_p, mod_p, *common, mix_p, mixb_p, *tail,
                                           zero_counts, tp + ts, 0, (h2, route), emit_cv=False)
        xs, h2, route, cnt, cv_s = _merge_call(xs, zs, oa_s, ob_s, mod_s, *common, mix_s, mixb_s, *tail,
                                               cnt_p, tp + ts, tp // ROW_TILE, (h2, route), emit_cv=True)
        y4 = _moe(h2, route, cnt, moe_w1, moe_b1, moe_w2, moe_b2, l)

        pk.append(k_p.reshape(bp, WINDOW, A_KV_HEADS, HEAD_DIM))
        pv.append(v_p.reshape(bp, WINDOW, A_KV_HEADS, HEAD_DIM))
        ps.append(s_p)
        sk.append(k_s.reshape(bs, ls, A_KV_HEADS, HEAD_DIM))
        sv.append(v_s.reshape(bs, ls, A_KV_HEADS, HEAD_DIM))
        sg.append(cv_s.reshape(bs, ls, d))

    xp = _resid_call(xp, y4, route, mod_p, 0)
    xs = _resid_call(xs, y4, route, mod_s, tp // ROW_TILE)
    return (xp, xs, jnp.stack(pk), jnp.stack(pv), jnp.stack(ps), jnp.stack(sk), jnp.stack(sv),
            s_all, jnp.stack(sg))
```

```python
import functools
import math

import numpy as np
import jax
import jax.numpy as jnp
from jax import lax
from jax.experimental import pallas as pl
from jax.experimental.pallas import tpu as pltpu
from jax.experimental.pallas import tpu_sc as plsc

F32 = jnp.float32
BF16 = jnp.bfloat16
U32 = jnp.uint32

D_MODEL = 1024
DEPTH = 4
PAST_LEN = 8192
HEAD_DIM = 64
A_Q_HEADS = 16
A_KV_HEADS = 4
A_GROUP = 4
WINDOW = 128
R_HEADS = 8
R_DK = 64
R_DV = 128
R_CHUNK = 128
ROPE_BASE = 10000.0
C_CHUNK = 128
C_GROUPS = 8
N_EXPERTS = 32
TOP_K = 4
D_FF = D_MODEL
SWIGLU_LIMIT = 7.0
SWIGLU_ALPHA = 1.702
EPS = 1e-6

Z_AQ, Z_RV, Z_RG, Z_CU, Z_RQ, Z_AK = 0, 1024, 2048, 3072, 4096, 4608
Z_CV, Z_GA, Z_GB, Z_GC, Z_RK, Z_AV = 5120, 6144, 7168, 8192, 9216, 9728
D_IN = 10240
IN_CHUNKS = 2
_ORIG = dict(aq=0, ak=1024, av=1280, rq=1536, rk=2048, rv=2560, rg=3584, cu=4608, cv=5632, mg=6656)

ROW_TILE = 512
MOE_ROWS = 256
MOE_HALVES = 4
ROUTER_PAD = 128
PACK_W = D_MODEL // 4
SC_WINDOW = 128
HI_MASK = 0xFFFF0000
VMEM_LIMIT = 56 * 1024 * 1024
NEG_BIG = -1e30


def _cparams(sem):
    return pltpu.CompilerParams(dimension_semantics=sem, vmem_limit_bytes=VMEM_LIMIT)


def _split_bf16(x):
    hi = x.astype(BF16)
    lo = (x - hi.astype(F32)).astype(BF16)
    return hi, lo


def _dot(a, b):
    return jnp.dot(a, b, preferred_element_type=F32)


def _dot_nt(a, b):
    return lax.dot_general(a, b, (((1,), (1,)), ((), ())), preferred_element_type=F32)


def _dot_tn(a, b):
    return lax.dot_general(a, b, (((0,), (0,)), ((), ())), preferred_element_type=F32)


def _ada_kernel(c_ref, w_ref, b_ref, o_ref):
    c = c_ref[...]
    s_hi, s_lo = _split_bf16(c * jax.nn.sigmoid(c))
    w_hi, w_lo = _split_bf16(w_ref[0])
    acc = _dot(s_hi, w_hi) + _dot(s_lo, w_hi) + _dot(s_hi, w_lo)
    o_ref[0] = acc + b_ref[0]


def _ada_call(c_all, w_ada, b_ada):
    depth, d, n = w_ada.shape
    m = c_all.shape[0]
    tn = 1024
    return pl.pallas_call(
        _ada_kernel,
        grid=(depth, n // tn),
        in_specs=[
            pl.BlockSpec((m, d), lambda l, j: (0, 0)),
            pl.BlockSpec((1, d, tn), lambda l, j: (l, 0, j)),
            pl.BlockSpec((1, 1, tn), lambda l, j: (l, 0, j)),
        ],
        out_specs=pl.BlockSpec((1, m, tn), lambda l, j: (l, 0, j)),
        out_shape=jax.ShapeDtypeStruct((depth, m, n), F32),
        compiler_params=_cparams(("arbitrary", "arbitrary")),
        name="ada",
    )(c_all, w_ada, b_ada.reshape(depth, 1, n))


def _pack_rows(y):
    bits = pltpu.bitcast(y.astype(BF16).astype(F32), U32)
    q = [bits[:, i * PACK_W:(i + 1) * PACK_W] for i in range(4)]
    return (q[0] >> 16) | q[1], (q[2] >> 16) | q[3]


def _unpack_rows(a, b):
    f = lambda w: pltpu.bitcast(w, F32)
    return jnp.concatenate([f(a << 16), f(a & jnp.uint32(HI_MASK)), f(b << 16), f(b & jnp.uint32(HI_MASK))], axis=-1)


def _combine(y4_ref, route_ref):
    route = route_ref[...]
    acc = None
    for k in range(TOP_K):
        term = route[:, k:k + 1] * _unpack_rows(y4_ref[0, k], y4_ref[1, k])
        acc = term if acc is None else acc + term
    return acc


def _rms(x):
    return x * lax.rsqrt(jnp.mean(x * x, axis=-1, keepdims=True) + EPS)


def _gelu(x):
    return jax.nn.gelu(x, approximate=True)


def _silu(x):
    return x * jax.nn.sigmoid(x)


def _inproj_kernel(*refs, has_resid):
    if has_resid:
        (x_ref, y4_ref, route_ref, g2_ref, sh_ref, sc_ref, lng_ref, w_ref, bg_ref, gmg_ref, gmb_ref,
         z_ref, xo_ref, hb) = refs
    else:
        x_ref, sh_ref, sc_ref, lng_ref, w_ref, bg_ref, gmg_ref, gmb_ref, z_ref, hb = refs
    c = pl.program_id(2)

    def gm_v(acc):
        v = _gelu(acc)
        vc = v - jnp.mean(v, axis=-1, keepdims=True)
        var = jnp.mean(vc * vc, axis=-1, keepdims=True)
        return vc * lax.rsqrt(var + EPS) * gmg_ref[...] + gmb_ref[...]

    merge_gate = lambda k: (lambda acc: jax.nn.sigmoid(acc + bg_ref[k:k + 1]))
    keep = lambda acc: acc
    segment_fns = ((keep, keep, _silu, _gelu, keep),
                   (gm_v, merge_gate(0), merge_gate(1), merge_gate(2), keep))

    @pl.when(c == 0)
    def _():
        x = x_ref[...]
        if has_resid:
            x = x + g2_ref[...] * _combine(y4_ref, route_ref).reshape(x.shape)
            xo_ref[...] = x
        h = _rms(x) * lng_ref[...] * (1.0 + sc_ref[...]) + sh_ref[...]
        hb[...] = h.reshape(hb.shape).astype(BF16)

    def project(half):
        h = hb[...]
        for s, fn in enumerate(segment_fns[half]):
            cols = slice(s * D_MODEL, (s + 1) * D_MODEL)
            z_ref[:, cols] = fn(_dot(h, w_ref[half, :, cols])).astype(BF16)

    for half in range(IN_CHUNKS):
        pl.when(c == half)(functools.partial(project, half))


def _group_blocks(g, r):
    if r >= ROW_TILE:
        return 1, ROW_TILE
    return ROW_TILE // r, r


def _moe_out_specs(tm, nj, tile_off):
    return [pl.BlockSpec((2, TOP_K, tm, PACK_W), lambda i, j, *_: (0, 0, tile_off + i * nj + j, 0)),
            pl.BlockSpec((tm, ROUTER_PAD), lambda i, j, *_: (tile_off + i * nj + j, 0))]


def _inproj_call(x, mod, ln_g, w_bf, b_gate, gm_ln_g, gm_ln_b, moe_out=None):
    g, r, d = x.shape
    gb, rb = _group_blocks(g, r)
    nj = r // rb
    nc, _, cw = w_bf.shape
    grid = (g // gb, nj, nc)
    xspec = pl.BlockSpec((gb, rb, d), lambda i, j, c: (i, j, 0))
    mspec = lambda col: pl.BlockSpec((gb, 1, d), lambda i, j, c: (i, 0, col))
    in_specs, args = [xspec], [x]
    if moe_out is not None:
        y4, route, mod_prev, tile_off = moe_out
        in_specs += _moe_out_specs(gb * rb, nj, tile_off) + [mspec(5)]
        args += [y4, route, mod_prev]
    in_specs += [mspec(0), mspec(1), pl.BlockSpec((1, 1, d), lambda i, j, c: (0, 0, 0)),
                 pl.BlockSpec((nc, d, cw), lambda i, j, c: (0, 0, 0), pipeline_mode=pl.Buffered(1)),
                 pl.BlockSpec((3, d), lambda i, j, c: (0, 0)),
                 pl.BlockSpec((1, d), lambda i, j, c: (0, 0)),
                 pl.BlockSpec((1, d), lambda i, j, c: (0, 0))]
    args += [mod, mod, ln_g.reshape(1, 1, d), w_bf, b_gate.reshape(3, d), gm_ln_g.reshape(1, d),
             gm_ln_b.reshape(1, d)]
    zspec = pl.BlockSpec((gb * rb, cw), lambda i, j, c: (i * nj + j, c))
    zshape = jax.ShapeDtypeStruct((g * r, nc * cw), BF16)
    if moe_out is not None:
        out_specs, out_shape = [zspec, xspec], [zshape, jax.ShapeDtypeStruct(x.shape, F32)]
    else:
        out_specs, out_shape = zspec, zshape
    return pl.pallas_call(
        functools.partial(_inproj_kernel, has_resid=moe_out is not None),
        grid=grid, in_specs=in_specs, out_specs=out_specs, out_shape=out_shape,
        scratch_shapes=[pltpu.VMEM((gb * rb, d), BF16)],
        compiler_params=_cparams(("arbitrary", "arbitrary", "arbitrary")),
        name="inproj",
    )(*args)


def _resid_kernel(x_ref, y4_ref, route_ref, g2_ref, o_ref):
    x = x_ref[...]
    o_ref[...] = x + g2_ref[...] * _combine(y4_ref, route_ref).reshape(x.shape)


def _resid_call(x, y4, route, mod, tile_off):
    g, r, d = x.shape
    gb, rb = _group_blocks(g, r)
    nj = r // rb
    xspec = pl.BlockSpec((gb, rb, d), lambda i, j: (i, j, 0))
    return pl.pallas_call(
        _resid_kernel,
        grid=(g // gb, nj),
        in_specs=[xspec] + _moe_out_specs(gb * rb, nj, tile_off)
        + [pl.BlockSpec((gb, 1, d), lambda i, j: (i, 0, 5))],
        out_specs=xspec,
        out_shape=jax.ShapeDtypeStruct(x.shape, F32),
        compiler_params=_cparams(("arbitrary", "arbitrary")),
        name="resid",
    )(x, y4, route, mod)


def _head_rms(x, g):
    return x * lax.rsqrt(jnp.mean(x * x, axis=-1, keepdims=True) + EPS) * g


def _q_col(kvh, g):
    p, kv_odd = divmod(kvh, 2)
    j, g_odd = divmod(g, 2)
    tile = p * 4 + (kv_odd ^ g_odd) * 2 + j
    return tile * 2 * HEAD_DIM + g_odd * HEAD_DIM


def _stack_q(q, qg, kvh, scale):
    parts = []
    for g in range(A_GROUP):
        c0 = _q_col(kvh, g)
        parts.append(_head_rms(q[:, c0:c0 + HEAD_DIM], qg) * scale)
    return jnp.concatenate(parts, axis=0).astype(BF16)


def _half_mats():
    r = lax.broadcasted_iota(jnp.int32, (2 * HEAD_DIM, 2 * HEAD_DIM), 0)
    c = lax.broadcasted_iota(jnp.int32, (2 * HEAD_DIM, 2 * HEAD_DIM), 1)
    seg = jnp.where(r // HEAD_DIM == c // HEAD_DIM, 1.0, 0.0).astype(BF16)
    swap = jnp.where((r + HEAD_DIM) % (2 * HEAD_DIM) == c, 1.0, 0.0).astype(BF16)
    return seg, swap


def _pair_rms(x, g2, seg):
    outs = []
    for t in range(x.shape[1] // (2 * HEAD_DIM)):
        xt = x[:, t * 2 * HEAD_DIM:(t + 1) * 2 * HEAD_DIM]
        hi, lo = _split_bf16(xt * xt)
        ss = _dot(hi, seg) + _dot(lo, seg)
        outs.append(xt * lax.rsqrt(ss * (1.0 / HEAD_DIM) + EPS) * g2)
    return jnp.concatenate(outs, axis=-1)


def _sink_col(sink_ref, kvh, rows):
    return jnp.concatenate(
        [jnp.full((rows, 1), sink_ref[kvh * A_GROUP + g], F32) for g in range(A_GROUP)], axis=0)


def _attn_prompt_kernel(sink_ref, q_ref, k_ref, v_ref, qg_ref, kg_ref, o_ref, nk_ref, nv_ref, kprev, vprev):
    n = pl.program_id(1)
    w = WINDOW

    @pl.when(n == 0)
    def _():
        kprev[...] = jnp.zeros_like(kprev)
        vprev[...] = jnp.zeros_like(vprev)

    seg, swap = _half_mats()
    tw = 2 * HEAD_DIM
    scale = HEAD_DIM ** -0.5
    v = v_ref[...]
    qn = (_pair_rms(q_ref[...].astype(F32), qg_ref[...], seg) * scale).astype(BF16)
    kn = _pair_rms(k_ref[...].astype(F32), kg_ref[...], seg)
    knb = kn.astype(BF16)
    kcat = jnp.concatenate([kprev[...], knb], axis=0)
    vcat = jnp.concatenate([vprev[...], v], axis=0)

    row = lax.broadcasted_iota(jnp.int32, (2 * w, 2 * w), 0)
    i = row % w
    j = lax.broadcasted_iota(jnp.int32, (2 * w, 2 * w), 1)
    lo = jnp.where(n == 0, w - 1, -1)
    mask = (j > i) & (j <= i + w) & (j > lo)
    sink_top = (j == 0) & (row < w)
    sink_bot = (j == 0) & (row >= w)
    lane_kv = lax.broadcasted_iota(jnp.int32, (2 * w, tw), 1)
    key_kv = lax.broadcasted_iota(jnp.int32, (2 * w, tw), 0)
    first_o = lax.broadcasted_iota(jnp.int32, (w, tw), 1) < HEAD_DIM
    scores, values = [], []
    for p in range(A_KV_HEADS // 2):
        kp = kcat[:, p * tw:(p + 1) * tw]
        vp = vcat[:, p * tw:(p + 1) * tw]
        kv_tiles = ((kp, vp), (_dot(kp, swap).astype(BF16), _dot(vp, swap).astype(BF16)))
        for variant, (kk, vv) in enumerate(kv_tiles):
            t0 = p * 4 + variant * 2
            qs = jnp.concatenate([qn[:, t0 * tw:(t0 + 1) * tw], qn[:, (t0 + 1) * tw:(t0 + 2) * tw]], axis=0)
            for half in range(2):
                keep = (lane_kv < HEAD_DIM) if half == 0 else (lane_kv >= HEAD_DIM)
                kh = jnp.where(keep, kk, jnp.zeros_like(kk))
                values.append(jnp.where(keep & (key_kv > 0), vv, jnp.zeros_like(vv)))
                kvh = 2 * p + (half ^ variant)
                s = jnp.where(mask, _dot_nt(qs, kh), NEG_BIG)
                s = jnp.where(sink_top, sink_ref[kvh * A_GROUP + half], s)
                scores.append(jnp.where(sink_bot, sink_ref[kvh * A_GROUP + 2 + half], s))
    s_all = jnp.concatenate(scores, axis=0)
    p_all = jnp.exp(s_all - jnp.max(s_all, axis=-1, keepdims=True))
    inv = 1.0 / jnp.sum(p_all, axis=-1, keepdims=True)
    p_all = p_all.astype(BF16)
    outs = [_dot(p_all[c * 2 * w:(c + 1) * 2 * w], values[c]) * inv[c * 2 * w:(c + 1) * 2 * w]
            for c in range(len(values))]
    for p in range(A_KV_HEADS // 2):
        o_a = outs[4 * p] + outs[4 * p + 1]
        o_b = outs[4 * p + 2] + outs[4 * p + 3]
        for jr in range(2):
            a = o_a[jr * w:(jr + 1) * w]
            b = o_b[jr * w:(jr + 1) * w]
            c_even = (2 * p * A_GROUP + 2 * jr) * HEAD_DIM
            c_odd = ((2 * p + 1) * A_GROUP + 2 * jr) * HEAD_DIM
            o_ref[:, c_even:c_even + tw] = jnp.where(first_o, a, b).astype(o_ref.dtype)
            o_ref[:, c_odd:c_odd + tw] = jnp.where(first_o, b, a).astype(o_ref.dtype)

    kprev[...] = knb
    vprev[...] = v

    @pl.when(n == pl.num_programs(1) - 1)
    def _():
        nk_ref[0] = kn
        nv_ref[0] = v.astype(F32)


def _attn_prompt_call(z, sinks, qg, kg, batch, seq):
    nb = seq // WINDOW
    kvw = A_KV_HEADS * HEAD_DIM
    row = lambda b, n, s: b * nb + n
    grid_spec = pltpu.PrefetchScalarGridSpec(
        num_scalar_prefetch=1,
        grid=(batch, nb),
        in_specs=[
            pl.BlockSpec((WINDOW, D_MODEL), lambda b, n, s: (row(b, n, s), Z_AQ // D_MODEL)),
            pl.BlockSpec((WINDOW, kvw), lambda b, n, s: (row(b, n, s), Z_AK // kvw)),
            pl.BlockSpec((WINDOW, kvw), lambda b, n, s: (row(b, n, s), Z_AV // kvw)),
            pl.BlockSpec((1, 2 * HEAD_DIM), lambda b, n, s: (0, 0)),
            pl.BlockSpec((1, 2 * HEAD_DIM), lambda b, n, s: (0, 0)),
        ],
        out_specs=[
            pl.BlockSpec((WINDOW, D_MODEL), lambda b, n, s: (row(b, n, s), 0)),
            pl.BlockSpec((1, WINDOW, kvw), lambda b, n, s: (b, 0, 0)),
            pl.BlockSpec((1, WINDOW, kvw), lambda b, n, s: (b, 0, 0)),
        ],
        scratch_shapes=[pltpu.VMEM((WINDOW, kvw), BF16), pltpu.VMEM((WINDOW, kvw), BF16)],
    )
    pair_gain = lambda g: jnp.tile(g, 2).reshape(1, 2 * HEAD_DIM)
    return pl.pallas_call(
        _attn_prompt_kernel,
        grid_spec=grid_spec,
        out_shape=[
            jax.ShapeDtypeStruct((batch * seq, D_MODEL), BF16),
            jax.ShapeDtypeStruct((batch, WINDOW, kvw), F32),
            jax.ShapeDtypeStruct((batch, WINDOW, kvw), F32),
        ],
        compiler_params=_cparams(("arbitrary", "arbitrary")),
        name="attn_prompt",
    )(sinks, z, z, z, pair_gain(qg), pair_gain(kg))


ATTN_S_SEQS = 8


def _attn_sample_kernel(sink_ref, q_ref, k_ref, v_ref, kc_ref, vc_ref, qg_ref, kg_ref, o_ref, nk_ref, nv_ref):
    _, sb, wb, kvw = kc_ref.shape
    l = q_ref.shape[0] // sb
    rows = sb * l
    q = q_ref[...].astype(F32)
    k = k_ref[...].astype(F32)
    v = v_ref[...]
    qg = qg_ref[...]
    kg = kg_ref[...]
    kn = jnp.concatenate(
        [_head_rms(k[:, h * HEAD_DIM:(h + 1) * HEAD_DIM], kg) for h in range(A_KV_HEADS)], axis=-1)
    nk_ref[...] = kn
    nv_ref[...] = v.astype(F32)
    knb = kn.astype(BF16)
    kc = kc_ref[0].reshape(sb * wb, kvw).astype(BF16)
    vc = vc_ref[0].reshape(sb * wb, kvw).astype(BF16)

    rq = lax.broadcasted_iota(jnp.int32, (A_GROUP * rows, sb * wb), 0) % rows
    cc = lax.broadcasted_iota(jnp.int32, (A_GROUP * rows, sb * wb), 1)
    mask_c = (rq // l == cc // wb) & (cc % wb > rq % l + (wb - WINDOW))
    rq2 = lax.broadcasted_iota(jnp.int32, (A_GROUP * rows, rows), 0) % rows
    cn = lax.broadcasted_iota(jnp.int32, (A_GROUP * rows, rows), 1)
    mask_n = (rq2 // l == cn // l) & (cn % l <= rq2 % l)
    scale = HEAD_DIM ** -0.5
    outs = []
    for h in range(A_KV_HEADS):
        hs = slice(h * HEAD_DIM, (h + 1) * HEAD_DIM)
        qs = _stack_q(q, qg, h, scale)
        s_c = jnp.where(mask_c, _dot_nt(qs, kc[:, hs]), NEG_BIG)
        s_n = jnp.where(mask_n, _dot_nt(qs, knb[:, hs]), NEG_BIG)
        sink = _sink_col(sink_ref, h, rows)
        m = jnp.maximum(jnp.maximum(jnp.max(s_c, axis=-1, keepdims=True),
                                    jnp.max(s_n, axis=-1, keepdims=True)), sink)
        p_c = jnp.exp(s_c - m)
        p_n = jnp.exp(s_n - m)
        denom = (jnp.sum(p_c, axis=-1, keepdims=True) + jnp.sum(p_n, axis=-1, keepdims=True)
                 + jnp.exp(sink - m))
        o = (_dot(p_c.astype(BF16), vc[:, hs]) + _dot(p_n.astype(BF16), v[:, hs])) / denom
        outs += [o[g * rows:(g + 1) * rows] for g in range(A_GROUP)]
    o_ref[...] = jnp.concatenate(outs, axis=-1).astype(o_ref.dtype)


def _attn_sample_call(z, kc, vc, sinks, qg, kg, batch, l, layer):
    kvw = A_KV_HEADS * HEAD_DIM
    sb = ATTN_S_SEQS
    rows = sb * l
    wb = kc.shape[1]
    grid_spec = pltpu.PrefetchScalarGridSpec(
        num_scalar_prefetch=1,
        grid=(batch // sb,),
        in_specs=[
            pl.BlockSpec((rows, D_MODEL), lambda i, s: (i, Z_AQ // D_MODEL)),
            pl.BlockSpec((rows, kvw), lambda i, s: (i, Z_AK // kvw)),
            pl.BlockSpec((rows, kvw), lambda i, s: (i, Z_AV // kvw)),
            pl.BlockSpec((1, sb, wb, kvw), lambda i, s: (layer, i, 0, 0)),
            pl.BlockSpec((1, sb, wb, kvw), lambda i, s: (layer, i, 0, 0)),
            pl.BlockSpec((1, HEAD_DIM), lambda i, s: (0, 0)),
            pl.BlockSpec((1, HEAD_DIM), lambda i, s: (0, 0)),
        ],
        out_specs=[
            pl.BlockSpec((rows, D_MODEL), lambda i, s: (i, 0)),
            pl.BlockSpec((rows, kvw), lambda i, s: (i, 0)),
            pl.BlockSpec((rows, kvw), lambda i, s: (i, 0)),
        ],
    )
    return pl.pallas_call(
        _attn_sample_kernel,
        grid_spec=grid_spec,
        out_shape=[
            jax.ShapeDtypeStruct((batch * l, D_MODEL), BF16),
            jax.ShapeDtypeStruct((batch * l, kvw), F32),
            jax.ShapeDtypeStruct((batch * l, kvw), F32),
        ],
        compiler_params=_cparams(("arbitrary",)),
        name="attn_sample",
    )(sinks, z, z, z, kc, vc, qg.reshape(1, HEAD_DIM), kg.reshape(1, HEAD_DIM))


def _ret_tables(chunk, pos0, length):
    h = np.arange(R_HEADS, dtype=np.float64)
    log_gamma = np.log1p(-np.exp2(-5.0 - h))
    idx = np.arange(chunk, dtype=np.float64)
    diff = idx[:, None] - idx[None, :]
    intra = np.where(diff[None] >= 0, np.exp(np.maximum(diff, 0.0)[None] * log_gamma[:, None, None]), 0.0)
    q_decay = np.exp((idx + 1.0)[:, None] * log_gamma[None, :])
    k_decay = np.exp((chunk - 1.0 - idx)[:, None] * log_gamma[None, :])
    c_decay = np.exp(chunk * log_gamma)
    qd = np.repeat(q_decay, R_DK, axis=1)
    kd = np.repeat(k_decay, R_DK, axis=1)
    inv_freq = ROPE_BASE ** (-np.arange(0, R_DK, 2, dtype=np.float64) / R_DK)
    ang = (pos0 + np.arange(length, dtype=np.float64))[:, None] * inv_freq[None, :]
    cos = np.tile(np.concatenate([np.cos(ang), np.cos(ang)], axis=1), (1, R_HEADS))
    sin = np.tile(np.concatenate([-np.sin(ang), np.sin(ang)], axis=1), (1, R_HEADS))
    f = lambda a: jnp.asarray(a, F32)
    return f(intra), f(qd), f(kd), [float(c) for c in c_decay], f(cos), f(sin)


def _rope(x, cos, sin):
    n = x.shape[-1]
    half = R_DK // 2
    lane = lax.broadcasted_iota(jnp.int32, x.shape, 1)
    up = pltpu.roll(x, n - half, axis=1)
    dn = pltpu.roll(x, half, axis=1)
    partner = jnp.where(lane % R_DK < half, up, dn)
    return x * cos + partner * sin


def _rope_mxu(x, cos, sin):
    tw = 2 * R_DK
    half = R_DK // 2
    r = lax.broadcasted_iota(jnp.int32, (tw, tw), 0)
    c = lax.broadcasted_iota(jnp.int32, (tw, tw), 1)
    perm = jnp.where((r // R_DK == c // R_DK) & ((r + half) % R_DK == c % R_DK), 1.0, 0.0).astype(BF16)
    parts = []
    for t in range(x.shape[1] // tw):
        hi, lo = _split_bf16(x[:, t * tw:(t + 1) * tw])
        parts.append(_dot(hi, perm) + _dot(lo, perm))
    return x * cos + jnp.concatenate(parts, axis=-1) * sin


def _head_ln(o, g):
    mu = jnp.mean(o, axis=-1, keepdims=True)
    oc = o - mu
    var = jnp.mean(oc * oc, axis=-1, keepdims=True)
    return oc * lax.rsqrt(var + EPS) * g


def _ret_prompt_kernel(q_ref, k_ref, v_ref, g_ref, cos_ref, sin_ref, intra_ref, qd_ref, kd_ref, ng_ref,
                       o_ref, s_ref, state, *, c_decay):
    n = pl.program_id(1)

    @pl.when(n == 0)
    def _():
        state[...] = jnp.zeros_like(state)

    cos = cos_ref[...]
    sin = sin_ref[...]
    q = _rope_mxu(q_ref[...].astype(F32), cos, sin)
    k = _rope_mxu(k_ref[...].astype(F32), cos, sin) * (R_DK ** -0.5)
    qb = q.astype(BF16)
    kb = k.astype(BF16)
    qdb = (q * qd_ref[...]).astype(BF16)
    kdb = (k * kd_ref[...]).astype(BF16)
    c = q.shape[0]
    ks = [slice(h * R_DK, (h + 1) * R_DK) for h in range(R_HEADS)]
    vs = [slice(h * R_DV, (h + 1) * R_DV) for h in range(R_HEADS)]
    a = jnp.concatenate([_dot_nt(qb[:, ks[h]], kb[:, ks[h]]) for h in range(R_HEADS)], axis=0)
    ab = (a * intra_ref[...].reshape(R_HEADS * c, c)).astype(BF16)
    outs = []
    for h in range(R_HEADS):
        vh = v_ref[:, vs[h]]
        s_old = state[h]
        outs.append(_dot(ab[h * c:(h + 1) * c], vh) + _dot(qdb[:, ks[h]], s_old.astype(BF16)))
        state[h] = s_old * c_decay[h] + _dot_tn(kdb[:, ks[h]], vh)
    y = _head_ln(jnp.concatenate(outs, axis=0), 1.0)
    gate = g_ref[...].astype(F32) * ng_ref[...]
    for h in range(R_HEADS):
        o_ref[:, vs[h]] = (y[h * c:(h + 1) * c] * gate[:, vs[h]]).astype(o_ref.dtype)

    @pl.when(n == pl.num_programs(1) - 1)
    def _():
        s_ref[0] = state[...]


def _ret_prompt_call(z, ret_norm_g, batch, seq):
    c = R_CHUNK
    nc = seq // c
    intra, qd, kd, c_decay, cos, sin = _ret_tables(c, 0, seq)
    qkw = R_HEADS * R_DK
    row = lambda b, n: b * nc + n
    const2 = lambda b, n: (0, 0)
    return pl.pallas_call(
        functools.partial(_ret_prompt_kernel, c_decay=c_decay),
        grid=(batch, nc),
        in_specs=[
            pl.BlockSpec((c, qkw), lambda b, n: (row(b, n), Z_RQ // qkw)),
            pl.BlockSpec((c, qkw), lambda b, n: (row(b, n), Z_RK // qkw)),
            pl.BlockSpec((c, D_MODEL), lambda b, n: (row(b, n), Z_RV // D_MODEL)),
            pl.BlockSpec((c, D_MODEL), lambda b, n: (row(b, n), Z_RG // D_MODEL)),
            pl.BlockSpec((c, qkw), lambda b, n: (n, 0)),
            pl.BlockSpec((c, qkw), lambda b, n: (n, 0)),
            pl.BlockSpec((R_HEADS, c, c), lambda b, n: (0, 0, 0)),
            pl.BlockSpec((c, qkw), const2),
            pl.BlockSpec((c, qkw), const2),
            pl.BlockSpec((1, D_MODEL), const2),
        ],
        out_specs=[
            pl.BlockSpec((c, D_MODEL), lambda b, n: (row(b, n), 0)),
            pl.BlockSpec((1, R_HEADS, R_DK, R_DV), lambda b, n: (b, 0, 0, 0)),
        ],
        out_shape=[
            jax.ShapeDtypeStruct((batch * seq, D_MODEL), BF16),
            jax.ShapeDtypeStruct((batch, R_HEADS, R_DK, R_DV), F32),
        ],
        scratch_shapes=[pltpu.VMEM((R_HEADS, R_DK, R_DV), F32)],
        compiler_params=_cparams(("arbitrary", "arbitrary")),
        name="ret_prompt",
    )(z, z, z, z, cos, sin, intra, qd, kd, ret_norm_g.reshape(1, D_MODEL))


RET_S_SEQS = 8


def _ret_sample_kernel(q_ref, k_ref, v_ref, g_ref, s0_ref, cos_ref, sin_ref, intra_ref, qd_ref, kd_ref, ng_ref,
                       *rest, c_decay, l):
    o_ref, s_ref = rest[-2:]
    sb = s0_ref.shape[1]
    rows = sb * l
    cos = cos_ref[...]
    sin = sin_ref[...]
    q = _rope(q_ref[...].astype(F32), cos, sin)
    k = _rope(k_ref[...].astype(F32), cos, sin) * (R_DK ** -0.5)
    qb = q.astype(BF16)
    kb = k.astype(BF16)
    qdb = (q * qd_ref[...]).astype(BF16)
    kdb = (k * kd_ref[...]).astype(BF16)
    for h in range(R_HEADS):
        ks = slice(h * R_DK, (h + 1) * R_DK)
        vs = slice(h * R_DV, (h + 1) * R_DV)
        vh = v_ref[:, vs]
        a = _dot_nt(qb[:, ks], kb[:, ks]) * intra_ref[h]
        o = _dot(a.astype(BF16), vh)
        cross, new_s = [], []
        for b in range(sb):
            rs = slice(b * l, (b + 1) * l)
            s_old = s0_ref[0, b, h]
            cross.append(_dot(qdb[rs, ks], s_old.astype(BF16)))
            s_ref[0, b, h] = s_old * c_decay[h] + _dot_tn(kdb[rs, ks], vh[rs])
        o = o + jnp.concatenate(cross, axis=0)
        y = _head_ln(o, ng_ref[:, vs]) * g_ref[:, vs].astype(F32)
        o_ref[:, vs] = y.astype(o_ref.dtype)


def _ret_sample_call(z, s0, ret_norm_g, batch, l, layer, stack):
    c = math.gcd(l, R_CHUNK)
    assert c == l, "sample step expects a single retention chunk"
    sb = RET_S_SEQS
    rows = sb * l
    intra, qd, kd, c_decay, cos, sin = _ret_tables(c, PAST_LEN, l)
    eye = jnp.eye(sb, dtype=F32)
    intra_bd = jnp.einsum("ab,hij->haibj", eye, intra).reshape(R_HEADS, rows, rows)
    tile = lambda t: jnp.tile(t, (sb, 1))
    qkw = R_HEADS * R_DK
    const2 = lambda i: (0, 0)
    return pl.pallas_call(
        functools.partial(_ret_sample_kernel, c_decay=c_decay, l=l),
        grid=(batch // sb,),
        in_specs=[
            pl.BlockSpec((rows, qkw), lambda i: (i, Z_RQ // qkw)),
            pl.BlockSpec((rows, qkw), lambda i: (i, Z_RK // qkw)),
            pl.BlockSpec((rows, D_MODEL), lambda i: (i, Z_RV // D_MODEL)),
            pl.BlockSpec((rows, D_MODEL), lambda i: (i, Z_RG // D_MODEL)),
            pl.BlockSpec((1, sb, R_HEADS, R_DK, R_DV), lambda i: (layer, i, 0, 0, 0)),
            pl.BlockSpec((rows, qkw), const2),
            pl.BlockSpec((rows, qkw), const2),
            pl.BlockSpec((R_HEADS, rows, rows), lambda i: (0, 0, 0)),
            pl.BlockSpec((rows, qkw), const2),
            pl.BlockSpec((rows, qkw), const2),
            pl.BlockSpec((1, D_MODEL), const2),
            pl.BlockSpec(memory_space=pl.ANY),
        ],
        out_specs=[
            pl.BlockSpec((rows, D_MODEL), lambda i: (i, 0)),
            pl.BlockSpec((1, sb, R_HEADS, R_DK, R_DV), lambda i: (layer, i, 0, 0, 0)),
        ],
        out_shape=[
            jax.ShapeDtypeStruct((batch * l, D_MODEL), BF16),
            jax.ShapeDtypeStruct(s0.shape, F32),
        ],
        input_output_aliases={11: 1},
        compiler_params=_cparams(("arbitrary",)),
        name="ret_sample",
    )(z, z, z, z, s0, tile(cos), tile(sin), intra_bd, tile(qd), tile(kd), ret_norm_g.reshape(1, D_MODEL), stack)


def _route_rows(logits, carry):
    tm = logits.shape[0]
    lane = lax.broadcasted_iota(jnp.int32, logits.shape, 1).astype(F32)
    work = logits
    sel = jnp.zeros(logits.shape, F32)
    vals, idxs = [], []
    for _ in range(TOP_K):
        m = jnp.max(work, axis=-1, keepdims=True)
        idx = jnp.min(jnp.where(work == m, lane, float(ROUTER_PAD)), axis=-1, keepdims=True)
        hit = lane == idx
        vals.append(m)
        idxs.append(idx)
        sel = jnp.where(hit, 1.0, sel)
        work = jnp.where(hit, -3e38, work)
    ex = [jnp.exp(v - vals[0]) for v in vals]
    den = ex[0] + ex[1] + ex[2] + ex[3]
    r = lax.broadcasted_iota(jnp.int32, (tm, tm), 0)
    c = lax.broadcasted_iota(jnp.int32, (tm, tm), 1)
    before = jnp.where(c < r, 1.0, 0.0).astype(BF16)
    rank = _dot(before, sel.astype(BF16)) + carry
    route = jnp.zeros(logits.shape, F32)
    for k in range(TOP_K):
        route = jnp.where(lane == float(k), ex[k] / den, route)
        route = jnp.where(lane == float(TOP_K + k), idxs[k], route)
        rk = jnp.sum(jnp.where(lane == idxs[k], rank, 0.0), axis=-1, keepdims=True)
        route = jnp.where(lane == float(2 * TOP_K + k), rk, route)
    return route, carry + jnp.sum(sel, axis=0, keepdims=True)


def _merge_kernel(*refs, emit_cv):
    (x_ref, oa_ref, ob_ref, cu_ref, cv_ref, ga_ref, gb_ref, gc_ref, g1_ref, sh2_ref, sc2_ref,
     lnf_ref, mix_ref, mixb_ref, wout_ref, rwh_ref, rwl_ref, rb_ref, cnt_ref) = refs[:19]
    n_out = 5 if emit_cv else 4
    outs = refs[-(n_out + 1):]
    if emit_cv:
        xo_ref, h2_ref, route_ref, cnto_ref, cvo_ref, carry = outs
    else:
        xo_ref, h2_ref, route_ref, cnto_ref, carry = outs
    first = (pl.program_id(0) == 0) & (pl.program_id(1) == 0)

    @pl.when(first)
    def _():
        carry[...] = cnt_ref[...]

    x = x_ref[...]
    gbk, rb, d = x.shape
    tm = gbk * rb
    cw = C_CHUNK
    gw = d // C_GROUPS

    cvb = cv_ref[...]
    if emit_cv:
        cvo_ref[...] = cvb.astype(F32)
    mixed_rows = []
    for c in range(tm // cw):
        rs = slice(c * cw, (c + 1) * cw)
        cols = [_dot(mix_ref[g], cvb[rs, g * gw:(g + 1) * gw]) for g in range(C_GROUPS)]
        mixed_rows.append(jnp.concatenate(cols, axis=-1) + mixb_ref[...])
    mixed = jnp.concatenate(mixed_rows, axis=0)
    oc = cu_ref[...].astype(F32) * mixed
    merged = (ga_ref[...].astype(F32) * oa_ref[...].astype(F32) + gb_ref[...].astype(F32) * ob_ref[...].astype(F32)
              + gc_ref[...].astype(F32) * oc)
    y = _dot(merged.astype(BF16), wout_ref[...])
    x = x + g1_ref[...] * y.reshape(gbk, rb, d)
    xo_ref[...] = x

    h2 = (_rms(x) * lnf_ref[...] * (1.0 + sc2_ref[...]) + sh2_ref[...]).reshape(tm, d)
    h2b = h2.astype(BF16)
    h2_ref[0], h2_ref[1] = _pack_rows(h2)
    h2l = (h2 - h2b.astype(F32)).astype(BF16)
    rwh = rwh_ref[...]
    logits = _dot(h2b, rwh) + _dot(h2l, rwh) + _dot(h2b, rwl_ref[...]) + rb_ref[...]
    route, counts = _route_rows(logits, carry[...])
    route_ref[...] = route
    carry[...] = counts
    cnto_ref[...] = counts


def _merge_call(x, z, oa, ob, mod, ln_ffn_g, mix, mixb, w_out_bf, rw_hi, rw_lo, rb,
                counts, t_all, tile_off, shared, emit_cv):
    g, r, d = x.shape
    gb, rb_ = _group_blocks(g, r)
    tm = gb * rb_
    nj = r // rb_
    t = g * r
    xspec = pl.BlockSpec((gb, rb_, d), lambda i, j: (i, j, 0))
    rows = lambda col: pl.BlockSpec((tm, d), lambda i, j: (i * nj + j, col))
    mspec = lambda col: pl.BlockSpec((gb, 1, d), lambda i, j: (i, 0, col))
    const = lambda shape: pl.BlockSpec(shape, lambda i, j: (0,) * len(shape))
    in_specs = [
        xspec, rows(0), rows(0),
        rows(Z_CU // d), rows(Z_CV // d), rows(Z_GA // d), rows(Z_GB // d), rows(Z_GC // d),
        mspec(2), mspec(3), mspec(4),
        const((1, 1, d)),
        const((C_GROUPS, C_CHUNK, C_CHUNK)), const((C_CHUNK, d)),
        const((d, d)), const((d, ROUTER_PAD)), const((d, ROUTER_PAD)), const((1, ROUTER_PAD)),
        const((1, ROUTER_PAD)),
    ]
    aliases = {len(in_specs): 1, len(in_specs) + 1: 2}
    in_specs += [pl.BlockSpec(memory_space=pl.ANY), pl.BlockSpec(memory_space=pl.ANY)]
    out_specs = [xspec, pl.BlockSpec((2, tm, PACK_W), lambda i, j: (0, tile_off + i * nj + j, 0)),
                 pl.BlockSpec((tm, ROUTER_PAD), lambda i, j: (tile_off + i * nj + j, 0)), const((1, ROUTER_PAD))]
    out_shape = [jax.ShapeDtypeStruct(x.shape, F32), jax.ShapeDtypeStruct((2, t_all, PACK_W), U32),
                 jax.ShapeDtypeStruct((t_all, ROUTER_PAD), F32), jax.ShapeDtypeStruct((1, ROUTER_PAD), F32)]
    if emit_cv:
        out_specs.append(rows(0))
        out_shape.append(jax.ShapeDtypeStruct((t, d), F32))
    return pl.pallas_call(
        functools.partial(_merge_kernel, emit_cv=emit_cv),
        grid=(g // gb, nj), in_specs=in_specs, out_specs=out_specs, out_shape=out_shape,
        scratch_shapes=[pltpu.VMEM((1, ROUTER_PAD), F32)],
        input_output_aliases=aliases,
        compiler_params=_cparams(("arbitrary", "arbitrary")),
        name="merge",
    )(x, oa, ob, z, z, z, z, z, mod, mod, mod, ln_ffn_g.reshape(1, 1, d),
      mix, mixb, w_out_bf, rw_hi, rw_lo, rb, counts, *shared)


def _moe_kernel(be_ref, nx_ref, nh_ref, nu_ref, x_ref, w1_hbm, b1_ref, w2_hbm, b2_ref, o_ref,
                w1s, w2s, w1b, w2b, sem, *, layer):
    i = pl.program_id(0)
    e = be_ref[i]
    prev = be_ref[jnp.maximum(i - 1, 0)]

    def weight_copies(ex):
        return (pltpu.make_async_copy(w1_hbm.at[layer, ex], w1s, sem.at[0]),
                pltpu.make_async_copy(w2_hbm.at[layer, ex], w2s, sem.at[1]))

    @pl.when(i == 0)
    def _():
        for cp in weight_copies(e):
            cp.start()

    @pl.when((i == 0) | (e != prev))
    def _():
        for cp in weight_copies(e):
            cp.wait()
        w1b[...] = w1s[...].astype(BF16)
        w2b[...] = w2s[...].astype(BF16)
        nxt = nx_ref[i]

        @pl.when(nxt >= 0)
        def _():
            for cp in weight_copies(nxt):
                cp.start()

    def experts(rows):
        rs = slice(0, rows)
        xb = _unpack_rows(x_ref[0, rs], x_ref[1, rs]).astype(BF16)
        hdn = _dot(xb, w1b[...]) + b1_ref[0, 0]
        g = jnp.minimum(hdn[:, :D_FF], SWIGLU_LIMIT)
        up = jnp.clip(hdn[:, D_FF:], -SWIGLU_LIMIT, SWIGLU_LIMIT)
        act = (up + 1.0) * (g * jax.nn.sigmoid(g * SWIGLU_ALPHA))
        o_ref[0, rs], o_ref[1, rs] = _pack_rows(_dot(act.astype(BF16), w2b[...]) + b2_ref[0, 0])

    for nh in range(1, MOE_HALVES + 1):
        pl.when(nh_ref[i] == nh)(functools.partial(experts, nh * MOE_ROWS))


def _moe_call(xb, block_e, next_e, n_halves, n_used, w1, b1, w2, b2, layer):
    _, n_rows, _ = xb.shape
    step_rows = MOE_HALVES * MOE_ROWS
    nblk = n_rows // step_rows
    depth, ne, d, f2 = w1.shape
    last = lambda i, be, nx, nh, nu: jnp.minimum(i, nu[0] - 1)
    xspec = pl.BlockSpec((2, step_rows, PACK_W), lambda i, be, nx, nh, nu: (0, last(i, be, nx, nh, nu), 0))
    grid_spec = pltpu.PrefetchScalarGridSpec(
        num_scalar_prefetch=4,
        grid=(nblk,),
        in_specs=[
            xspec,
            pl.BlockSpec(memory_space=pl.ANY),
            pl.BlockSpec((1, 1, 1, f2), lambda i, be, nx, nh, nu: (layer, be[i], 0, 0)),
            pl.BlockSpec(memory_space=pl.ANY),
            pl.BlockSpec((1, 1, 1, d), lambda i, be, nx, nh, nu: (layer, be[i], 0, 0)),
        ],
        out_specs=xspec,
        scratch_shapes=[pltpu.VMEM((d, f2), F32), pltpu.VMEM((f2 // 2, d), F32),
                        pltpu.VMEM((d, f2), BF16), pltpu.VMEM((f2 // 2, d), BF16),
                        pltpu.SemaphoreType.DMA((2,))],
    )
    return pl.pallas_call(
        functools.partial(_moe_kernel, layer=layer),
        grid_spec=grid_spec,
        out_shape=jax.ShapeDtypeStruct(xb.shape, U32),
        compiler_params=_cparams(("arbitrary",)),
        name="moe",
    )(block_e, next_e, n_halves, n_used, xb, w1, b1.reshape(depth, ne, 1, f2), w2, b2.reshape(depth, ne, 1, d))


def _sc_mesh():
    return plsc.VectorSubcoreMesh(core_axis_name="core", subcore_axis_name="subcore")


def _sc_scatter_rows(x, idx, n_out):
    t, c = x.shape
    kk = idx.shape[0]

    @pl.kernel(out_type=jax.ShapeDtypeStruct((n_out, c), x.dtype), mesh=_sc_mesh(), scratch_types=[])
    def scatter(x_hbm, i_hbm, o_hbm):
        def body(x_vmem, i_vmem):
            for k in range(kk):
                pltpu.sync_copy(x_vmem, o_hbm.at[i_vmem.at[k]])

        pltpu.emit_pipeline(
            body,
            grid=(t // SC_WINDOW,),
            in_specs=[pl.BlockSpec((SC_WINDOW, c), index_map=lambda i: (i, 0)),
                      pl.BlockSpec((kk, SC_WINDOW), index_map=lambda i: (0, i))],
            out_specs=[],
            core_axis_name=("core", "subcore"),
            dimension_semantics=(pltpu.PARALLEL,),
        )(x_hbm, i_hbm)

    return scatter(x, idx)


def _sc_gather_rows(data, idx):
    n = idx.shape[0]
    c = data.shape[1]

    @pl.kernel(out_type=jax.ShapeDtypeStruct((n, c), data.dtype), mesh=_sc_mesh(), scratch_types=[])
    def gather(x_hbm, i_hbm, o_hbm):
        def body(i_vmem, o_vmem):
            pltpu.sync_copy(x_hbm.at[i_vmem.at[0]], o_vmem)

        pltpu.emit_pipeline(
            body,
            grid=(n // SC_WINDOW,),
            in_specs=[pl.BlockSpec((1, SC_WINDOW), index_map=lambda i: (0, i))],
            out_specs=[pl.BlockSpec((SC_WINDOW, c), index_map=lambda i: (i, 0))],
            core_axis_name=("core", "subcore"),
            dimension_semantics=(pltpu.PARALLEL,),
        )(i_hbm, o_hbm)

    return gather(data, idx.reshape(1, n))


def _plan(route, counts):
    t = route.shape[0]
    step_rows = MOE_HALVES * MOE_ROWS
    n_steps = -(-t * TOP_K // step_rows) + N_EXPERTS
    e4 = route[:, TOP_K:2 * TOP_K].astype(jnp.int32)
    r4 = route[:, 2 * TOP_K:3 * TOP_K].astype(jnp.int32)
    cnt = counts[0, :N_EXPERTS].astype(jnp.int32)
    halves = (cnt + MOE_ROWS - 1) // MOE_ROWS
    steps = (halves + MOE_HALVES - 1) // MOE_HALVES
    send = jnp.cumsum(steps)
    sstart = send - steps
    onehot = e4[:, :, None] == jnp.arange(N_EXPERTS, dtype=jnp.int32)[None, None, :]
    dest = (r4 + jnp.sum(jnp.where(onehot, (sstart * step_rows)[None, None, :], 0), axis=-1)).T
    step = jnp.arange(n_steps, dtype=jnp.int32)
    step_e = jnp.sum((step[:, None] >= send[None, :]).astype(jnp.int32), axis=-1)
    n_used = send[-1].astype(jnp.int32).reshape(1)
    step_e = jnp.minimum(step_e, jnp.sum((send < send[-1]).astype(jnp.int32))).astype(jnp.int32)
    mine = step_e[:, None] == jnp.arange(N_EXPERTS, dtype=jnp.int32)[None, :]
    pick = lambda table: jnp.sum(jnp.where(mine, table[None, :], 0), axis=-1)
    n_halves = jnp.clip(pick(halves) - MOE_HALVES * (step - pick(sstart)), 0, MOE_HALVES)
    n_halves = jnp.where(step < n_used[0], n_halves, 0).astype(jnp.int32)
    after = pick(send)
    next_e = jnp.sum((after[:, None] >= send[None, :]).astype(jnp.int32), axis=-1)
    next_e = jnp.where(after < n_used[0], next_e, -1).astype(jnp.int32)
    return dest, step_e, next_e, n_halves, n_used, n_steps * step_rows


def _moe(h2p, route, counts, w1, b1, w2, b2, layer):
    _, t, pw = h2p.shape
    dest, step_e, next_e, n_halves, n_used, n_rows = _plan(route, counts)
    idx_s = jnp.concatenate([dest, dest + n_rows], axis=1)
    xb = _sc_scatter_rows(h2p.reshape(2 * t, pw), idx_s, 2 * n_rows).reshape(2, n_rows, pw)
    yb = _moe_call(xb, step_e, next_e, n_halves, n_used, w1, b1, w2, b2, layer)
    idx_g = jnp.concatenate([dest.reshape(-1), dest.reshape(-1) + n_rows])
    y4 = _sc_gather_rows(yb.reshape(2 * n_rows, pw), idx_g)
    return y4.reshape(2, TOP_K, t, pw)


def _reorder_w_in(w):
    o = _ORIG
    seg = lambda a, n: w[:, a:a + n]
    d = D_MODEL
    q_heads = sorted(((_q_col(kv, g), kv * A_GROUP + g) for kv in range(A_KV_HEADS) for g in range(A_GROUP)))
    aq = jnp.concatenate([seg(o["aq"] + h * HEAD_DIM, HEAD_DIM) for _, h in q_heads], axis=1)
    pad = jnp.zeros((d, 256), w.dtype)
    parts = [aq, seg(o["rv"], d), seg(o["rg"], d), seg(o["cu"], d), seg(o["rq"], 512), seg(o["ak"], 256), pad,
             seg(o["cv"], d), seg(o["mg"], d), seg(o["mg"] + d, d), seg(o["mg"] + 2 * d, d),
             seg(o["rk"], 512), seg(o["av"], 256), pad]
    w = jnp.concatenate(parts, axis=1).astype(BF16)
    return w.reshape(D_MODEL, IN_CHUNKS, D_IN // IN_CHUNKS).transpose(1, 0, 2)


def _gmlp_tables(ws, bs, chunk_len, rows):
    causal = jnp.tril(jnp.ones((C_CHUNK, C_CHUNK), dtype=bool))
    w = jnp.where(causal[None], ws, 0.0)[:, :chunk_len, :chunk_len]
    reps = rows // chunk_len
    eye = jnp.eye(reps, dtype=F32)
    mix = jnp.einsum("ab,gts->gatbs", eye, w).reshape(C_GROUPS, rows, rows).astype(BF16)
    b = jnp.tile(bs[:, :chunk_len].T, (reps, 1))
    mixb = jnp.repeat(b, D_MODEL // C_GROUPS, axis=1)
    return mix, mixb


def kernel(x_prompt, x_sample, c_prompt, c_sample, cache_attn_k, cache_attn_v, state_retention, ln_mix_g, ln_ffn_g, w_ada, b_ada, w_in, b_gate, q_norm_g, k_norm_g, attn_sinks, ret_norm_g, gm_ln_g, gm_ln_b, gm_ws, gm_bs, w_out, router_w, router_b, moe_w1, moe_b1, moe_w2, moe_b2):
    bp, lp, d = x_prompt.shape
    bs, ls, _ = x_sample.shape
    tp, ts = bp * lp, bs * ls
    kvw = A_KV_HEADS * HEAD_DIM
    wb = cache_attn_k.shape[2]

    mod_all = _ada_call(jnp.concatenate([c_prompt, c_sample], axis=0), w_ada, b_ada)

    cache_k = cache_attn_k.reshape(DEPTH, bs, wb, kvw)
    cache_v = cache_attn_v.reshape(DEPTH, bs, wb, kvw)
    xp, xs = x_prompt, x_sample
    y4 = None
    mod_p = mod_s = None
    s_all = jnp.zeros(state_retention.shape, F32)
    h2 = jnp.zeros((2, tp + ts, PACK_W), U32)
    route = jnp.zeros((tp + ts, ROUTER_PAD), F32)
    pk, pv, ps, sk, sv, sg = [], [], [], [], [], []
    for l in range(DEPTH):
        prev_mod_p, prev_mod_s = mod_p, mod_s
        mod_p = mod_all[l, :bp].reshape(bp, 1, 6 * d)
        mod_s = mod_all[l, bp:].reshape(bs, 1, 6 * d)
        w_in_bf = _reorder_w_in(w_in[l])
        w_out_bf = w_out[l].astype(BF16)
        rw = jnp.pad(router_w[l], ((0, 0), (0, ROUTER_PAD - N_EXPERTS)))
        rw_hi, rw_lo = _split_bf16(rw)
        rb = jnp.pad(router_b[l], (0, ROUTER_PAD - N_EXPERTS), constant_values=NEG_BIG).reshape(1, ROUTER_PAD)

        proj = (ln_mix_g[l], w_in_bf, b_gate[l], gm_ln_g[l], gm_ln_b[l])
        if l == 0:
            zp = _inproj_call(xp, mod_p, *proj)
            zs = _inproj_call(xs, mod_s, *proj)
        else:
            zp, xp = _inproj_call(xp, mod_p, *proj, moe_out=(y4, route, prev_mod_p, 0))
            zs, xs = _inproj_call(xs, mod_s, *proj, moe_out=(y4, route, prev_mod_s, tp // ROW_TILE))
        oa_p, k_p, v_p = _attn_prompt_call(zp, attn_sinks[l], q_norm_g[l], k_norm_g[l], bp, lp)
        oa_s, k_s, v_s = _attn_sample_call(zs, cache_k, cache_v, attn_sinks[l], q_norm_g[l], k_norm_g[l], bs, ls, l)
        ob_p, s_p = _ret_prompt_call(zp, ret_norm_g[l], bp, lp)
        ob_s, s_all = _ret_sample_call(zs, state_retention, ret_norm_g[l], bs, ls, l, stack=s_all)

        mix_p, mixb_p = _gmlp_tables(gm_ws[l], gm_bs[l], C_CHUNK, C_CHUNK)
        mix_s, mixb_s = _gmlp_tables(gm_ws[l], gm_bs[l], ls, C_CHUNK)
        common = (ln_ffn_g[l],)
        tail = (w_out_bf, rw_hi, rw_lo, rb)
        zero_counts = jnp.zeros((1, ROUTER_PAD), F32)
        xp, h2, route, cnt_p = _merge_call(xp, zp, oa_p, ob_p, mod_p, *common, mix_p, mixb_p, *tail,
                                           zero_counts, tp + ts, 0, (h2, route), emit_cv=False)
        xs, h2, route, cnt, cv_s = _merge_call(xs, zs, oa_s, ob_s, mod_s, *common, mix_s, mixb_s, *tail,
                                               cnt_p, tp + ts, tp // ROW_TILE, (h2, route), emit_cv=True)
        y4 = _moe(h2, route, cnt, moe_w1, moe_b1, moe_w2, moe_b2, l)

        pk.append(k_p.reshape(bp, WINDOW, A_KV_HEADS, HEAD_DIM))
        pv.append(v_p.reshape(bp, WINDOW, A_KV_HEADS, HEAD_DIM))
        ps.append(s_p)
        sk.append(k_s.reshape(bs, ls, A_KV_HEADS, HEAD_DIM))
        sv.append(v_s.reshape(bs, ls, A_KV_HEADS, HEAD_DIM))
        sg.append(cv_s.reshape(bs, ls, d))

    xp = _resid_call(xp, y4, route, mod_p, 0)
    xs = _resid_call(xs, y4, route, mod_s, tp // ROW_TILE)
    return (xp, xs, jnp.stack(pk), jnp.stack(pv), jnp.stack(ps), jnp.stack(sk), jnp.stack(sv),
            s_all, jnp.stack(sg))
```

```python
import functools
import math

import numpy as np
import jax
import jax.numpy as jnp
from jax import lax
from jax.experimental import pallas as pl
from jax.experimental.pallas import tpu as pltpu
from jax.experimental.pallas import tpu_sc as plsc

F32 = jnp.float32
BF16 = jnp.bfloat16
U32 = jnp.uint32

D_MODEL = 1024
DEPTH = 4
PAST_LEN = 8192
HEAD_DIM = 64
A_Q_HEADS = 16
A_KV_HEADS = 4
A_GROUP = 4
WINDOW = 128
R_HEADS = 8
R_DK = 64
R_DV = 128
R_CHUNK = 128
ROPE_BASE = 10000.0
C_CHUNK = 128
C_GROUPS = 8
N_EXPERTS = 32
TOP_K = 4
D_FF = D_MODEL
SWIGLU_LIMIT = 7.0
SWIGLU_ALPHA = 1.702
EPS = 1e-6

Z_AQ, Z_RV, Z_RG, Z_CU, Z_RQ, Z_AK = 0, 1024, 2048, 3072, 4096, 4608
Z_CV, Z_GA, Z_GB, Z_GC, Z_RK, Z_AV = 5120, 6144, 7168, 8192, 9216, 9728
D_IN = 10240
IN_CHUNKS = 2
_ORIG = dict(aq=0, ak=1024, av=1280, rq=1536, rk=2048, rv=2560, rg=3584, cu=4608, cv=5632, mg=6656)

ROW_TILE = 512
MOE_ROWS = 256
MOE_HALVES = 4
ROUTER_PAD = 128
PACK_W = D_MODEL // 4
SC_WINDOW = 128
HI_MASK = 0xFFFF0000
VMEM_LIMIT = 56 * 1024 * 1024
NEG_BIG = -1e30


def _cparams(sem):
    return pltpu.CompilerParams(dimension_semantics=sem, vmem_limit_bytes=VMEM_LIMIT)


def _split_bf16(x):
    hi = x.astype(BF16)
    lo = (x - hi.astype(F32)).astype(BF16)
    return hi, lo


def _dot(a, b):
    return jnp.dot(a, b, preferred_element_type=F32)


def _dot_nt(a, b):
    return lax.dot_general(a, b, (((1,), (1,)), ((), ())), preferred_element_type=F32)


def _dot_tn(a, b):
    return lax.dot_general(a, b, (((0,), (0,)), ((), ())), preferred_element_type=F32)


def _ada_kernel(c_ref, w_ref, b_ref, o_ref):
    c = c_ref[...]
    s_hi, s_lo = _split_bf16(c * jax.nn.sigmoid(c))
    w_hi, w_lo = _split_bf16(w_ref[0])
    acc = _dot(s_hi, w_hi) + _dot(s_lo, w_hi) + _dot(s_hi, w_lo)
    o_ref[0] = acc + b_ref[0]


def _ada_call(c_all, w_ada, b_ada):
    depth, d, n = w_ada.shape
    m = c_all.shape[0]
    tn = 1024
    return pl.pallas_call(
        _ada_kernel,
        grid=(depth, n // tn),
        in_specs=[
            pl.BlockSpec((m, d), lambda l, j: (0, 0)),
            pl.BlockSpec((1, d, tn), lambda l, j: (l, 0, j)),
            pl.BlockSpec((1, 1, tn), lambda l, j: (l, 0, j)),
        ],
        out_specs=pl.BlockSpec((1, m, tn), lambda l, j: (l, 0, j)),
        out_shape=jax.ShapeDtypeStruct((depth, m, n), F32),
        compiler_params=_cparams(("arbitrary", "arbitrary")),
        name="ada",
    )(c_all, w_ada, b_ada.reshape(depth, 1, n))


def _pack_rows(y):
    bits = pltpu.bitcast(y.astype(BF16).astype(F32), U32)
    q = [bits[:, i * PACK_W:(i + 1) * PACK_W] for i in range(4)]
    return (q[0] >> 16) | q[1], (q[2] >> 16) | q[3]


def _unpack_rows(a, b):
    f = lambda w: pltpu.bitcast(w, F32)
    return jnp.concatenate([f(a << 16), f(a & jnp.uint32(HI_MASK)), f(b << 16), f(b & jnp.uint32(HI_MASK))], axis=-1)


def _combine(y4_ref, route_ref):
    route = route_ref[...]
    acc = None
    for k in range(TOP_K):
        term = route[:, k:k + 1] * _unpack_rows(y4_ref[0, k], y4_ref[1, k])
        acc = term if acc is None else acc + term
    return acc


def _rms(x):
    return x * lax.rsqrt(jnp.mean(x * x, axis=-1, keepdims=True) + EPS)


def _gelu(x):
    return jax.nn.gelu(x, approximate=True)


def _silu(x):
    return x * jax.nn.sigmoid(x)


def _inproj_kernel(*refs, has_resid):
    if has_resid:
        (x_ref, y4_ref, route_ref, g2_ref, sh_ref, sc_ref, lng_ref, w_ref, bg_ref, gmg_ref, gmb_ref,
         z_ref, xo_ref, hb) = refs
    else:
        x_ref, sh_ref, sc_ref, lng_ref, w_ref, bg_ref, gmg_ref, gmb_ref, z_ref, hb = refs
    c = pl.program_id(2)

    def gm_v(acc):
        v = _gelu(acc)
        vc = v - jnp.mean(v, axis=-1, keepdims=True)
        var = jnp.mean(vc * vc, axis=-1, keepdims=True)
        return vc * lax.rsqrt(var + EPS) * gmg_ref[...] + gmb_ref[...]

    merge_gate = lambda k: (lambda acc: jax.nn.sigmoid(acc + bg_ref[k:k + 1]))
    keep = lambda acc: acc
    segment_fns = ((keep, keep, _silu, _gelu, keep),
                   (gm_v, merge_gate(0), merge_gate(1), merge_gate(2), keep))

    @pl.when(c == 0)
    def _():
        x = x_ref[...]
        if has_resid:
            x = x + g2_ref[...] * _combine(y4_ref, route_ref).reshape(x.shape)
            xo_ref[...] = x
        h = _rms(x) * lng_ref[...] * (1.0 + sc_ref[...]) + sh_ref[...]
        hb[...] = h.reshape(hb.shape).astype(BF16)

    def project(half):
        h = hb[...]
        for s, fn in enumerate(segment_fns[half]):
            cols = slice(s * D_MODEL, (s + 1) * D_MODEL)
            z_ref[:, cols] = fn(_dot(h, w_ref[half, :, cols])).astype(BF16)

    for half in range(IN_CHUNKS):
        pl.when(c == half)(functools.partial(project, half))


def _group_blocks(g, r):
    if r >= ROW_TILE:
        return 1, ROW_TILE
    return ROW_TILE // r, r


def _moe_out_specs(tm, nj, tile_off):
    return [pl.BlockSpec((2, TOP_K, tm, PACK_W), lambda i, j, *_: (0, 0, tile_off + i * nj + j, 0)),
            pl.BlockSpec((tm, ROUTER_PAD), lambda i, j, *_: (tile_off + i * nj + j, 0))]


def _inproj_call(x, mod, ln_g, w_bf, b_gate, gm_ln_g, gm_ln_b, moe_out=None):
    g, r, d = x.shape
    gb, rb = _group_blocks(g, r)
    nj = r // rb
    nc, _, cw = w_bf.shape
    grid = (g // gb, nj, nc)
    xspec = pl.BlockSpec((gb, rb, d), lambda i, j, c: (i, j, 0))
    mspec = lambda col: pl.BlockSpec((gb, 1, d), lambda i, j, c: (i, 0, col))
    in_specs, args = [xspec], [x]
    if moe_out is not None:
        y4, route, mod_prev, tile_off = moe_out
        in_specs += _moe_out_specs(gb * rb, nj, tile_off) + [mspec(5)]
        args += [y4, route, mod_prev]
    in_specs += [mspec(0), mspec(1), pl.BlockSpec((1, 1, d), lambda i, j, c: (0, 0, 0)),
                 pl.BlockSpec((nc, d, cw), lambda i, j, c: (0, 0, 0), pipeline_mode=pl.Buffered(1)),
                 pl.BlockSpec((3, d), lambda i, j, c: (0, 0)),
                 pl.BlockSpec((1, d), lambda i, j, c: (0, 0)),
                 pl.BlockSpec((1, d), lambda i, j, c: (0, 0))]
    args += [mod, mod, ln_g.reshape(1, 1, d), w_bf, b_gate.reshape(3, d), gm_ln_g.reshape(1, d),
             gm_ln_b.reshape(1, d)]
    zspec = pl.BlockSpec((gb * rb, cw), lambda i, j, c: (i * nj + j, c))
    zshape = jax.ShapeDtypeStruct((g * r, nc * cw), BF16)
    if moe_out is not None:
        out_specs, out_shape = [zspec, xspec], [zshape, jax.ShapeDtypeStruct(x.shape, F32)]
    else:
        out_specs, out_shape = zspec, zshape
    return pl.pallas_call(
        functools.partial(_inproj_kernel, has_resid=moe_out is not None),
        grid=grid, in_specs=in_specs, out_specs=out_specs, out_shape=out_shape,
        scratch_shapes=[pltpu.VMEM((gb * rb, d), BF16)],
        compiler_params=_cparams(("arbitrary", "arbitrary", "arbitrary")),
        name="inproj",
    )(*args)


def _resid_kernel(x_ref, y4_ref, route_ref, g2_ref, o_ref):
    x = x_ref[...]
    o_ref[...] = x + g2_ref[...] * _combine(y4_ref, route_ref).reshape(x.shape)


def _resid_call(x, y4, route, mod, tile_off):
    g, r, d = x.shape
    gb, rb = _group_blocks(g, r)
    nj = r // rb
    xspec = pl.BlockSpec((gb, rb, d), lambda i, j: (i, j, 0))
    return pl.pallas_call(
        _resid_kernel,
        grid=(g // gb, nj),
        in_specs=[xspec] + _moe_out_specs(gb * rb, nj, tile_off)
        + [pl.BlockSpec((gb, 1, d), lambda i, j: (i, 0, 5))],
        out_specs=xspec,
        out_shape=jax.ShapeDtypeStruct(x.shape, F32),
        compiler_params=_cparams(("arbitrary", "arbitrary")),
        name="resid",
    )(x, y4, route, mod)


def _q_col(kvh, g):
    p, kv_odd = divmod(kvh, 2)
    j, g_odd = divmod(g, 2)
    tile = p * 4 + (kv_odd ^ g_odd) * 2 + j
    return tile * 2 * HEAD_DIM + g_odd * HEAD_DIM


def _half_mats():
    r = lax.broadcasted_iota(jnp.int32, (2 * HEAD_DIM, 2 * HEAD_DIM), 0)
    c = lax.broadcasted_iota(jnp.int32, (2 * HEAD_DIM, 2 * HEAD_DIM), 1)
    seg = jnp.where(r // HEAD_DIM == c // HEAD_DIM, 1.0, 0.0).astype(BF16)
    swap = jnp.where((r + HEAD_DIM) % (2 * HEAD_DIM) == c, 1.0, 0.0).astype(BF16)
    return seg, swap


def _pair_rms(x, g2, seg):
    outs = []
    for t in range(x.shape[1] // (2 * HEAD_DIM)):
        xt = x[:, t * 2 * HEAD_DIM:(t + 1) * 2 * HEAD_DIM]
        hi, lo = _split_bf16(xt * xt)
        ss = _dot(hi, seg) + _dot(lo, seg)
        outs.append(xt * lax.rsqrt(ss * (1.0 / HEAD_DIM) + EPS) * g2)
    return jnp.concatenate(outs, axis=-1)


def _pair_tile_attention(sink_ref, qn, kall, vall, mask, w, o_ref):
    _, swap = _half_mats()
    tw = 2 * HEAD_DIM
    nk = kall.shape[0]
    row = lax.broadcasted_iota(jnp.int32, (2 * w, nk), 0)
    col = lax.broadcasted_iota(jnp.int32, (2 * w, nk), 1)
    sink_top = (col == 0) & (row < w)
    sink_bot = (col == 0) & (row >= w)
    lane_kv = lax.broadcasted_iota(jnp.int32, (nk, tw), 1)
    key_kv = lax.broadcasted_iota(jnp.int32, (nk, tw), 0)
    first_o = lax.broadcasted_iota(jnp.int32, (w, tw), 1) < HEAD_DIM
    scores, values = [], []
    for p in range(A_KV_HEADS // 2):
        kp = kall[:, p * tw:(p + 1) * tw]
        vp = vall[:, p * tw:(p + 1) * tw]
        kv_tiles = ((kp, vp), (_dot(kp, swap).astype(BF16), _dot(vp, swap).astype(BF16)))
        for variant, (kk, vv) in enumerate(kv_tiles):
            t0 = p * 4 + variant * 2
            qs = jnp.concatenate([qn[:, t0 * tw:(t0 + 1) * tw], qn[:, (t0 + 1) * tw:(t0 + 2) * tw]], axis=0)
            for half in range(2):
                keep = (lane_kv < HEAD_DIM) if half == 0 else (lane_kv >= HEAD_DIM)
                kh = jnp.where(keep, kk, jnp.zeros_like(kk))
                values.append(jnp.where(keep & (key_kv > 0), vv, jnp.zeros_like(vv)))
                kvh = 2 * p + (half ^ variant)
                s = jnp.where(mask, _dot_nt(qs, kh), NEG_BIG)
                s = jnp.where(sink_top, sink_ref[kvh * A_GROUP + half], s)
                scores.append(jnp.where(sink_bot, sink_ref[kvh * A_GROUP + 2 + half], s))
    s_all = jnp.concatenate(scores, axis=0)
    p_all = jnp.exp(s_all - jnp.max(s_all, axis=-1, keepdims=True))
    inv = 1.0 / jnp.sum(p_all, axis=-1, keepdims=True)
    p_all = p_all.astype(BF16)
    outs = [_dot(p_all[c * 2 * w:(c + 1) * 2 * w], values[c]) * inv[c * 2 * w:(c + 1) * 2 * w]
            for c in range(len(values))]
    for p in range(A_KV_HEADS // 2):
        o_a = outs[4 * p] + outs[4 * p + 1]
        o_b = outs[4 * p + 2] + outs[4 * p + 3]
        for jr in range(2):
            a = o_a[jr * w:(jr + 1) * w]
            b = o_b[jr * w:(jr + 1) * w]
            c_even = (2 * p * A_GROUP + 2 * jr) * HEAD_DIM
            c_odd = ((2 * p + 1) * A_GROUP + 2 * jr) * HEAD_DIM
            o_ref[:, c_even:c_even + tw] = jnp.where(first_o, a, b).astype(o_ref.dtype)
            o_ref[:, c_odd:c_odd + tw] = jnp.where(first_o, b, a).astype(o_ref.dtype)


def _attn_prompt_kernel(sink_ref, q_ref, k_ref, v_ref, qg_ref, kg_ref, o_ref, nk_ref, nv_ref, kprev, vprev):
    n = pl.program_id(1)
    w = WINDOW

    @pl.when(n == 0)
    def _():
        kprev[...] = jnp.zeros_like(kprev)
        vprev[...] = jnp.zeros_like(vprev)

    seg, _ = _half_mats()
    v = v_ref[...]
    qn = (_pair_rms(q_ref[...].astype(F32), qg_ref[...], seg) * HEAD_DIM ** -0.5).astype(BF16)
    kn = _pair_rms(k_ref[...].astype(F32), kg_ref[...], seg)
    knb = kn.astype(BF16)
    kcat = jnp.concatenate([kprev[...], knb], axis=0)
    vcat = jnp.concatenate([vprev[...], v], axis=0)

    i = lax.broadcasted_iota(jnp.int32, (2 * w, 2 * w), 0) % w
    j = lax.broadcasted_iota(jnp.int32, (2 * w, 2 * w), 1)
    lo = jnp.where(n == 0, w - 1, -1)
    mask = (j > i) & (j <= i + w) & (j > lo)
    _pair_tile_attention(sink_ref, qn, kcat, vcat, mask, w, o_ref)

    kprev[...] = knb
    vprev[...] = v

    @pl.when(n == pl.num_programs(1) - 1)
    def _():
        nk_ref[0] = kn
        nv_ref[0] = v.astype(F32)


ATTN_S_SEQS = 8


def _attn_sample_kernel(sink_ref, q_ref, k_ref, v_ref, kc_ref, vc_ref, qg_ref, kg_ref, o_ref, nk_ref, nv_ref):
    _, sb, wb, kvw = kc_ref.shape
    rows = q_ref.shape[0]
    l = rows // sb
    seg, _ = _half_mats()
    v = v_ref[...]
    qn = (_pair_rms(q_ref[...].astype(F32), qg_ref[...], seg) * HEAD_DIM ** -0.5).astype(BF16)
    kn = _pair_rms(k_ref[...].astype(F32), kg_ref[...], seg)
    nk_ref[...] = kn
    nv_ref[...] = v.astype(F32)
    nc = sb * wb
    kall = jnp.concatenate([kc_ref[0].reshape(nc, kvw).astype(BF16), kn.astype(BF16)], axis=0)
    vall = jnp.concatenate([vc_ref[0].reshape(nc, kvw).astype(BF16), v], axis=0)

    r = lax.broadcasted_iota(jnp.int32, (2 * rows, nc + rows), 0) % rows
    c = lax.broadcasted_iota(jnp.int32, (2 * rows, nc + rows), 1)
    cached = c < nc
    key_seq = jnp.where(cached, c // wb, (c - nc) // l)
    i = r % l
    seen = jnp.where(cached, c % wb - (wb - WINDOW), i + 1) > jnp.where(cached, i, (c - nc) % l)
    mask = (key_seq == r // l) & seen
    _pair_tile_attention(sink_ref, qn, kall, vall, mask, rows, o_ref)


def _attn_sample_call(z, kc, vc, sinks, qg, kg, batch, l, layer):
    kvw = A_KV_HEADS * HEAD_DIM
    sb = ATTN_S_SEQS
    rows = sb * l
    wb = kc.shape[2]
    assert wb == WINDOW, "key 0 of the window buffer must be out of every new token's window"
    pair_gain = lambda g: jnp.tile(g, 2).reshape(1, 2 * HEAD_DIM)
    grid_spec = pltpu.PrefetchScalarGridSpec(
        num_scalar_prefetch=1,
        grid=(batch // sb,),
        in_specs=[
            pl.BlockSpec((rows, D_MODEL), lambda i, s: (i, Z_AQ // D_MODEL)),
            pl.BlockSpec((rows, kvw), lambda i, s: (i, Z_AK // kvw)),
            pl.BlockSpec((rows, kvw), lambda i, s: (i, Z_AV // kvw)),
            pl.BlockSpec((1, sb, wb, kvw), lambda i, s: (layer, i, 0, 0)),
            pl.BlockSpec((1, sb, wb, kvw), lambda i, s: (layer, i, 0, 0)),
            pl.BlockSpec((1, 2 * HEAD_DIM), lambda i, s: (0, 0)),
            pl.BlockSpec((1, 2 * HEAD_DIM), lambda i, s: (0, 0)),
        ],
        out_specs=[
            pl.BlockSpec((rows, D_MODEL), lambda i, s: (i, 0)),
            pl.BlockSpec((rows, kvw), lambda i, s: (i, 0)),
            pl.BlockSpec((rows, kvw), lambda i, s: (i, 0)),
        ],
    )
    return pl.pallas_call(
        _attn_sample_kernel,
        grid_spec=grid_spec,
        out_shape=[
            jax.ShapeDtypeStruct((batch * l, D_MODEL), BF16),
            jax.ShapeDtypeStruct((batch * l, kvw), F32),
            jax.ShapeDtypeStruct((batch * l, kvw), F32),
        ],
        compiler_params=_cparams(("arbitrary",)),
        name="attn_sample",
    )(sinks, z, z, z, kc, vc, pair_gain(qg), pair_gain(kg))


def _ret_tables(chunk, pos0, length):
    h = np.arange(R_HEADS, dtype=np.float64)
    log_gamma = np.log1p(-np.exp2(-5.0 - h))
    idx = np.arange(chunk, dtype=np.float64)
    diff = idx[:, None] - idx[None, :]
    intra = np.where(diff[None] >= 0, np.exp(np.maximum(diff, 0.0)[None] * log_gamma[:, None, None]), 0.0)
    q_decay = np.exp((idx + 1.0)[:, None] * log_gamma[None, :])
    k_decay = np.exp((chunk - 1.0 - idx)[:, None] * log_gamma[None, :])
    c_decay = np.exp(chunk * log_gamma)
    qd = np.repeat(q_decay, R_DK, axis=1)
    kd = np.repeat(k_decay, R_DK, axis=1)
    inv_freq = ROPE_BASE ** (-np.arange(0, R_DK, 2, dtype=np.float64) / R_DK)
    ang = (pos0 + np.arange(length, dtype=np.float64))[:, None] * inv_freq[None, :]
    cos = np.tile(np.concatenate([np.cos(ang), np.cos(ang)], axis=1), (1, R_HEADS))
    sin = np.tile(np.concatenate([-np.sin(ang), np.sin(ang)], axis=1), (1, R_HEADS))
    f = lambda a: jnp.asarray(a, F32)
    return f(intra), f(qd), f(kd), [float(c) for c in c_decay], f(cos), f(sin)


def _rope(x, cos, sin):
    n = x.shape[-1]
    half = R_DK // 2
    lane = lax.broadcasted_iota(jnp.int32, x.shape, 1)
    up = pltpu.roll(x, n - half, axis=1)
    dn = pltpu.roll(x, half, axis=1)
    partner = jnp.where(lane % R_DK < half, up, dn)
    return x * cos + partner * sin


def _rope_mxu(x, cos, sin):
    tw = 2 * R_DK
    half = R_DK // 2
    r = lax.broadcasted_iota(jnp.int32, (tw, tw), 0)
    c = lax.broadcasted_iota(jnp.int32, (tw, tw), 1)
    perm = jnp.where((r // R_DK == c // R_DK) & ((r + half) % R_DK == c % R_DK), 1.0, 0.0).astype(BF16)
    parts = []
    for t in range(x.shape[1] // tw):
        hi, lo = _split_bf16(x[:, t * tw:(t + 1) * tw])
        parts.append(_dot(hi, perm) + _dot(lo, perm))
    return x * cos + jnp.concatenate(parts, axis=-1) * sin


def _head_ln(o, g):
    mu = jnp.mean(o, axis=-1, keepdims=True)
    oc = o - mu
    var = jnp.mean(oc * oc, axis=-1, keepdims=True)
    return oc * lax.rsqrt(var + EPS) * g


def _ret_prompt_kernel(q_ref, k_ref, v_ref, g_ref, cos_ref, sin_ref, intra_ref, qd_ref, kd_ref, ng_ref,
                       o_ref, s_ref, state, *, c_decay):
    n = pl.program_id(1)

    @pl.when(n == 0)
    def _():
        state[...] = jnp.zeros_like(state)

    cos = cos_ref[...]
    sin = sin_ref[...]
    q = _rope_mxu(q_ref[...].astype(F32), cos, sin)
    k = _rope_mxu(k_ref[...].astype(F32), cos, sin) * (R_DK ** -0.5)
    qb = q.astype(BF16)
    kb = k.astype(BF16)
    qdb = (q * qd_ref[...]).astype(BF16)
    kdb = (k * kd_ref[...]).astype(BF16)
    c = q.shape[0]
    ks = [slice(h * R_DK, (h + 1) * R_DK) for h in range(R_HEADS)]
    vs = [slice(h * R_DV, (h + 1) * R_DV) for h in range(R_HEADS)]
    a = jnp.concatenate([_dot_nt(qb[:, ks[h]], kb[:, ks[h]]) for h in range(R_HEADS)], axis=0)
    ab = (a * intra_ref[...].reshape(R_HEADS * c, c)).astype(BF16)
    outs = []
    for h in range(R_HEADS):
        vh = v_ref[:, vs[h]]
        s_old = state[h]
        outs.append(_dot(ab[h * c:(h + 1) * c], vh) + _dot(qdb[:, ks[h]], s_old.astype(BF16)))
        state[h] = s_old * c_decay[h] + _dot_tn(kdb[:, ks[h]], vh)
    y = _head_ln(jnp.concatenate(outs, axis=0), 1.0)
    gate = g_ref[...].astype(F32) * ng_ref[...]
    for h in range(R_HEADS):
        o_ref[:, vs[h]] = (y[h * c:(h + 1) * c] * gate[:, vs[h]]).astype(o_ref.dtype)

    @pl.when(n == pl.num_programs(1) - 1)
    def _():
        s_ref[0] = state[...]


def _mixer_prompt_kernel(sink_ref, q_ref, k_ref, v_ref, qg_ref, kg_ref,
                         rq_ref, rk_ref, rv_ref, rg_ref, cos_ref, sin_ref, intra_ref, qd_ref, kd_ref, ng_ref,
                         oa_ref, nk_ref, nv_ref, ob_ref, s_ref, kprev, vprev, state, *, c_decay):
    _attn_prompt_kernel(sink_ref, q_ref, k_ref, v_ref, qg_ref, kg_ref, oa_ref, nk_ref, nv_ref, kprev, vprev)
    _ret_prompt_kernel(rq_ref, rk_ref, rv_ref, rg_ref, cos_ref, sin_ref, intra_ref, qd_ref, kd_ref, ng_ref,
                       ob_ref, s_ref, state, c_decay=c_decay)


def _mixer_prompt_call(z, sinks, qg, kg, ret_norm_g, batch, seq):
    c = WINDOW
    assert c == R_CHUNK
    nb = seq // c
    kvw = A_KV_HEADS * HEAD_DIM
    qkw = R_HEADS * R_DK
    intra, qd, kd, c_decay, cos, sin = _ret_tables(c, 0, seq)
    row = lambda b, n, s: b * nb + n
    zcol = lambda width, off: pl.BlockSpec((c, width), lambda b, n, s: (row(b, n, s), off // width))
    const = lambda shape: pl.BlockSpec(shape, lambda b, n, s: (0,) * len(shape))
    pos = pl.BlockSpec((c, qkw), lambda b, n, s: (n, 0))
    grid_spec = pltpu.PrefetchScalarGridSpec(
        num_scalar_prefetch=1,
        grid=(batch, nb),
        in_specs=[
            zcol(D_MODEL, Z_AQ), zcol(kvw, Z_AK), zcol(kvw, Z_AV), const((1, 2 * HEAD_DIM)), const((1, 2 * HEAD_DIM)),
            zcol(qkw, Z_RQ), zcol(qkw, Z_RK), zcol(D_MODEL, Z_RV), zcol(D_MODEL, Z_RG), pos, pos,
            const((R_HEADS, c, c)), const((c, qkw)), const((c, qkw)), const((1, D_MODEL)),
        ],
        out_specs=[
            pl.BlockSpec((c, D_MODEL), lambda b, n, s: (row(b, n, s), 0)),
            pl.BlockSpec((1, c, kvw), lambda b, n, s: (b, 0, 0)),
            pl.BlockSpec((1, c, kvw), lambda b, n, s: (b, 0, 0)),
            pl.BlockSpec((c, D_MODEL), lambda b, n, s: (row(b, n, s), 0)),
            pl.BlockSpec((1, R_HEADS, R_DK, R_DV), lambda b, n, s: (b, 0, 0, 0)),
        ],
        scratch_shapes=[pltpu.VMEM((c, kvw), BF16), pltpu.VMEM((c, kvw), BF16),
                        pltpu.VMEM((R_HEADS, R_DK, R_DV), F32)],
    )
    pair_gain = lambda g: jnp.tile(g, 2).reshape(1, 2 * HEAD_DIM)
    return pl.pallas_call(
        functools.partial(_mixer_prompt_kernel, c_decay=c_decay),
        grid_spec=grid_spec,
        out_shape=[
            jax.ShapeDtypeStruct((batch * seq, D_MODEL), BF16),
            jax.ShapeDtypeStruct((batch, c, kvw), F32),
            jax.ShapeDtypeStruct((batch, c, kvw), F32),
            jax.ShapeDtypeStruct((batch * seq, D_MODEL), BF16),
            jax.ShapeDtypeStruct((batch, R_HEADS, R_DK, R_DV), F32),
        ],
        compiler_params=_cparams(("arbitrary", "arbitrary")),
        name="mixer_prompt",
    )(sinks, z, z, z, pair_gain(qg), pair_gain(kg), z, z, z, z, cos, sin, intra, qd, kd,
      ret_norm_g.reshape(1, D_MODEL))


RET_S_SEQS = 8


def _ret_sample_kernel(q_ref, k_ref, v_ref, g_ref, s0_ref, cos_ref, sin_ref, intra_ref, qd_ref, kd_ref, ng_ref,
                       *rest, c_decay, l):
    o_ref, s_ref = rest[-2:]
    sb = s0_ref.shape[1]
    rows = sb * l
    cos = cos_ref[...]
    sin = sin_ref[...]
    q = _rope(q_ref[...].astype(F32), cos, sin)
    k = _rope(k_ref[...].astype(F32), cos, sin) * (R_DK ** -0.5)
    qb = q.astype(BF16)
    kb = k.astype(BF16)
    qdb = (q * qd_ref[...]).astype(BF16)
    kdb = (k * kd_ref[...]).astype(BF16)
    for h in range(R_HEADS):
        ks = slice(h * R_DK, (h + 1) * R_DK)
        vs = slice(h * R_DV, (h + 1) * R_DV)
        vh = v_ref[:, vs]
        a = _dot_nt(qb[:, ks], kb[:, ks]) * intra_ref[h]
        o = _dot(a.astype(BF16), vh)
        cross, new_s = [], []
        for b in range(sb):
            rs = slice(b * l, (b + 1) * l)
            s_old = s0_ref[0, b, h]
            cross.append(_dot(qdb[rs, ks], s_old.astype(BF16)))
            s_ref[0, b, h] = s_old * c_decay[h] + _dot_tn(kdb[rs, ks], vh[rs])
        o = o + jnp.concatenate(cross, axis=0)
        y = _head_ln(o, ng_ref[:, vs]) * g_ref[:, vs].astype(F32)
        o_ref[:, vs] = y.astype(o_ref.dtype)


def _ret_sample_call(z, s0, ret_norm_g, batch, l, layer, stack):
    c = math.gcd(l, R_CHUNK)
    assert c == l, "sample step expects a single retention chunk"
    sb = RET_S_SEQS
    rows = sb * l
    intra, qd, kd, c_decay, cos, sin = _ret_tables(c, PAST_LEN, l)
    eye = jnp.eye(sb, dtype=F32)
    intra_bd = jnp.einsum("ab,hij->haibj", eye, intra).reshape(R_HEADS, rows, rows)
    tile = lambda t: jnp.tile(t, (sb, 1))
    qkw = R_HEADS * R_DK
    const2 = lambda i: (0, 0)
    return pl.pallas_call(
        functools.partial(_ret_sample_kernel, c_decay=c_decay, l=l),
        grid=(batch // sb,),
        in_specs=[
            pl.BlockSpec((rows, qkw), lambda i: (i, Z_RQ // qkw)),
            pl.BlockSpec((rows, qkw), lambda i: (i, Z_RK // qkw)),
            pl.BlockSpec((rows, D_MODEL), lambda i: (i, Z_RV // D_MODEL)),
            pl.BlockSpec((rows, D_MODEL), lambda i: (i, Z_RG // D_MODEL)),
            pl.BlockSpec((1, sb, R_HEADS, R_DK, R_DV), lambda i: (layer, i, 0, 0, 0)),
            pl.BlockSpec((rows, qkw), const2),
            pl.BlockSpec((rows, qkw), const2),
            pl.BlockSpec((R_HEADS, rows, rows), lambda i: (0, 0, 0)),
            pl.BlockSpec((rows, qkw), const2),
            pl.BlockSpec((rows, qkw), const2),
            pl.BlockSpec((1, D_MODEL), const2),
            pl.BlockSpec(memory_space=pl.ANY),
        ],
        out_specs=[
            pl.BlockSpec((rows, D_MODEL), lambda i: (i, 0)),
            pl.BlockSpec((1, sb, R_HEADS, R_DK, R_DV), lambda i: (layer, i, 0, 0, 0)),
        ],
        out_shape=[
            jax.ShapeDtypeStruct((batch * l, D_MODEL), BF16),
            jax.ShapeDtypeStruct(s0.shape, F32),
        ],
        input_output_aliases={11: 1},
        compiler_params=_cparams(("arbitrary",)),
        name="ret_sample",
    )(z, z, z, z, s0, tile(cos), tile(sin), intra_bd, tile(qd), tile(kd), ret_norm_g.reshape(1, D_MODEL), stack)


def _route_rows(logits, carry):
    tm = logits.shape[0]
    lane = lax.broadcasted_iota(jnp.int32, logits.shape, 1).astype(F32)
    work = logits
    sel = jnp.zeros(logits.shape, F32)
    vals, idxs = [], []
    for _ in range(TOP_K):
        m = jnp.max(work, axis=-1, keepdims=True)
        idx = jnp.min(jnp.where(work == m, lane, float(ROUTER_PAD)), axis=-1, keepdims=True)
        hit = lane == idx
        vals.append(m)
        idxs.append(idx)
        sel = jnp.where(hit, 1.0, sel)
        work = jnp.where(hit, -3e38, work)
    ex = [jnp.exp(v - vals[0]) for v in vals]
    den = ex[0] + ex[1] + ex[2] + ex[3]
    r = lax.broadcasted_iota(jnp.int32, (tm, tm), 0)
    c = lax.broadcasted_iota(jnp.int32, (tm, tm), 1)
    before = jnp.where(c < r, 1.0, 0.0).astype(BF16)
    rank = _dot(before, sel.astype(BF16)) + carry
    route = jnp.zeros(logits.shape, F32)
    for k in range(TOP_K):
        route = jnp.where(lane == float(k), ex[k] / den, route)
        route = jnp.where(lane == float(TOP_K + k), idxs[k], route)
        rk = jnp.sum(jnp.where(lane == idxs[k], rank, 0.0), axis=-1, keepdims=True)
        route = jnp.where(lane == float(2 * TOP_K + k), rk, route)
    return route, carry + jnp.sum(sel, axis=0, keepdims=True)


def _merge_kernel(*refs, emit_cv):
    (x_ref, oa_ref, ob_ref, cu_ref, cv_ref, ga_ref, gb_ref, gc_ref, g1_ref, sh2_ref, sc2_ref,
     lnf_ref, mix_ref, mixb_ref, wout_ref, rwh_ref, rwl_ref, rb_ref, cnt_ref) = refs[:19]
    n_out = 5 if emit_cv else 4
    outs = refs[-(n_out + 1):]
    if emit_cv:
        xo_ref, h2_ref, route_ref, cnto_ref, cvo_ref, carry = outs
    else:
        xo_ref, h2_ref, route_ref, cnto_ref, carry = outs
    first = (pl.program_id(0) == 0) & (pl.program_id(1) == 0)

    @pl.when(first)
    def _():
        carry[...] = cnt_ref[...]

    x = x_ref[...]
    gbk, rb, d = x.shape
    tm = gbk * rb
    cw = C_CHUNK
    gw = d // C_GROUPS

    cvb = cv_ref[...]
    if emit_cv:
        cvo_ref[...] = cvb.astype(F32)
    mixed_rows = []
    for c in range(tm // cw):
        rs = slice(c * cw, (c + 1) * cw)
        cols = [_dot(mix_ref[g], cvb[rs, g * gw:(g + 1) * gw]) for g in range(C_GROUPS)]
        mixed_rows.append(jnp.concatenate(cols, axis=-1) + mixb_ref[...])
    mixed = jnp.concatenate(mixed_rows, axis=0)
    oc = cu_ref[...].astype(F32) * mixed
    merged = (ga_ref[...].astype(F32) * oa_ref[...].astype(F32) + gb_ref[...].astype(F32) * ob_ref[...].astype(F32)
              + gc_ref[...].astype(F32) * oc)
    y = _dot(merged.astype(BF16), wout_ref[...])
    x = x + g1_ref[...] * y.reshape(gbk, rb, d)
    xo_ref[...] = x

    h2 = (_rms(x) * lnf_ref[...] * (1.0 + sc2_ref[...]) + sh2_ref[...]).reshape(tm, d)
    h2b = h2.astype(BF16)
    h2_ref[0], h2_ref[1] = _pack_rows(h2)
    h2l = (h2 - h2b.astype(F32)).astype(BF16)
    rwh = rwh_ref[...]
    logits = _dot(h2b, rwh) + _dot(h2l, rwh) + _dot(h2b, rwl_ref[...]) + rb_ref[...]
    route, counts = _route_rows(logits, carry[...])
    route_ref[...] = route
    carry[...] = counts
    cnto_ref[...] = counts


def _merge_call(x, z, oa, ob, mod, ln_ffn_g, mix, mixb, w_out_bf, rw_hi, rw_lo, rb,
                counts, t_all, tile_off, shared, emit_cv):
    g, r, d = x.shape
    gb, rb_ = _group_blocks(g, r)
    tm = gb * rb_
    nj = r // rb_
    t = g * r
    xspec = pl.BlockSpec((gb, rb_, d), lambda i, j: (i, j, 0))
    rows = lambda col: pl.BlockSpec((tm, d), lambda i, j: (i * nj + j, col))
    mspec = lambda col: pl.BlockSpec((gb, 1, d), lambda i, j: (i, 0, col))
    const = lambda shape: pl.BlockSpec(shape, lambda i, j: (0,) * len(shape))
    in_specs = [
        xspec, rows(0), rows(0),
        rows(Z_CU // d), rows(Z_CV // d), rows(Z_GA // d), rows(Z_GB // d), rows(Z_GC // d),
        mspec(2), mspec(3), mspec(4),
        const((1, 1, d)),
        const((C_GROUPS, C_CHUNK, C_CHUNK)), const((C_CHUNK, d)),
        const((d, d)), const((d, ROUTER_PAD)), const((d, ROUTER_PAD)), const((1, ROUTER_PAD)),
        const((1, ROUTER_PAD)),
    ]
    aliases = {len(in_specs): 1, len(in_specs) + 1: 2}
    in_specs += [pl.BlockSpec(memory_space=pl.ANY), pl.BlockSpec(memory_space=pl.ANY)]
    out_specs = [xspec, pl.BlockSpec((2, tm, PACK_W), lambda i, j: (0, tile_off + i * nj + j, 0)),
                 pl.BlockSpec((tm, ROUTER_PAD), lambda i, j: (tile_off + i * nj + j, 0)), const((1, ROUTER_PAD))]
    out_shape = [jax.ShapeDtypeStruct(x.shape, F32), jax.ShapeDtypeStruct((2, t_all, PACK_W), U32),
                 jax.ShapeDtypeStruct((t_all, ROUTER_PAD), F32), jax.ShapeDtypeStruct((1, ROUTER_PAD), F32)]
    if emit_cv:
        out_specs.append(rows(0))
        out_shape.append(jax.ShapeDtypeStruct((t, d), F32))
    return pl.pallas_call(
        functools.partial(_merge_kernel, emit_cv=emit_cv),
        grid=(g // gb, nj), in_specs=in_specs, out_specs=out_specs, out_shape=out_shape,
        scratch_shapes=[pltpu.VMEM((1, ROUTER_PAD), F32)],
        input_output_aliases=aliases,
        compiler_params=_cparams(("arbitrary", "arbitrary")),
        name="merge",
    )(x, oa, ob, z, z, z, z, z, mod, mod, mod, ln_ffn_g.reshape(1, 1, d),
      mix, mixb, w_out_bf, rw_hi, rw_lo, rb, counts, *shared)


def _moe_kernel(be_ref, nx_ref, nh_ref, nu_ref, x_ref, w1_hbm, b1_ref, w2_hbm, b2_ref, o_ref,
                w1s, w2s, w1b, w2b, sem, *, layer):
    i = pl.program_id(0)
    e = be_ref[i]
    prev = be_ref[jnp.maximum(i - 1, 0)]

    def weight_copies(ex):
        return (pltpu.make_async_copy(w1_hbm.at[layer, ex], w1s, sem.at[0]),
                pltpu.make_async_copy(w2_hbm.at[layer, ex], w2s, sem.at[1]))

    @pl.when(i == 0)
    def _():
        for cp in weight_copies(e):
            cp.start()

    @pl.when((i == 0) | (e != prev))
    def _():
        for cp in weight_copies(e):
            cp.wait()
        w1b[...] = w1s[...].astype(BF16)
        w2b[...] = w2s[...].astype(BF16)
        nxt = nx_ref[i]

        @pl.when(nxt >= 0)
        def _():
            for cp in weight_copies(nxt):
                cp.start()

    def experts(rows):
        rs = slice(0, rows)
        xb = _unpack_rows(x_ref[0, rs], x_ref[1, rs]).astype(BF16)
        hdn = _dot(xb, w1b[...]) + b1_ref[0, 0]
        g = jnp.minimum(hdn[:, :D_FF], SWIGLU_LIMIT)
        up = jnp.clip(hdn[:, D_FF:], -SWIGLU_LIMIT, SWIGLU_LIMIT)
        act = (up + 1.0) * (g * jax.nn.sigmoid(g * SWIGLU_ALPHA))
        o_ref[0, rs], o_ref[1, rs] = _pack_rows(_dot(act.astype(BF16), w2b[...]) + b2_ref[0, 0])

    for nh in range(1, MOE_HALVES + 1):
        pl.when(nh_ref[i] == nh)(functools.partial(experts, nh * MOE_ROWS))


def _moe_call(xb, block_e, next_e, n_halves, n_used, w1, b1, w2, b2, layer):
    _, n_rows, _ = xb.shape
    step_rows = MOE_HALVES * MOE_ROWS
    nblk = n_rows // step_rows
    depth, ne, d, f2 = w1.shape
    last = lambda i, be, nx, nh, nu: jnp.minimum(i, nu[0] - 1)
    xspec = pl.BlockSpec((2, step_rows, PACK_W), lambda i, be, nx, nh, nu: (0, last(i, be, nx, nh, nu), 0))
    grid_spec = pltpu.PrefetchScalarGridSpec(
        num_scalar_prefetch=4,
        grid=(nblk,),
        in_specs=[
            xspec,
            pl.BlockSpec(memory_space=pl.ANY),
            pl.BlockSpec((1, 1, 1, f2), lambda i, be, nx, nh, nu: (layer, be[i], 0, 0)),
            pl.BlockSpec(memory_space=pl.ANY),
            pl.BlockSpec((1, 1, 1, d), lambda i, be, nx, nh, nu: (layer, be[i], 0, 0)),
        ],
        out_specs=xspec,
        scratch_shapes=[pltpu.VMEM((d, f2), F32), pltpu.VMEM((f2 // 2, d), F32),
                        pltpu.VMEM((d, f2), BF16), pltpu.VMEM((f2 // 2, d), BF16),
                        pltpu.SemaphoreType.DMA((2,))],
    )
    return pl.pallas_call(
        functools.partial(_moe_kernel, layer=layer),
        grid_spec=grid_spec,
        out_shape=jax.ShapeDtypeStruct(xb.shape, U32),
        compiler_params=_cparams(("arbitrary",)),
        name="moe",
    )(block_e, next_e, n_halves, n_used, xb, w1, b1.reshape(depth, ne, 1, f2), w2, b2.reshape(depth, ne, 1, d))


def _sc_mesh():
    return plsc.VectorSubcoreMesh(core_axis_name="core", subcore_axis_name="subcore")


def _sc_scatter_rows(x, idx, n_out):
    t, c = x.shape
    kk = idx.shape[0]

    @pl.kernel(out_type=jax.ShapeDtypeStruct((n_out, c), x.dtype), mesh=_sc_mesh(), scratch_types=[])
    def scatter(x_hbm, i_hbm, o_hbm):
        def body(x_vmem, i_vmem):
            for k in range(kk):
                pltpu.sync_copy(x_vmem, o_hbm.at[i_vmem.at[k]])

        pltpu.emit_pipeline(
            body,
            grid=(t // SC_WINDOW,),
            in_specs=[pl.BlockSpec((SC_WINDOW, c), index_map=lambda i: (i, 0)),
                      pl.BlockSpec((kk, SC_WINDOW), index_map=lambda i: (0, i))],
            out_specs=[],
            core_axis_name=("core", "subcore"),
            dimension_semantics=(pltpu.PARALLEL,),
        )(x_hbm, i_hbm)

    return scatter(x, idx)


def _sc_gather_rows(data, idx):
    n = idx.shape[0]
    c = data.shape[1]

    @pl.kernel(out_type=jax.ShapeDtypeStruct((n, c), data.dtype), mesh=_sc_mesh(), scratch_types=[])
    def gather(x_hbm, i_hbm, o_hbm):
        def body(i_vmem, o_vmem):
            pltpu.sync_copy(x_hbm.at[i_vmem.at[0]], o_vmem)

        pltpu.emit_pipeline(
            body,
            grid=(n // SC_WINDOW,),
            in_specs=[pl.BlockSpec((1, SC_WINDOW), index_map=lambda i: (0, i))],
            out_specs=[pl.BlockSpec((SC_WINDOW, c), index_map=lambda i: (i, 0))],
            core_axis_name=("core", "subcore"),
            dimension_semantics=(pltpu.PARALLEL,),
        )(i_hbm, o_hbm)

    return gather(data, idx.reshape(1, n))


def _plan(route, counts):
    t = route.shape[0]
    step_rows = MOE_HALVES * MOE_ROWS
    n_steps = -(-t * TOP_K // step_rows) + N_EXPERTS
    e4 = route[:, TOP_K:2 * TOP_K].astype(jnp.int32)
    r4 = route[:, 2 * TOP_K:3 * TOP_K].astype(jnp.int32)
    cnt = counts[0, :N_EXPERTS].astype(jnp.int32)
    halves = (cnt + MOE_ROWS - 1) // MOE_ROWS
    steps = (halves + MOE_HALVES - 1) // MOE_HALVES
    send = jnp.cumsum(steps)
    sstart = send - steps
    onehot = e4[:, :, None] == jnp.arange(N_EXPERTS, dtype=jnp.int32)[None, None, :]
    dest = (r4 + jnp.sum(jnp.where(onehot, (sstart * step_rows)[None, None, :], 0), axis=-1)).T
    step = jnp.arange(n_steps, dtype=jnp.int32)
    step_e = jnp.sum((step[:, None] >= send[None, :]).astype(jnp.int32), axis=-1)
    n_used = send[-1].astype(jnp.int32).reshape(1)
    step_e = jnp.minimum(step_e, jnp.sum((send < send[-1]).astype(jnp.int32))).astype(jnp.int32)
    mine = step_e[:, None] == jnp.arange(N_EXPERTS, dtype=jnp.int32)[None, :]
    pick = lambda table: jnp.sum(jnp.where(mine, table[None, :], 0), axis=-1)
    n_halves = jnp.clip(pick(halves) - MOE_HALVES * (step - pick(sstart)), 0, MOE_HALVES)
    n_halves = jnp.where(step < n_used[0], n_halves, 0).astype(jnp.int32)
    after = pick(send)
    next_e = jnp.sum((after[:, None] >= send[None, :]).astype(jnp.int32), axis=-1)
    next_e = jnp.where(after < n_used[0], next_e, -1).astype(jnp.int32)
    return dest, step_e, next_e, n_halves, n_used, n_steps * step_rows


def _moe(h2p, route, counts, w1, b1, w2, b2, layer):
    _, t, pw = h2p.shape
    dest, step_e, next_e, n_halves, n_used, n_rows = _plan(route, counts)
    idx_s = jnp.concatenate([dest, dest + n_rows], axis=1)
    xb = _sc_scatter_rows(h2p.reshape(2 * t, pw), idx_s, 2 * n_rows).reshape(2, n_rows, pw)
    yb = _moe_call(xb, step_e, next_e, n_halves, n_used, w1, b1, w2, b2, layer)
    idx_g = jnp.concatenate([dest.reshape(-1), dest.reshape(-1) + n_rows])
    y4 = _sc_gather_rows(yb.reshape(2 * n_rows, pw), idx_g)
    return y4.reshape(2, TOP_K, t, pw)


def _reorder_w_in(w):
    o = _ORIG
    seg = lambda a, n: w[:, a:a + n]
    d = D_MODEL
    q_heads = sorted(((_q_col(kv, g), kv * A_GROUP + g) for kv in range(A_KV_HEADS) for g in range(A_GROUP)))
    aq = jnp.concatenate([seg(o["aq"] + h * HEAD_DIM, HEAD_DIM) for _, h in q_heads], axis=1)
    pad = jnp.zeros((d, 256), w.dtype)
    parts = [aq, seg(o["rv"], d), seg(o["rg"], d), seg(o["cu"], d), seg(o["rq"], 512), seg(o["ak"], 256), pad,
             seg(o["cv"], d), seg(o["mg"], d), seg(o["mg"] + d, d), seg(o["mg"] + 2 * d, d),
             seg(o["rk"], 512), seg(o["av"], 256), pad]
    w = jnp.concatenate(parts, axis=1).astype(BF16)
    return w.reshape(D_MODEL, IN_CHUNKS, D_IN // IN_CHUNKS).transpose(1, 0, 2)


def _gmlp_tables(ws, bs, chunk_len, rows):
    causal = jnp.tril(jnp.ones((C_CHUNK, C_CHUNK), dtype=bool))
    w = jnp.where(causal[None], ws, 0.0)[:, :chunk_len, :chunk_len]
    reps = rows // chunk_len
    eye = jnp.eye(reps, dtype=F32)
    mix = jnp.einsum("ab,gts->gatbs", eye, w).reshape(C_GROUPS, rows, rows).astype(BF16)
    b = jnp.tile(bs[:, :chunk_len].T, (reps, 1))
    mixb = jnp.repeat(b, D_MODEL // C_GROUPS, axis=1)
    return mix, mixb


def kernel(x_prompt, x_sample, c_prompt, c_sample, cache_attn_k, cache_attn_v, state_retention, ln_mix_g, ln_ffn_g, w_ada, b_ada, w_in, b_gate, q_norm_g, k_norm_g, attn_sinks, ret_norm_g, gm_ln_g, gm_ln_b, gm_ws, gm_bs, w_out, router_w, router_b, moe_w1, moe_b1, moe_w2, moe_b2):
    bp, lp, d = x_prompt.shape
    bs, ls, _ = x_sample.shape
    tp, ts = bp * lp, bs * ls
    kvw = A_KV_HEADS * HEAD_DIM
    wb = cache_attn_k.shape[2]

    mod_all = _ada_call(jnp.concatenate([c_prompt, c_sample], axis=0), w_ada, b_ada)

    cache_k = cache_attn_k.reshape(DEPTH, bs, wb, kvw)
    cache_v = cache_attn_v.reshape(DEPTH, bs, wb, kvw)
    xp, xs = x_prompt, x_sample
    y4 = None
    mod_p = mod_s = None
    s_all = jnp.zeros(state_retention.shape, F32)
    h2 = jnp.zeros((2, tp + ts, PACK_W), U32)
    route = jnp.zeros((tp + ts, ROUTER_PAD), F32)
    pk, pv, ps, sk, sv, sg = [], [], [], [], [], []
    for l in range(DEPTH):
        prev_mod_p, prev_mod_s = mod_p, mod_s
        mod_p = mod_all[l, :bp].reshape(bp, 1, 6 * d)
        mod_s = mod_all[l, bp:].reshape(bs, 1, 6 * d)
        w_in_bf = _reorder_w_in(w_in[l])
        w_out_bf = w_out[l].astype(BF16)
        rw = jnp.pad(router_w[l], ((0, 0), (0, ROUTER_PAD - N_EXPERTS)))
        rw_hi, rw_lo = _split_bf16(rw)
        rb = jnp.pad(router_b[l], (0, ROUTER_PAD - N_EXPERTS), constant_values=NEG_BIG).reshape(1, ROUTER_PAD)

        proj = (ln_mix_g[l], w_in_bf, b_gate[l], gm_ln_g[l], gm_ln_b[l])
        if l == 0:
            zp = _inproj_call(xp, mod_p, *proj)
            zs = _inproj_call(xs, mod_s, *proj)
        else:
            zp, xp = _inproj_call(xp, mod_p, *proj, moe_out=(y4, route, prev_mod_p, 0))
            zs, xs = _inproj_call(xs, mod_s, *proj, moe_out=(y4, route, prev_mod_s, tp // ROW_TILE))
        oa_p, k_p, v_p, ob_p, s_p = _mixer_prompt_call(zp, attn_sinks[l], q_norm_g[l], k_norm_g[l], ret_norm_g[l],
                                                       bp, lp)
        oa_s, k_s, v_s = _attn_sample_call(zs, cache_k, cache_v, attn_sinks[l], q_norm_g[l], k_norm_g[l], bs, ls, l)
        ob_s, s_all = _ret_sample_call(zs, state_retention, ret_norm_g[l], bs, ls, l, stack=s_all)

        mix_p, mixb_p = _gmlp_tables(gm_ws[l], gm_bs[l], C_CHUNK, C_CHUNK)
        mix_s, mixb_s = _gmlp_tables(gm_ws[l], gm_bs[l], ls, C_CHUNK)
        common = (ln_ffn_g[l],)
        tail = (w_out_bf, rw_hi, rw_lo, rb)
        zero_counts = jnp.zeros((1, ROUTER_PAD), F32)
        xp, h2, route, cnt_p = _merge_call(xp, zp, oa_p, ob_p, mod_p, *common, mix_p, mixb_p, *tail,
                                           zero_counts, tp + ts, 0, (h2, route), emit_cv=False)
        xs, h2, route, cnt, cv_s = _merge_call(xs, zs, oa_s, ob_s, mod_s, *common, mix_s, mixb_s, *tail,
                                               cnt_p, tp + ts, tp // ROW_TILE, (h2, route), emit_cv=True)
        y4 = _moe(h2, route, cnt, moe_w1, moe_b1, moe_w2, moe_b2, l)

        pk.append(k_p.reshape(bp, WINDOW, A_KV_HEADS, HEAD_DIM))
        pv.append(v_p.reshape(bp, WINDOW, A_KV_HEADS, HEAD_DIM))
        ps.append(s_p)
        sk.append(k_s.reshape(bs, ls, A_KV_HEADS, HEAD_DIM))
        sv.append(v_s.reshape(bs, ls, A_KV_HEADS, HEAD_DIM))
        sg.append(cv_s.reshape(bs, ls, d))

    xp = _resid_call(xp, y4, route, mod_p, 0)
    xs = _resid_call(xs, y4, route, mod_s, tp // ROW_TILE)
    return (xp, xs, jnp.stack(pk), jnp.stack(pv), jnp.stack(ps), jnp.stack(sk), jnp.stack(sv),
            s_all, jnp.stack(sg))
```

```python
import functools
import math

import numpy as np
import jax
import jax.numpy as jnp
from jax import lax
from jax.experimental import pallas as pl
from jax.experimental.pallas import tpu as pltpu
from jax.experimental.pallas import tpu_sc as plsc

F32 = jnp.float32
BF16 = jnp.bfloat16
U32 = jnp.uint32

D_MODEL = 1024
DEPTH = 4
PAST_LEN = 8192
HEAD_DIM = 64
A_Q_HEADS = 16
A_KV_HEADS = 4
A_GROUP = 4
WINDOW = 128
R_HEADS = 8
R_DK = 64
R_DV = 128
R_CHUNK = 128
ROPE_BASE = 10000.0
C_CHUNK = 128
C_GROUPS = 8
N_EXPERTS = 32
TOP_K = 4
D_FF = D_MODEL
SWIGLU_LIMIT = 7.0
SWIGLU_ALPHA = 1.702
EPS = 1e-6

Z_AQ, Z_RV, Z_RG, Z_CU, Z_RQ, Z_AK = 0, 1024, 2048, 3072, 4096, 4608
Z_CV, Z_GA, Z_GB, Z_GC, Z_RK, Z_AV = 5120, 6144, 7168, 8192, 9216, 9728
D_IN = 10240
IN_CHUNKS = 2
_ORIG = dict(aq=0, ak=1024, av=1280, rq=1536, rk=2048, rv=2560, rg=3584, cu=4608, cv=5632, mg=6656)

ROW_TILE = 512
MOE_ROWS = 256
MOE_HALVES = 4
ROUTER_PAD = 128
PACK_W = D_MODEL // 4
SC_WINDOW = 128
HI_MASK = 0xFFFF0000
VMEM_LIMIT = 56 * 1024 * 1024
NEG_BIG = -1e30


def _cparams(sem):
    return pltpu.CompilerParams(dimension_semantics=sem, vmem_limit_bytes=VMEM_LIMIT)


def _split_bf16(x):
    hi = x.astype(BF16)
    lo = (x - hi.astype(F32)).astype(BF16)
    return hi, lo


def _dot(a, b):
    return jnp.dot(a, b, preferred_element_type=F32)


def _dot_nt(a, b):
    return lax.dot_general(a, b, (((1,), (1,)), ((), ())), preferred_element_type=F32)


def _dot_tn(a, b):
    return lax.dot_general(a, b, (((0,), (0,)), ((), ())), preferred_element_type=F32)


def _ada_kernel(c_ref, w_ref, b_ref, o_ref):
    c = c_ref[...]
    s_hi, s_lo = _split_bf16(c * jax.nn.sigmoid(c))
    w_hi, w_lo = _split_bf16(w_ref[0])
    acc = _dot(s_hi, w_hi) + _dot(s_lo, w_hi) + _dot(s_hi, w_lo)
    o_ref[0] = acc + b_ref[0]


def _ada_call(c_all, w_ada, b_ada):
    depth, d, n = w_ada.shape
    m = c_all.shape[0]
    tn = 1024
    return pl.pallas_call(
        _ada_kernel,
        grid=(depth, n // tn),
        in_specs=[
            pl.BlockSpec((m, d), lambda l, j: (0, 0)),
            pl.BlockSpec((1, d, tn), lambda l, j: (l, 0, j)),
            pl.BlockSpec((1, 1, tn), lambda l, j: (l, 0, j)),
        ],
        out_specs=pl.BlockSpec((1, m, tn), lambda l, j: (l, 0, j)),
        out_shape=jax.ShapeDtypeStruct((depth, m, n), F32),
        compiler_params=_cparams(("arbitrary", "arbitrary")),
        name="ada",
    )(c_all, w_ada, b_ada.reshape(depth, 1, n))


def _pack_rows(y):
    bits = pltpu.bitcast(y.astype(BF16).astype(F32), U32)
    q = [bits[:, i * PACK_W:(i + 1) * PACK_W] for i in range(4)]
    return (q[0] >> 16) | q[1], (q[2] >> 16) | q[3]


def _unpack_rows(a, b):
    f = lambda w: pltpu.bitcast(w, F32)
    return jnp.concatenate([f(a << 16), f(a & jnp.uint32(HI_MASK)), f(b << 16), f(b & jnp.uint32(HI_MASK))], axis=-1)


def _combine(y4_ref, route_ref):
    route = route_ref[...]
    acc = None
    for k in range(TOP_K):
        term = route[:, k:k + 1] * _unpack_rows(y4_ref[0, k], y4_ref[1, k])
        acc = term if acc is None else acc + term
    return acc


def _rms(x):
    return x * lax.rsqrt(jnp.mean(x * x, axis=-1, keepdims=True) + EPS)


def _gelu(x):
    return jax.nn.gelu(x, approximate=True)


def _silu(x):
    return x * jax.nn.sigmoid(x)


def _inproj_kernel(*refs, has_resid):
    if has_resid:
        (x_ref, y4_ref, route_ref, g2_ref, sh_ref, sc_ref, lng_ref, w_ref, bg_ref, gmg_ref, gmb_ref,
         z_ref, xo_ref, hb) = refs
    else:
        x_ref, sh_ref, sc_ref, lng_ref, w_ref, bg_ref, gmg_ref, gmb_ref, z_ref, hb = refs
    c = pl.program_id(2)

    def gm_v(acc):
        v = _gelu(acc)
        vc = v - jnp.mean(v, axis=-1, keepdims=True)
        var = jnp.mean(vc * vc, axis=-1, keepdims=True)
        return vc * lax.rsqrt(var + EPS) * gmg_ref[...] + gmb_ref[...]

    merge_gate = lambda k: (lambda acc: jax.nn.sigmoid(acc + bg_ref[k:k + 1]))
    keep = lambda acc: acc
    segment_fns = ((keep, keep, _silu, _gelu, keep),
                   (gm_v, merge_gate(0), merge_gate(1), merge_gate(2), keep))

    @pl.when(c == 0)
    def _():
        x = x_ref[...]
        if has_resid:
            x = x + g2_ref[...] * _combine(y4_ref, route_ref).reshape(x.shape)
            xo_ref[...] = x
        h = _rms(x) * lng_ref[...] * (1.0 + sc_ref[...]) + sh_ref[...]
        hb[...] = h.reshape(hb.shape).astype(BF16)

    def project(half):
        h = hb[...]
        for s, fn in enumerate(segment_fns[half]):
            cols = slice(s * D_MODEL, (s + 1) * D_MODEL)
            z_ref[:, cols] = fn(_dot(h, w_ref[half, :, cols])).astype(BF16)

    for half in range(IN_CHUNKS):
        pl.when(c == half)(functools.partial(project, half))


def _group_blocks(g, r):
    if r >= ROW_TILE:
        return 1, ROW_TILE
    return ROW_TILE // r, r


def _moe_out_specs(tm, nj, tile_off):
    return [pl.BlockSpec((2, TOP_K, tm, PACK_W), lambda i, j, *_: (0, 0, tile_off + i * nj + j, 0)),
            pl.BlockSpec((tm, ROUTER_PAD), lambda i, j, *_: (tile_off + i * nj + j, 0))]


def _inproj_call(x, mod, ln_g, w_bf, b_gate, gm_ln_g, gm_ln_b, moe_out=None):
    g, r, d = x.shape
    gb, rb = _group_blocks(g, r)
    nj = r // rb
    nc, _, cw = w_bf.shape
    grid = (g // gb, nj, nc)
    xspec = pl.BlockSpec((gb, rb, d), lambda i, j, c: (i, j, 0))
    mspec = lambda col: pl.BlockSpec((gb, 1, d), lambda i, j, c: (i, 0, col))
    in_specs, args = [xspec], [x]
    if moe_out is not None:
        y4, route, mod_prev, tile_off = moe_out
        in_specs += _moe_out_specs(gb * rb, nj, tile_off) + [mspec(5)]
        args += [y4, route, mod_prev]
    in_specs += [mspec(0), mspec(1), pl.BlockSpec((1, 1, d), lambda i, j, c: (0, 0, 0)),
                 pl.BlockSpec((nc, d, cw), lambda i, j, c: (0, 0, 0), pipeline_mode=pl.Buffered(1)),
                 pl.BlockSpec((3, d), lambda i, j, c: (0, 0)),
                 pl.BlockSpec((1, d), lambda i, j, c: (0, 0)),
                 pl.BlockSpec((1, d), lambda i, j, c: (0, 0))]
    args += [mod, mod, ln_g.reshape(1, 1, d), w_bf, b_gate.reshape(3, d), gm_ln_g.reshape(1, d),
             gm_ln_b.reshape(1, d)]
    zspec = pl.BlockSpec((gb * rb, cw), lambda i, j, c: (i * nj + j, c))
    zshape = jax.ShapeDtypeStruct((g * r, nc * cw), BF16)
    if moe_out is not None:
        out_specs, out_shape = [zspec, xspec], [zshape, jax.ShapeDtypeStruct(x.shape, F32)]
    else:
        out_specs, out_shape = zspec, zshape
    return pl.pallas_call(
        functools.partial(_inproj_kernel, has_resid=moe_out is not None),
        grid=grid, in_specs=in_specs, out_specs=out_specs, out_shape=out_shape,
        scratch_shapes=[pltpu.VMEM((gb * rb, d), BF16)],
        compiler_params=_cparams(("arbitrary", "arbitrary", "arbitrary")),
        name="inproj",
    )(*args)


def _resid_kernel(x_ref, y4_ref, route_ref, g2_ref, o_ref):
    x = x_ref[...]
    o_ref[...] = x + g2_ref[...] * _combine(y4_ref, route_ref).reshape(x.shape)


def _resid_call(x, y4, route, mod, tile_off):
    g, r, d = x.shape
    gb, rb = _group_blocks(g, r)
    nj = r // rb
    xspec = pl.BlockSpec((gb, rb, d), lambda i, j: (i, j, 0))
    return pl.pallas_call(
        _resid_kernel,
        grid=(g // gb, nj),
        in_specs=[xspec] + _moe_out_specs(gb * rb, nj, tile_off)
        + [pl.BlockSpec((gb, 1, d), lambda i, j: (i, 0, 5))],
        out_specs=xspec,
        out_shape=jax.ShapeDtypeStruct(x.shape, F32),
        compiler_params=_cparams(("arbitrary", "arbitrary")),
        name="resid",
    )(x, y4, route, mod)


def _q_col(kvh, g):
    p, kv_odd = divmod(kvh, 2)
    j, g_odd = divmod(g, 2)
    tile = p * 4 + (kv_odd ^ g_odd) * 2 + j
    return tile * 2 * HEAD_DIM + g_odd * HEAD_DIM


def _half_mats():
    r = lax.broadcasted_iota(jnp.int32, (2 * HEAD_DIM, 2 * HEAD_DIM), 0)
    c = lax.broadcasted_iota(jnp.int32, (2 * HEAD_DIM, 2 * HEAD_DIM), 1)
    seg = jnp.where(r // HEAD_DIM == c // HEAD_DIM, 1.0, 0.0).astype(BF16)
    swap = jnp.where((r + HEAD_DIM) % (2 * HEAD_DIM) == c, 1.0, 0.0).astype(BF16)
    return seg, swap


def _pair_rms(x, g2, seg):
    outs = []
    for t in range(x.shape[1] // (2 * HEAD_DIM)):
        xt = x[:, t * 2 * HEAD_DIM:(t + 1) * 2 * HEAD_DIM]
        hi, lo = _split_bf16(xt * xt)
        ss = _dot(hi, seg) + _dot(lo, seg)
        outs.append(xt * lax.rsqrt(ss * (1.0 / HEAD_DIM) + EPS) * g2)
    return jnp.concatenate(outs, axis=-1)


def _pair_tile_attention(sink_ref, qn, kall, vall, mask, w, o_ref):
    _, swap = _half_mats()
    tw = 2 * HEAD_DIM
    nk = kall.shape[0]
    row = lax.broadcasted_iota(jnp.int32, (2 * w, nk), 0)
    col = lax.broadcasted_iota(jnp.int32, (2 * w, nk), 1)
    sink_top = (col == 0) & (row < w)
    sink_bot = (col == 0) & (row >= w)
    lane_kv = lax.broadcasted_iota(jnp.int32, (nk, tw), 1)
    key_kv = lax.broadcasted_iota(jnp.int32, (nk, tw), 0)
    first_o = lax.broadcasted_iota(jnp.int32, (w, tw), 1) < HEAD_DIM
    scores, values = [], []
    for p in range(A_KV_HEADS // 2):
        kp = kall[:, p * tw:(p + 1) * tw]
        vp = vall[:, p * tw:(p + 1) * tw]
        kv_tiles = ((kp, vp), (_dot(kp, swap).astype(BF16), _dot(vp, swap).astype(BF16)))
        for variant, (kk, vv) in enumerate(kv_tiles):
            t0 = p * 4 + variant * 2
            qs = jnp.concatenate([qn[:, t0 * tw:(t0 + 1) * tw], qn[:, (t0 + 1) * tw:(t0 + 2) * tw]], axis=0)
            for half in range(2):
                keep = (lane_kv < HEAD_DIM) if half == 0 else (lane_kv >= HEAD_DIM)
                kh = jnp.where(keep, kk, jnp.zeros_like(kk))
                values.append(jnp.where(keep & (key_kv > 0), vv, jnp.zeros_like(vv)))
                kvh = 2 * p + (half ^ variant)
                s = jnp.where(mask, _dot_nt(qs, kh), NEG_BIG)
                s = jnp.where(sink_top, sink_ref[kvh * A_GROUP + half], s)
                scores.append(jnp.where(sink_bot, sink_ref[kvh * A_GROUP + 2 + half], s))
    s_all = jnp.concatenate(scores, axis=0)
    p_all = jnp.exp(s_all - jnp.max(s_all, axis=-1, keepdims=True))
    inv = 1.0 / jnp.sum(p_all, axis=-1, keepdims=True)
    p_all = p_all.astype(BF16)
    outs = [_dot(p_all[c * 2 * w:(c + 1) * 2 * w], values[c]) * inv[c * 2 * w:(c + 1) * 2 * w]
            for c in range(len(values))]
    for p in range(A_KV_HEADS // 2):
        o_a = outs[4 * p] + outs[4 * p + 1]
        o_b = outs[4 * p + 2] + outs[4 * p + 3]
        for jr in range(2):
            a = o_a[jr * w:(jr + 1) * w]
            b = o_b[jr * w:(jr + 1) * w]
            c_even = (2 * p * A_GROUP + 2 * jr) * HEAD_DIM
            c_odd = ((2 * p + 1) * A_GROUP + 2 * jr) * HEAD_DIM
            o_ref[:, c_even:c_even + tw] = jnp.where(first_o, a, b).astype(o_ref.dtype)
            o_ref[:, c_odd:c_odd + tw] = jnp.where(first_o, b, a).astype(o_ref.dtype)


def _when(cond, fn):
    if cond is not None:
        pl.when(cond)(fn)


def _attn_prompt_block(sink_ref, q_ref, k_ref, v_ref, qg_ref, kg_ref, o_ref, nk_ref, nv_ref, kprev, vprev,
                       first, last):
    w = WINDOW

    def reset():
        kprev[...] = jnp.zeros_like(kprev)
        vprev[...] = jnp.zeros_like(vprev)

    _when(first, reset)

    seg, _ = _half_mats()
    v = v_ref[...]
    qn = (_pair_rms(q_ref[...].astype(F32), qg_ref[...], seg) * HEAD_DIM ** -0.5).astype(BF16)
    kn = _pair_rms(k_ref[...].astype(F32), kg_ref[...], seg)
    knb = kn.astype(BF16)
    kcat = jnp.concatenate([kprev[...], knb], axis=0)
    vcat = jnp.concatenate([vprev[...], v], axis=0)

    i = lax.broadcasted_iota(jnp.int32, (2 * w, 2 * w), 0) % w
    j = lax.broadcasted_iota(jnp.int32, (2 * w, 2 * w), 1)
    lo = -1 if first is None else jnp.where(first, w - 1, -1)
    mask = (j > i) & (j <= i + w) & (j > lo)
    _pair_tile_attention(sink_ref, qn, kcat, vcat, mask, w, o_ref)

    kprev[...] = knb
    vprev[...] = v

    def emit():
        nk_ref[0] = kn
        nv_ref[0] = v.astype(F32)

    _when(last, emit)


ATTN_S_SEQS = 8


def _attn_sample_kernel(sink_ref, q_ref, k_ref, v_ref, kc_ref, vc_ref, qg_ref, kg_ref, o_ref, nk_ref, nv_ref):
    _, sb, wb, kvw = kc_ref.shape
    rows = q_ref.shape[0]
    l = rows // sb
    seg, _ = _half_mats()
    v = v_ref[...]
    qn = (_pair_rms(q_ref[...].astype(F32), qg_ref[...], seg) * HEAD_DIM ** -0.5).astype(BF16)
    kn = _pair_rms(k_ref[...].astype(F32), kg_ref[...], seg)
    nk_ref[...] = kn
    nv_ref[...] = v.astype(F32)
    nc = sb * wb
    kall = jnp.concatenate([kc_ref[0].reshape(nc, kvw).astype(BF16), kn.astype(BF16)], axis=0)
    vall = jnp.concatenate([vc_ref[0].reshape(nc, kvw).astype(BF16), v], axis=0)

    r = lax.broadcasted_iota(jnp.int32, (2 * rows, nc + rows), 0) % rows
    c = lax.broadcasted_iota(jnp.int32, (2 * rows, nc + rows), 1)
    cached = c < nc
    key_seq = jnp.where(cached, c // wb, (c - nc) // l)
    i = r % l
    seen = jnp.where(cached, c % wb - (wb - WINDOW), i + 1) > jnp.where(cached, i, (c - nc) % l)
    mask = (key_seq == r // l) & seen
    _pair_tile_attention(sink_ref, qn, kall, vall, mask, rows, o_ref)


def _attn_sample_call(z, kc, vc, sinks, qg, kg, batch, l, layer):
    kvw = A_KV_HEADS * HEAD_DIM
    sb = ATTN_S_SEQS
    rows = sb * l
    wb = kc.shape[2]
    assert wb == WINDOW, "key 0 of the window buffer must be out of every new token's window"
    pair_gain = lambda g: jnp.tile(g, 2).reshape(1, 2 * HEAD_DIM)
    grid_spec = pltpu.PrefetchScalarGridSpec(
        num_scalar_prefetch=1,
        grid=(batch // sb,),
        in_specs=[
            pl.BlockSpec((rows, D_MODEL), lambda i, s: (i, Z_AQ // D_MODEL)),
            pl.BlockSpec((rows, kvw), lambda i, s: (i, Z_AK // kvw)),
            pl.BlockSpec((rows, kvw), lambda i, s: (i, Z_AV // kvw)),
            pl.BlockSpec((1, sb, wb, kvw), lambda i, s: (layer, i, 0, 0)),
            pl.BlockSpec((1, sb, wb, kvw), lambda i, s: (layer, i, 0, 0)),
            pl.BlockSpec((1, 2 * HEAD_DIM), lambda i, s: (0, 0)),
            pl.BlockSpec((1, 2 * HEAD_DIM), lambda i, s: (0, 0)),
        ],
        out_specs=[
            pl.BlockSpec((rows, D_MODEL), lambda i, s: (i, 0)),
            pl.BlockSpec((rows, kvw), lambda i, s: (i, 0)),
            pl.BlockSpec((rows, kvw), lambda i, s: (i, 0)),
        ],
    )
    return pl.pallas_call(
        _attn_sample_kernel,
        grid_spec=grid_spec,
        out_shape=[
            jax.ShapeDtypeStruct((batch * l, D_MODEL), BF16),
            jax.ShapeDtypeStruct((batch * l, kvw), F32),
            jax.ShapeDtypeStruct((batch * l, kvw), F32),
        ],
        compiler_params=_cparams(("arbitrary",)),
        name="attn_sample",
    )(sinks, z, z, z, kc, vc, pair_gain(qg), pair_gain(kg))


def _ret_tables(chunk, pos0, length):
    h = np.arange(R_HEADS, dtype=np.float64)
    log_gamma = np.log1p(-np.exp2(-5.0 - h))
    idx = np.arange(chunk, dtype=np.float64)
    diff = idx[:, None] - idx[None, :]
    intra = np.where(diff[None] >= 0, np.exp(np.maximum(diff, 0.0)[None] * log_gamma[:, None, None]), 0.0)
    q_decay = np.exp((idx + 1.0)[:, None] * log_gamma[None, :])
    k_decay = np.exp((chunk - 1.0 - idx)[:, None] * log_gamma[None, :])
    c_decay = np.exp(chunk * log_gamma)
    qd = np.repeat(q_decay, R_DK, axis=1)
    kd = np.repeat(k_decay, R_DK, axis=1)
    inv_freq = ROPE_BASE ** (-np.arange(0, R_DK, 2, dtype=np.float64) / R_DK)
    ang = (pos0 + np.arange(length, dtype=np.float64))[:, None] * inv_freq[None, :]
    cos = np.tile(np.concatenate([np.cos(ang), np.cos(ang)], axis=1), (1, R_HEADS))
    sin = np.tile(np.concatenate([-np.sin(ang), np.sin(ang)], axis=1), (1, R_HEADS))
    f = lambda a: jnp.asarray(a, F32)
    return f(intra), f(qd), f(kd), [float(c) for c in c_decay], f(cos), f(sin)


def _rope(x, cos, sin):
    n = x.shape[-1]
    half = R_DK // 2
    lane = lax.broadcasted_iota(jnp.int32, x.shape, 1)
    up = pltpu.roll(x, n - half, axis=1)
    dn = pltpu.roll(x, half, axis=1)
    partner = jnp.where(lane % R_DK < half, up, dn)
    return x * cos + partner * sin


def _rope_mxu(x, cos, sin):
    tw = 2 * R_DK
    half = R_DK // 2
    r = lax.broadcasted_iota(jnp.int32, (tw, tw), 0)
    c = lax.broadcasted_iota(jnp.int32, (tw, tw), 1)
    perm = jnp.where((r // R_DK == c // R_DK) & ((r + half) % R_DK == c % R_DK), 1.0, 0.0).astype(BF16)
    parts = []
    for t in range(x.shape[1] // tw):
        hi, lo = _split_bf16(x[:, t * tw:(t + 1) * tw])
        parts.append(_dot(hi, perm) + _dot(lo, perm))
    return x * cos + jnp.concatenate(parts, axis=-1) * sin


def _head_ln(o, g):
    mu = jnp.mean(o, axis=-1, keepdims=True)
    oc = o - mu
    var = jnp.mean(oc * oc, axis=-1, keepdims=True)
    return oc * lax.rsqrt(var + EPS) * g


def _ret_prompt_block(q_ref, k_ref, v_ref, g_ref, cos_ref, sin_ref, intra_ref, qd_ref, kd_ref, ng_ref,
                      o_ref, s_ref, state, first, last, *, c_decay):
    def reset():
        state[...] = jnp.zeros_like(state)

    _when(first, reset)

    cos = cos_ref[...]
    sin = sin_ref[...]
    q = _rope_mxu(q_ref[...].astype(F32), cos, sin)
    k = _rope_mxu(k_ref[...].astype(F32), cos, sin) * (R_DK ** -0.5)
    qb = q.astype(BF16)
    kb = k.astype(BF16)
    qdb = (q * qd_ref[...]).astype(BF16)
    kdb = (k * kd_ref[...]).astype(BF16)
    c = q.shape[0]
    ks = [slice(h * R_DK, (h + 1) * R_DK) for h in range(R_HEADS)]
    vs = [slice(h * R_DV, (h + 1) * R_DV) for h in range(R_HEADS)]
    a = jnp.concatenate([_dot_nt(qb[:, ks[h]], kb[:, ks[h]]) for h in range(R_HEADS)], axis=0)
    ab = (a * intra_ref[...].reshape(R_HEADS * c, c)).astype(BF16)
    outs = []
    for h in range(R_HEADS):
        vh = v_ref[:, vs[h]]
        s_old = state[h]
        outs.append(_dot(ab[h * c:(h + 1) * c], vh) + _dot(qdb[:, ks[h]], s_old.astype(BF16)))
        state[h] = s_old * c_decay[h] + _dot_tn(kdb[:, ks[h]], vh)
    y = _head_ln(jnp.concatenate(outs, axis=0), 1.0)
    gate = g_ref[...].astype(F32) * ng_ref[...]
    for h in range(R_HEADS):
        o_ref[:, vs[h]] = (y[h * c:(h + 1) * c] * gate[:, vs[h]]).astype(o_ref.dtype)

    def emit():
        s_ref[0] = state[...]

    _when(last, emit)


MIXER_BLOCKS = 4


def _mixer_prompt_kernel(sink_ref, q_ref, k_ref, v_ref, qg_ref, kg_ref,
                         rq_ref, rk_ref, rv_ref, rg_ref, cos_ref, sin_ref, intra_ref, qd_ref, kd_ref, ng_ref,
                         oa_ref, nk_ref, nv_ref, ob_ref, s_ref, kprev, vprev, state, *, c_decay):
    n = pl.program_id(1)
    c = WINDOW
    for h in range(MIXER_BLOCKS):
        rows = pl.ds(h * c, c)
        first = (n == 0) if h == 0 else None
        last = (n == pl.num_programs(1) - 1) if h == MIXER_BLOCKS - 1 else None
        _attn_prompt_block(sink_ref, q_ref.at[rows], k_ref.at[rows], v_ref.at[rows], qg_ref, kg_ref,
                           oa_ref.at[rows], nk_ref, nv_ref, kprev, vprev, first, last)
        _ret_prompt_block(rq_ref.at[rows], rk_ref.at[rows], rv_ref.at[rows], rg_ref.at[rows], cos_ref.at[rows],
                          sin_ref.at[rows], intra_ref, qd_ref, kd_ref, ng_ref, ob_ref.at[rows], s_ref, state,
                          first, last, c_decay=c_decay)


def _mixer_prompt_call(z, sinks, qg, kg, ret_norm_g, batch, seq):
    assert WINDOW == R_CHUNK
    w = WINDOW
    c = MIXER_BLOCKS * w
    nb = seq // c
    kvw = A_KV_HEADS * HEAD_DIM
    qkw = R_HEADS * R_DK
    intra, qd, kd, c_decay, cos, sin = _ret_tables(WINDOW, 0, seq)
    row = lambda b, n, s: b * nb + n
    zcol = lambda width, off: pl.BlockSpec((c, width), lambda b, n, s: (row(b, n, s), off // width))
    const = lambda shape: pl.BlockSpec(shape, lambda b, n, s: (0,) * len(shape))
    pos = pl.BlockSpec((c, qkw), lambda b, n, s: (n, 0))
    grid_spec = pltpu.PrefetchScalarGridSpec(
        num_scalar_prefetch=1,
        grid=(batch, nb),
        in_specs=[
            zcol(D_MODEL, Z_AQ), zcol(kvw, Z_AK), zcol(kvw, Z_AV), const((1, 2 * HEAD_DIM)), const((1, 2 * HEAD_DIM)),
            zcol(qkw, Z_RQ), zcol(qkw, Z_RK), zcol(D_MODEL, Z_RV), zcol(D_MODEL, Z_RG), pos, pos,
            const((R_HEADS, w, w)), const((w, qkw)), const((w, qkw)), const((1, D_MODEL)),
        ],
        out_specs=[
            pl.BlockSpec((c, D_MODEL), lambda b, n, s: (row(b, n, s), 0)),
            pl.BlockSpec((1, w, kvw), lambda b, n, s: (b, 0, 0)),
            pl.BlockSpec((1, w, kvw), lambda b, n, s: (b, 0, 0)),
            pl.BlockSpec((c, D_MODEL), lambda b, n, s: (row(b, n, s), 0)),
            pl.BlockSpec((1, R_HEADS, R_DK, R_DV), lambda b, n, s: (b, 0, 0, 0)),
        ],
        scratch_shapes=[pltpu.VMEM((w, kvw), BF16), pltpu.VMEM((w, kvw), BF16),
                        pltpu.VMEM((R_HEADS, R_DK, R_DV), F32)],
    )
    pair_gain = lambda g: jnp.tile(g, 2).reshape(1, 2 * HEAD_DIM)
    return pl.pallas_call(
        functools.partial(_mixer_prompt_kernel, c_decay=c_decay),
        grid_spec=grid_spec,
        out_shape=[
            jax.ShapeDtypeStruct((batch * seq, D_MODEL), BF16),
            jax.ShapeDtypeStruct((batch, w, kvw), F32),
            jax.ShapeDtypeStruct((batch, w, kvw), F32),
            jax.ShapeDtypeStruct((batch * seq, D_MODEL), BF16),
            jax.ShapeDtypeStruct((batch, R_HEADS, R_DK, R_DV), F32),
        ],
        compiler_params=_cparams(("arbitrary", "arbitrary")),
        name="mixer_prompt",
    )(sinks, z, z, z, pair_gain(qg), pair_gain(kg), z, z, z, z, cos, sin, intra, qd, kd,
      ret_norm_g.reshape(1, D_MODEL))


RET_S_SEQS = 8


def _ret_sample_kernel(q_ref, k_ref, v_ref, g_ref, s0_ref, cos_ref, sin_ref, intra_ref, qd_ref, kd_ref, ng_ref,
                       *rest, c_decay, l):
    o_ref, s_ref = rest[-2:]
    sb = s0_ref.shape[1]
    rows = sb * l
    cos = cos_ref[...]
    sin = sin_ref[...]
    q = _rope(q_ref[...].astype(F32), cos, sin)
    k = _rope(k_ref[...].astype(F32), cos, sin) * (R_DK ** -0.5)
    qb = q.astype(BF16)
    kb = k.astype(BF16)
    qdb = (q * qd_ref[...]).astype(BF16)
    kdb = (k * kd_ref[...]).astype(BF16)
    for h in range(R_HEADS):
        ks = slice(h * R_DK, (h + 1) * R_DK)
        vs = slice(h * R_DV, (h + 1) * R_DV)
        vh = v_ref[:, vs]
        a = _dot_nt(qb[:, ks], kb[:, ks]) * intra_ref[h]
        o = _dot(a.astype(BF16), vh)
        cross, new_s = [], []
        for b in range(sb):
            rs = slice(b * l, (b + 1) * l)
            s_old = s0_ref[0, b, h]
            cross.append(_dot(qdb[rs, ks], s_old.astype(BF16)))
            s_ref[0, b, h] = s_old * c_decay[h] + _dot_tn(kdb[rs, ks], vh[rs])
        o = o + jnp.concatenate(cross, axis=0)
        y = _head_ln(o, ng_ref[:, vs]) * g_ref[:, vs].astype(F32)
        o_ref[:, vs] = y.astype(o_ref.dtype)


def _ret_sample_call(z, s0, ret_norm_g, batch, l, layer, stack):
    c = math.gcd(l, R_CHUNK)
    assert c == l, "sample step expects a single retention chunk"
    sb = RET_S_SEQS
    rows = sb * l
    intra, qd, kd, c_decay, cos, sin = _ret_tables(c, PAST_LEN, l)
    eye = jnp.eye(sb, dtype=F32)
    intra_bd = jnp.einsum("ab,hij->haibj", eye, intra).reshape(R_HEADS, rows, rows)
    tile = lambda t: jnp.tile(t, (sb, 1))
    qkw = R_HEADS * R_DK
    const2 = lambda i: (0, 0)
    return pl.pallas_call(
        functools.partial(_ret_sample_kernel, c_decay=c_decay, l=l),
        grid=(batch // sb,),
        in_specs=[
            pl.BlockSpec((rows, qkw), lambda i: (i, Z_RQ // qkw)),
            pl.BlockSpec((rows, qkw), lambda i: (i, Z_RK // qkw)),
            pl.BlockSpec((rows, D_MODEL), lambda i: (i, Z_RV // D_MODEL)),
            pl.BlockSpec((rows, D_MODEL), lambda i: (i, Z_RG // D_MODEL)),
            pl.BlockSpec((1, sb, R_HEADS, R_DK, R_DV), lambda i: (layer, i, 0, 0, 0)),
            pl.BlockSpec((rows, qkw), const2),
            pl.BlockSpec((rows, qkw), const2),
            pl.BlockSpec((R_HEADS, rows, rows), lambda i: (0, 0, 0)),
            pl.BlockSpec((rows, qkw), const2),
            pl.BlockSpec((rows, qkw), const2),
            pl.BlockSpec((1, D_MODEL), const2),
            pl.BlockSpec(memory_space=pl.ANY),
        ],
        out_specs=[
            pl.BlockSpec((rows, D_MODEL), lambda i: (i, 0)),
            pl.BlockSpec((1, sb, R_HEADS, R_DK, R_DV), lambda i: (layer, i, 0, 0, 0)),
        ],
        out_shape=[
            jax.ShapeDtypeStruct((batch * l, D_MODEL), BF16),
            jax.ShapeDtypeStruct(s0.shape, F32),
        ],
        input_output_aliases={11: 1},
        compiler_params=_cparams(("arbitrary",)),
        name="ret_sample",
    )(z, z, z, z, s0, tile(cos), tile(sin), intra_bd, tile(qd), tile(kd), ret_norm_g.reshape(1, D_MODEL), stack)


def _route_rows(logits, carry):
    tm = logits.shape[0]
    lane = lax.broadcasted_iota(jnp.int32, logits.shape, 1).astype(F32)
    work = logits
    sel = jnp.zeros(logits.shape, F32)
    vals, idxs = [], []
    for _ in range(TOP_K):
        m = jnp.max(work, axis=-1, keepdims=True)
        idx = jnp.min(jnp.where(work == m, lane, float(ROUTER_PAD)), axis=-1, keepdims=True)
        hit = lane == idx
        vals.append(m)
        idxs.append(idx)
        sel = jnp.where(hit, 1.0, sel)
        work = jnp.where(hit, -3e38, work)
    ex = [jnp.exp(v - vals[0]) for v in vals]
    den = ex[0] + ex[1] + ex[2] + ex[3]
    r = lax.broadcasted_iota(jnp.int32, (tm, tm), 0)
    c = lax.broadcasted_iota(jnp.int32, (tm, tm), 1)
    before = jnp.where(c < r, 1.0, 0.0).astype(BF16)
    rank = _dot(before, sel.astype(BF16)) + carry
    route = jnp.zeros(logits.shape, F32)
    for k in range(TOP_K):
        route = jnp.where(lane == float(k), ex[k] / den, route)
        route = jnp.where(lane == float(TOP_K + k), idxs[k], route)
        rk = jnp.sum(jnp.where(lane == idxs[k], rank, 0.0), axis=-1, keepdims=True)
        route = jnp.where(lane == float(2 * TOP_K + k), rk, route)
    return route, carry + jnp.sum(sel, axis=0, keepdims=True)


def _merge_kernel(*refs, emit_cv):
    (x_ref, oa_ref, ob_ref, cu_ref, cv_ref, ga_ref, gb_ref, gc_ref, g1_ref, sh2_ref, sc2_ref,
     lnf_ref, mix_ref, mixb_ref, wout_ref, rwh_ref, rwl_ref, rb_ref, cnt_ref) = refs[:19]
    n_out = 5 if emit_cv else 4
    outs = refs[-(n_out + 1):]
    if emit_cv:
        xo_ref, h2_ref, route_ref, cnto_ref, cvo_ref, carry = outs
    else:
        xo_ref, h2_ref, route_ref, cnto_ref, carry = outs
    first = (pl.program_id(0) == 0) & (pl.program_id(1) == 0)

    @pl.when(first)
    def _():
        carry[...] = cnt_ref[...]

    x = x_ref[...]
    gbk, rb, d = x.shape
    tm = gbk * rb
    cw = C_CHUNK
    gw = d // C_GROUPS

    cvb = cv_ref[...]
    if emit_cv:
        cvo_ref[...] = cvb.astype(F32)
    mixed_rows = []
    for c in range(tm // cw):
        rs = slice(c * cw, (c + 1) * cw)
        cols = [_dot(mix_ref[g], cvb[rs, g * gw:(g + 1) * gw]) for g in range(C_GROUPS)]
        mixed_rows.append(jnp.concatenate(cols, axis=-1) + mixb_ref[...])
    mixed = jnp.concatenate(mixed_rows, axis=0)
    oc = cu_ref[...].astype(F32) * mixed
    merged = (ga_ref[...].astype(F32) * oa_ref[...].astype(F32) + gb_ref[...].astype(F32) * ob_ref[...].astype(F32)
              + gc_ref[...].astype(F32) * oc)
    y = _dot(merged.astype(BF16), wout_ref[...])
    x = x + g1_ref[...] * y.reshape(gbk, rb, d)
    xo_ref[...] = x

    h2 = (_rms(x) * lnf_ref[...] * (1.0 + sc2_ref[...]) + sh2_ref[...]).reshape(tm, d)
    h2b = h2.astype(BF16)
    h2_ref[0], h2_ref[1] = _pack_rows(h2)
    h2l = (h2 - h2b.astype(F32)).astype(BF16)
    rwh = rwh_ref[...]
    logits = _dot(h2b, rwh) + _dot(h2l, rwh) + _dot(h2b, rwl_ref[...]) + rb_ref[...]
    route, counts = _route_rows(logits, carry[...])
    route_ref[...] = route
    carry[...] = counts
    cnto_ref[...] = counts


def _merge_call(x, z, oa, ob, mod, ln_ffn_g, mix, mixb, w_out_bf, rw_hi, rw_lo, rb,
                counts, t_all, tile_off, shared, emit_cv):
    g, r, d = x.shape
    gb, rb_ = _group_blocks(g, r)
    tm = gb * rb_
    nj = r // rb_
    t = g * r
    xspec = pl.BlockSpec((gb, rb_, d), lambda i, j: (i, j, 0))
    rows = lambda col: pl.BlockSpec((tm, d), lambda i, j: (i * nj + j, col))
    mspec = lambda col: pl.BlockSpec((gb, 1, d), lambda i, j: (i, 0, col))
    const = lambda shape: pl.BlockSpec(shape, lambda i, j: (0,) * len(shape))
    in_specs = [
        xspec, rows(0), rows(0),
        rows(Z_CU // d), rows(Z_CV // d), rows(Z_GA // d), rows(Z_GB // d), rows(Z_GC // d),
        mspec(2), mspec(3), mspec(4),
        const((1, 1, d)),
        const((C_GROUPS, C_CHUNK, C_CHUNK)), const((C_CHUNK, d)),
        const((d, d)), const((d, ROUTER_PAD)), const((d, ROUTER_PAD)), const((1, ROUTER_PAD)),
        const((1, ROUTER_PAD)),
    ]
    aliases = {len(in_specs): 1, len(in_specs) + 1: 2}
    in_specs += [pl.BlockSpec(memory_space=pl.ANY), pl.BlockSpec(memory_space=pl.ANY)]
    out_specs = [xspec, pl.BlockSpec((2, tm, PACK_W), lambda i, j: (0, tile_off + i * nj + j, 0)),
                 pl.BlockSpec((tm, ROUTER_PAD), lambda i, j: (tile_off + i * nj + j, 0)), const((1, ROUTER_PAD))]
    out_shape = [jax.ShapeDtypeStruct(x.shape, F32), jax.ShapeDtypeStruct((2, t_all, PACK_W), U32),
                 jax.ShapeDtypeStruct((t_all, ROUTER_PAD), F32), jax.ShapeDtypeStruct((1, ROUTER_PAD), F32)]
    if emit_cv:
        out_specs.append(rows(0))
        out_shape.append(jax.ShapeDtypeStruct((t, d), F32))
    return pl.pallas_call(
        functools.partial(_merge_kernel, emit_cv=emit_cv),
        grid=(g // gb, nj), in_specs=in_specs, out_specs=out_specs, out_shape=out_shape,
        scratch_shapes=[pltpu.VMEM((1, ROUTER_PAD), F32)],
        input_output_aliases=aliases,
        compiler_params=_cparams(("arbitrary", "arbitrary")),
        name="merge",
    )(x, oa, ob, z, z, z, z, z, mod, mod, mod, ln_ffn_g.reshape(1, 1, d),
      mix, mixb, w_out_bf, rw_hi, rw_lo, rb, counts, *shared)


def _moe_kernel(be_ref, nx_ref, nh_ref, nu_ref, x_ref, w1_hbm, b1_ref, w2_hbm, b2_ref, o_ref,
                w1s, w2s, w1b, w2b, sem, *, layer):
    i = pl.program_id(0)
    e = be_ref[i]
    prev = be_ref[jnp.maximum(i - 1, 0)]

    def weight_copies(ex):
        return (pltpu.make_async_copy(w1_hbm.at[layer, ex], w1s, sem.at[0]),
                pltpu.make_async_copy(w2_hbm.at[layer, ex], w2s, sem.at[1]))

    @pl.when(i == 0)
    def _():
        for cp in weight_copies(e):
            cp.start()

    @pl.when((i == 0) | (e != prev))
    def _():
        for cp in weight_copies(e):
            cp.wait()
        w1b[...] = w1s[...].astype(BF16)
        w2b[...] = w2s[...].astype(BF16)
        nxt = nx_ref[i]

        @pl.when(nxt >= 0)
        def _():
            for cp in weight_copies(nxt):
                cp.start()

    def experts(rows):
        rs = slice(0, rows)
        xb = _unpack_rows(x_ref[0, rs], x_ref[1, rs]).astype(BF16)
        hdn = _dot(xb, w1b[...]) + b1_ref[0, 0]
        g = jnp.minimum(hdn[:, :D_FF], SWIGLU_LIMIT)
        up = jnp.clip(hdn[:, D_FF:], -SWIGLU_LIMIT, SWIGLU_LIMIT)
        act = (up + 1.0) * (g * jax.nn.sigmoid(g * SWIGLU_ALPHA))
        o_ref[0, rs], o_ref[1, rs] = _pack_rows(_dot(act.astype(BF16), w2b[...]) + b2_ref[0, 0])

    for nh in range(1, MOE_HALVES + 1):
        pl.when(nh_ref[i] == nh)(functools.partial(experts, nh * MOE_ROWS))


def _moe_call(xb, block_e, next_e, n_halves, n_used, w1, b1, w2, b2, layer):
    _, n_rows, _ = xb.shape
    step_rows = MOE_HALVES * MOE_ROWS
    nblk = n_rows // step_rows
    depth, ne, d, f2 = w1.shape
    last = lambda i, be, nx, nh, nu: jnp.minimum(i, nu[0] - 1)
    xspec = pl.BlockSpec((2, step_rows, PACK_W), lambda i, be, nx, nh, nu: (0, last(i, be, nx, nh, nu), 0))
    grid_spec = pltpu.PrefetchScalarGridSpec(
        num_scalar_prefetch=4,
        grid=(nblk,),
        in_specs=[
            xspec,
            pl.BlockSpec(memory_space=pl.ANY),
            pl.BlockSpec((1, 1, 1, f2), lambda i, be, nx, nh, nu: (layer, be[i], 0, 0)),
            pl.BlockSpec(memory_space=pl.ANY),
            pl.BlockSpec((1, 1, 1, d), lambda i, be, nx, nh, nu: (layer, be[i], 0, 0)),
        ],
        out_specs=xspec,
        scratch_shapes=[pltpu.VMEM((d, f2), F32), pltpu.VMEM((f2 // 2, d), F32),
                        pltpu.VMEM((d, f2), BF16), pltpu.VMEM((f2 // 2, d), BF16),
                        pltpu.SemaphoreType.DMA((2,))],
    )
    return pl.pallas_call(
        functools.partial(_moe_kernel, layer=layer),
        grid_spec=grid_spec,
        out_shape=jax.ShapeDtypeStruct(xb.shape, U32),
        compiler_params=_cparams(("arbitrary",)),
        name="moe",
    )(block_e, next_e, n_halves, n_used, xb, w1, b1.reshape(depth, ne, 1, f2), w2, b2.reshape(depth, ne, 1, d))


def _sc_mesh():
    return plsc.VectorSubcoreMesh(core_axis_name="core", subcore_axis_name="subcore")


def _sc_scatter_rows(x, idx, n_out):
    t, c = x.shape
    kk = idx.shape[0]

    @pl.kernel(out_type=jax.ShapeDtypeStruct((n_out, c), x.dtype), mesh=_sc_mesh(), scratch_types=[])
    def scatter(x_hbm, i_hbm, o_hbm):
        def body(x_vmem, i_vmem):
            for k in range(kk):
                pltpu.sync_copy(x_vmem, o_hbm.at[i_vmem.at[k]])

        pltpu.emit_pipeline(
            body,
            grid=(t // SC_WINDOW,),
            in_specs=[pl.BlockSpec((SC_WINDOW, c), index_map=lambda i: (i, 0)),
                      pl.BlockSpec((kk, SC_WINDOW), index_map=lambda i: (0, i))],
            out_specs=[],
            core_axis_name=("core", "subcore"),
            dimension_semantics=(pltpu.PARALLEL,),
        )(x_hbm, i_hbm)

    return scatter(x, idx)


def _sc_gather_rows(data, idx):
    n = idx.shape[0]
    c = data.shape[1]

    @pl.kernel(out_type=jax.ShapeDtypeStruct((n, c), data.dtype), mesh=_sc_mesh(), scratch_types=[])
    def gather(x_hbm, i_hbm, o_hbm):
        def body(i_vmem, o_vmem):
            pltpu.sync_copy(x_hbm.at[i_vmem.at[0]], o_vmem)

        pltpu.emit_pipeline(
            body,
            grid=(n // SC_WINDOW,),
            in_specs=[pl.BlockSpec((1, SC_WINDOW), index_map=lambda i: (0, i))],
            out_specs=[pl.BlockSpec((SC_WINDOW, c), index_map=lambda i: (i, 0))],
            core_axis_name=("core", "subcore"),
            dimension_semantics=(pltpu.PARALLEL,),
        )(i_hbm, o_hbm)

    return gather(data, idx.reshape(1, n))


def _plan(route, counts):
    t = route.shape[0]
    step_rows = MOE_HALVES * MOE_ROWS
    n_steps = -(-t * TOP_K // step_rows) + N_EXPERTS
    e4 = route[:, TOP_K:2 * TOP_K].astype(jnp.int32)
    r4 = route[:, 2 * TOP_K:3 * TOP_K].astype(jnp.int32)
    cnt = counts[0, :N_EXPERTS].astype(jnp.int32)
    halves = (cnt + MOE_ROWS - 1) // MOE_ROWS
    steps = (halves + MOE_HALVES - 1) // MOE_HALVES
    send = jnp.cumsum(steps)
    sstart = send - steps
    onehot = e4[:, :, None] == jnp.arange(N_EXPERTS, dtype=jnp.int32)[None, None, :]
    dest = (r4 + jnp.sum(jnp.where(onehot, (sstart * step_rows)[None, None, :], 0), axis=-1)).T
    step = jnp.arange(n_steps, dtype=jnp.int32)
    step_e = jnp.sum((step[:, None] >= send[None, :]).astype(jnp.int32), axis=-1)
    n_used = send[-1].astype(jnp.int32).reshape(1)
    step_e = jnp.minimum(step_e, jnp.sum((send < send[-1]).astype(jnp.int32))).astype(jnp.int32)
    mine = step_e[:, None] == jnp.arange(N_EXPERTS, dtype=jnp.int32)[None, :]
    pick = lambda table: jnp.sum(jnp.where(mine, table[None, :], 0), axis=-1)
    n_halves = jnp.clip(pick(halves) - MOE_HALVES * (step - pick(sstart)), 0, MOE_HALVES)
    n_halves = jnp.where(step < n_used[0], n_halves, 0).astype(jnp.int32)
    after = pick(send)
    next_e = jnp.sum((after[:, None] >= send[None, :]).astype(jnp.int32), axis=-1)
    next_e = jnp.where(after < n_used[0], next_e, -1).astype(jnp.int32)
    return dest, step_e, next_e, n_halves, n_used, n_steps * step_rows


def _moe(h2p, route, counts, w1, b1, w2, b2, layer):
    _, t, pw = h2p.shape
    dest, step_e, next_e, n_halves, n_used, n_rows = _plan(route, counts)
    idx_s = jnp.concatenate([dest, dest + n_rows], axis=1)
    xb = _sc_scatter_rows(h2p.reshape(2 * t, pw), idx_s, 2 * n_rows).reshape(2, n_rows, pw)
    yb = _moe_call(xb, step_e, next_e, n_halves, n_used, w1, b1, w2, b2, layer)
    idx_g = jnp.concatenate([dest.reshape(-1), dest.reshape(-1) + n_rows])
    y4 = _sc_gather_rows(yb.reshape(2 * n_rows, pw), idx_g)
    return y4.reshape(2, TOP_K, t, pw)


def _reorder_w_in(w):
    o = _ORIG
    seg = lambda a, n: w[:, a:a + n]
    d = D_MODEL
    q_heads = sorted(((_q_col(kv, g), kv * A_GROUP + g) for kv in range(A_KV_HEADS) for g in range(A_GROUP)))
    aq = jnp.concatenate([seg(o["aq"] + h * HEAD_DIM, HEAD_DIM) for _, h in q_heads], axis=1)
    pad = jnp.zeros((d, 256), w.dtype)
    parts = [aq, seg(o["rv"], d), seg(o["rg"], d), seg(o["cu"], d), seg(o["rq"], 512), seg(o["ak"], 256), pad,
             seg(o["cv"], d), seg(o["mg"], d), seg(o["mg"] + d, d), seg(o["mg"] + 2 * d, d),
             seg(o["rk"], 512), seg(o["av"], 256), pad]
    w = jnp.concatenate(parts, axis=1).astype(BF16)
    return w.reshape(D_MODEL, IN_CHUNKS, D_IN // IN_CHUNKS).transpose(1, 0, 2)


def _gmlp_tables(ws, bs, chunk_len, rows):
    causal = jnp.tril(jnp.ones((C_CHUNK, C_CHUNK), dtype=bool))
    w = jnp.where(causal[None], ws, 0.0)[:, :chunk_len, :chunk_len]
    reps = rows // chunk_len
    eye = jnp.eye(reps, dtype=F32)
    mix = jnp.einsum("ab,gts->gatbs", eye, w).reshape(C_GROUPS, rows, rows).astype(BF16)
    b = jnp.tile(bs[:, :chunk_len].T, (reps, 1))
    mixb = jnp.repeat(b, D_MODEL // C_GROUPS, axis=1)
    return mix, mixb


def kernel(x_prompt, x_sample, c_prompt, c_sample, cache_attn_k, cache_attn_v, state_retention, ln_mix_g, ln_ffn_g, w_ada, b_ada, w_in, b_gate, q_norm_g, k_norm_g, attn_sinks, ret_norm_g, gm_ln_g, gm_ln_b, gm_ws, gm_bs, w_out, router_w, router_b, moe_w1, moe_b1, moe_w2, moe_b2):
    bp, lp, d = x_prompt.shape
    bs, ls, _ = x_sample.shape
    tp, ts = bp * lp, bs * ls
    kvw = A_KV_HEADS * HEAD_DIM
    wb = cache_attn_k.shape[2]

    mod_all = _ada_call(jnp.concatenate([c_prompt, c_sample], axis=0), w_ada, b_ada)

    cache_k = cache_attn_k.reshape(DEPTH, bs, wb, kvw)
    cache_v = cache_attn_v.reshape(DEPTH, bs, wb, kvw)
    xp, xs = x_prompt, x_sample
    y4 = None
    mod_p = mod_s = None
    s_all = jnp.zeros(state_retention.shape, F32)
    h2 = jnp.zeros((2, tp + ts, PACK_W), U32)
    route = jnp.zeros((tp + ts, ROUTER_PAD), F32)
    pk, pv, ps, sk, sv, sg = [], [], [], [], [], []
    for l in range(DEPTH):
        prev_mod_p, prev_mod_s = mod_p, mod_s
        mod_p = mod_all[l, :bp].reshape(bp, 1, 6 * d)
        mod_s = mod_all[l, bp:].reshape(bs, 1, 6 * d)
        w_in_bf = _reorder_w_in(w_in[l])
        w_out_bf = w_out[l].astype(BF16)
        rw = jnp.pad(router_w[l], ((0, 0), (0, ROUTER_PAD - N_EXPERTS)))
        rw_hi, rw_lo = _split_bf16(rw)
        rb = jnp.pad(router_b[l], (0, ROUTER_PAD - N_EXPERTS), constant_values=NEG_BIG).reshape(1, ROUTER_PAD)

        proj = (ln_mix_g[l], w_in_bf, b_gate[l], gm_ln_g[l], gm_ln_b[l])
        if l == 0:
            zp = _inproj_call(xp, mod_p, *proj)
            zs = _inproj_call(xs, mod_s, *proj)
        else:
            zp, xp = _inproj_call(xp, mod_p, *proj, moe_out=(y4, route, prev_mod_p, 0))
            zs, xs = _inproj_call(xs, mod_s, *proj, moe_out=(y4, route, prev_mod_s, tp // ROW_TILE))
        oa_p, k_p, v_p, ob_p, s_p = _mixer_prompt_call(zp, attn_sinks[l], q_norm_g[l], k_norm_g[l], ret_norm_g[l],
                                                       bp, lp)
        oa_s, k_s, v_s = _attn_sample_call(zs, cache_k, cache_v, attn_sinks[l], q_norm_g[l], k_norm_g[l], bs, ls, l)
        ob_s, s_all = _ret_sample_call(zs, state_retention, ret_norm_g[l], bs, ls, l, stack=s_all)

        mix_p, mixb_p = _gmlp_tables(gm_ws[l], gm_bs[l], C_CHUNK, C_CHUNK)
        mix_s, mixb_s = _gmlp_tables(gm_ws[l], gm_bs[l], ls, C_CHUNK)
        common = (ln_ffn_g[l],)
        tail = (w_out_bf, rw_hi, rw_lo, rb)
        zero_counts = jnp.zeros((1, ROUTER_PAD), F32)
        xp, h2, route, cnt_p = _merge_call(xp, zp, oa_p, ob_p, mod_p, *common, mix_p, mixb_p, *tail,
                                           zero_counts, tp + ts, 0, (h2, route), emit_cv=False)
        xs, h2, route, cnt, cv_s = _merge_call(xs, zs, oa_s, ob_s, mod_s, *common, mix_s, mixb_s, *tail,
                                               cnt_p, tp + ts, tp // ROW_TILE, (h2, route), emit_cv=True)
        y4 = _moe(h2, route, cnt, moe_w1, moe_b1, moe_w2, moe_b2, l)

        pk.append(k_p.reshape(bp, WINDOW, A_KV_HEADS, HEAD_DIM))
        pv.append(v_p.reshape(bp, WINDOW, A_KV_HEADS, HEAD_DIM))
        ps.append(s_p)
        sk.append(k_s.reshape(bs, ls, A_KV_HEADS, HEAD_DIM))
        sv.append(v_s.reshape(bs, ls, A_KV_HEADS, HEAD_DIM))
        sg.append(cv_s.reshape(bs, ls, d))

    xp = _resid_call(xp, y4, route, mod_p, 0)
    xs = _resid_call(xs, y4, route, mod_s, tp // ROW_TILE)
    return (xp, xs, jnp.stack(pk), jnp.stack(pv), jnp.stack(ps), jnp.stack(sk), jnp.stack(sv),
            s_all, jnp.stack(sg))
```

```python
import functools
import math

import numpy as np
import jax
import jax.numpy as jnp
from jax import lax
from jax.experimental import pallas as pl
from jax.experimental.pallas import tpu as pltpu
from jax.experimental.pallas import tpu_sc as plsc

F32 = jnp.float32
BF16 = jnp.bfloat16
U32 = jnp.uint32

D_MODEL = 1024
DEPTH = 4
PAST_LEN = 8192
HEAD_DIM = 64
A_Q_HEADS = 16
A_KV_HEADS = 4
A_GROUP = 4
WINDOW = 128
R_HEADS = 8
R_DK = 64
R_DV = 128
R_CHUNK = 128
ROPE_BASE = 10000.0
C_CHUNK = 128
C_GROUPS = 8
N_EXPERTS = 32
TOP_K = 4
D_FF = D_MODEL
SWIGLU_LIMIT = 7.0
SWIGLU_ALPHA = 1.702
EPS = 1e-6

Z_AQ, Z_RV, Z_RG, Z_CU, Z_RQ, Z_AK = 0, 1024, 2048, 3072, 4096, 4608
Z_CV, Z_GA, Z_GB, Z_GC, Z_RK, Z_AV = 5120, 6144, 7168, 8192, 9216, 9728
D_IN = 10240
IN_CHUNKS = 2
_ORIG = dict(aq=0, ak=1024, av=1280, rq=1536, rk=2048, rv=2560, rg=3584, cu=4608, cv=5632, mg=6656)

ROW_TILE = 512
MOE_ROWS = 256
MOE_HALVES = 4
ROUTER_PAD = 128
PACK_W = D_MODEL // 4
SC_WINDOW = 128
HI_MASK = 0xFFFF0000
VMEM_LIMIT = 56 * 1024 * 1024
NEG_BIG = -1e30


def _cparams(sem):
    return pltpu.CompilerParams(dimension_semantics=sem, vmem_limit_bytes=VMEM_LIMIT)


def _split_bf16(x):
    hi = x.astype(BF16)
    lo = (x - hi.astype(F32)).astype(BF16)
    return hi, lo


def _dot(a, b):
    return jnp.dot(a, b, preferred_element_type=F32)


def _dot_nt(a, b):
    return lax.dot_general(a, b, (((1,), (1,)), ((), ())), preferred_element_type=F32)


def _dot_tn(a, b):
    return lax.dot_general(a, b, (((0,), (0,)), ((), ())), preferred_element_type=F32)


def _ada_kernel(c_ref, w_ref, b_ref, o_ref):
    c = c_ref[...]
    s_hi, s_lo = _split_bf16(c * jax.nn.sigmoid(c))
    w_hi, w_lo = _split_bf16(w_ref[0])
    acc = _dot(s_hi, w_hi) + _dot(s_lo, w_hi) + _dot(s_hi, w_lo)
    o_ref[0] = acc + b_ref[0]


def _ada_call(c_all, w_ada, b_ada):
    depth, d, n = w_ada.shape
    m = c_all.shape[0]
    tn = 1024
    return pl.pallas_call(
        _ada_kernel,
        grid=(depth, n // tn),
        in_specs=[
            pl.BlockSpec((m, d), lambda l, j: (0, 0)),
            pl.BlockSpec((1, d, tn), lambda l, j: (l, 0, j)),
            pl.BlockSpec((1, 1, tn), lambda l, j: (l, 0, j)),
        ],
        out_specs=pl.BlockSpec((1, m, tn), lambda l, j: (l, 0, j)),
        out_shape=jax.ShapeDtypeStruct((depth, m, n), F32),
        compiler_params=_cparams(("arbitrary", "arbitrary")),
        name="ada",
    )(c_all, w_ada, b_ada.reshape(depth, 1, n))


def _pack_rows(y):
    bits = pltpu.bitcast(y.astype(BF16).astype(F32), U32)
    q = [bits[:, i * PACK_W:(i + 1) * PACK_W] for i in range(4)]
    return (q[0] >> 16) | q[1], (q[2] >> 16) | q[3]


def _unpack_rows(a, b):
    f = lambda w: pltpu.bitcast(w, F32)
    return jnp.concatenate([f(a << 16), f(a & jnp.uint32(HI_MASK)), f(b << 16), f(b & jnp.uint32(HI_MASK))], axis=-1)


def _combine(y4_ref, route_ref):
    route = route_ref[...]
    acc = None
    for k in range(TOP_K):
        term = route[:, k:k + 1] * _unpack_rows(y4_ref[0, k], y4_ref[1, k])
        acc = term if acc is None else acc + term
    return acc


def _rms(x):
    return x * lax.rsqrt(jnp.mean(x * x, axis=-1, keepdims=True) + EPS)


def _gelu(x):
    return jax.nn.gelu(x, approximate=True)


def _silu(x):
    return x * jax.nn.sigmoid(x)


def _inproj_kernel(*refs, has_resid):
    if has_resid:
        (x_ref, y4_ref, route_ref, g2_ref, sh_ref, sc_ref, lng_ref, w_ref, bg_ref, gmg_ref, gmb_ref,
         z_ref, xo_ref, hb) = refs
    else:
        x_ref, sh_ref, sc_ref, lng_ref, w_ref, bg_ref, gmg_ref, gmb_ref, z_ref, hb = refs
    c = pl.program_id(2)

    def gm_v(acc):
        v = _gelu(acc)
        vc = v - jnp.mean(v, axis=-1, keepdims=True)
        var = jnp.mean(vc * vc, axis=-1, keepdims=True)
        return vc * lax.rsqrt(var + EPS) * gmg_ref[...] + gmb_ref[...]

    merge_gate = lambda k: (lambda acc: jax.nn.sigmoid(acc + bg_ref[k:k + 1]))
    keep = lambda acc: acc
    segment_fns = ((keep, keep, _silu, _gelu, keep),
                   (gm_v, merge_gate(0), merge_gate(1), merge_gate(2), keep))

    @pl.when(c == 0)
    def _():
        x = x_ref[...]
        if has_resid:
            x = x + g2_ref[...] * _combine(y4_ref, route_ref).reshape(x.shape)
            xo_ref[...] = x
        h = _rms(x) * lng_ref[...] * (1.0 + sc_ref[...]) + sh_ref[...]
        hb[...] = h.reshape(hb.shape).astype(BF16)

    def project(half):
        h = hb[...]
        for s, fn in enumerate(segment_fns[half]):
            cols = slice(s * D_MODEL, (s + 1) * D_MODEL)
            z_ref[:, cols] = fn(_dot(h, w_ref[half, :, cols])).astype(BF16)

    for half in range(IN_CHUNKS):
        pl.when(c == half)(functools.partial(project, half))


def _group_blocks(g, r):
    if r >= ROW_TILE:
        return 1, ROW_TILE
    return ROW_TILE // r, r


def _moe_out_specs(tm, nj, tile_off):
    return [pl.BlockSpec((2, TOP_K, tm, PACK_W), lambda i, j, *_: (0, 0, tile_off + i * nj + j, 0)),
            pl.BlockSpec((tm, ROUTER_PAD), lambda i, j, *_: (tile_off + i * nj + j, 0))]


def _inproj_call(x, mod, ln_g, w_bf, b_gate, gm_ln_g, gm_ln_b, moe_out=None):
    g, r, d = x.shape
    gb, rb = _group_blocks(g, r)
    nj = r // rb
    nc, _, cw = w_bf.shape
    grid = (g // gb, nj, nc)
    xspec = pl.BlockSpec((gb, rb, d), lambda i, j, c: (i, j, 0))
    mspec = lambda col: pl.BlockSpec((gb, 1, d), lambda i, j, c: (i, 0, col))
    in_specs, args = [xspec], [x]
    if moe_out is not None:
        y4, route, mod_prev, tile_off = moe_out
        in_specs += _moe_out_specs(gb * rb, nj, tile_off) + [mspec(5)]
        args += [y4, route, mod_prev]
    in_specs += [mspec(0), mspec(1), pl.BlockSpec((1, 1, d), lambda i, j, c: (0, 0, 0)),
                 pl.BlockSpec((nc, d, cw), lambda i, j, c: (0, 0, 0), pipeline_mode=pl.Buffered(1)),
                 pl.BlockSpec((3, d), lambda i, j, c: (0, 0)),
                 pl.BlockSpec((1, d), lambda i, j, c: (0, 0)),
                 pl.BlockSpec((1, d), lambda i, j, c: (0, 0))]
    args += [mod, mod, ln_g.reshape(1, 1, d), w_bf, b_gate.reshape(3, d), gm_ln_g.reshape(1, d),
             gm_ln_b.reshape(1, d)]
    zspec = pl.BlockSpec((gb * rb, cw), lambda i, j, c: (i * nj + j, c))
    zshape = jax.ShapeDtypeStruct((g * r, nc * cw), BF16)
    if moe_out is not None:
        out_specs, out_shape = [zspec, xspec], [zshape, jax.ShapeDtypeStruct(x.shape, F32)]
    else:
        out_specs, out_shape = zspec, zshape
    return pl.pallas_call(
        functools.partial(_inproj_kernel, has_resid=moe_out is not None),
        grid=grid, in_specs=in_specs, out_specs=out_specs, out_shape=out_shape,
        scratch_shapes=[pltpu.VMEM((gb * rb, d), BF16)],
        compiler_params=_cparams(("arbitrary", "arbitrary", "arbitrary")),
        name="inproj",
    )(*args)


def _resid_kernel(x_ref, y4_ref, route_ref, g2_ref, o_ref):
    x = x_ref[...]
    o_ref[...] = x + g2_ref[...] * _combine(y4_ref, route_ref).reshape(x.shape)


def _resid_call(x, y4, route, mod, tile_off):
    g, r, d = x.shape
    gb, rb = _group_blocks(g, r)
    nj = r // rb
    xspec = pl.BlockSpec((gb, rb, d), lambda i, j: (i, j, 0))
    return pl.pallas_call(
        _resid_kernel,
        grid=(g // gb, nj),
        in_specs=[xspec] + _moe_out_specs(gb * rb, nj, tile_off)
        + [pl.BlockSpec((gb, 1, d), lambda i, j: (i, 0, 5))],
        out_specs=xspec,
        out_shape=jax.ShapeDtypeStruct(x.shape, F32),
        compiler_params=_cparams(("arbitrary", "arbitrary")),
        name="resid",
    )(x, y4, route, mod)


def _q_col(kvh, g):
    p, kv_odd = divmod(kvh, 2)
    j, g_odd = divmod(g, 2)
    tile = p * 4 + (kv_odd ^ g_odd) * 2 + j
    return tile * 2 * HEAD_DIM + g_odd * HEAD_DIM


def _half_mats():
    r = lax.broadcasted_iota(jnp.int32, (2 * HEAD_DIM, 2 * HEAD_DIM), 0)
    c = lax.broadcasted_iota(jnp.int32, (2 * HEAD_DIM, 2 * HEAD_DIM), 1)
    seg = jnp.where(r // HEAD_DIM == c // HEAD_DIM, 1.0, 0.0).astype(BF16)
    swap = jnp.where((r + HEAD_DIM) % (2 * HEAD_DIM) == c, 1.0, 0.0).astype(BF16)
    return seg, swap


def _pair_rms(x, g2, seg):
    outs = []
    for t in range(x.shape[1] // (2 * HEAD_DIM)):
        xt = x[:, t * 2 * HEAD_DIM:(t + 1) * 2 * HEAD_DIM]
        hi, lo = _split_bf16(xt * xt)
        ss = _dot(hi, seg) + _dot(lo, seg)
        outs.append(xt * lax.rsqrt(ss * (1.0 / HEAD_DIM) + EPS) * g2)
    return jnp.concatenate(outs, axis=-1)


def _pair_tile_attention(sink_ref, qn, kall, vall, mask, w, o_ref):
    _, swap = _half_mats()
    tw = 2 * HEAD_DIM
    nk = kall.shape[0]
    row = lax.broadcasted_iota(jnp.int32, (2 * w, nk), 0)
    col = lax.broadcasted_iota(jnp.int32, (2 * w, nk), 1)
    sink_top = (col == 0) & (row < w)
    sink_bot = (col == 0) & (row >= w)
    lane_kv = lax.broadcasted_iota(jnp.int32, (nk, tw), 1)
    key_kv = lax.broadcasted_iota(jnp.int32, (nk, tw), 0)
    first_o = lax.broadcasted_iota(jnp.int32, (w, tw), 1) < HEAD_DIM
    scores, values = [], []
    for p in range(A_KV_HEADS // 2):
        kp = kall[:, p * tw:(p + 1) * tw]
        vp = vall[:, p * tw:(p + 1) * tw]
        kv_tiles = ((kp, vp), (_dot(kp, swap).astype(BF16), _dot(vp, swap).astype(BF16)))
        for variant, (kk, vv) in enumerate(kv_tiles):
            t0 = p * 4 + variant * 2
            qs = jnp.concatenate([qn[:, t0 * tw:(t0 + 1) * tw], qn[:, (t0 + 1) * tw:(t0 + 2) * tw]], axis=0)
            for half in range(2):
                keep = (lane_kv < HEAD_DIM) if half == 0 else (lane_kv >= HEAD_DIM)
                kh = jnp.where(keep, kk, jnp.zeros_like(kk))
                values.append(jnp.where(keep & (key_kv > 0), vv, jnp.zeros_like(vv)))
                kvh = 2 * p + (half ^ variant)
                s = jnp.where(mask, _dot_nt(qs, kh), NEG_BIG)
                s = jnp.where(sink_top, sink_ref[kvh * A_GROUP + half], s)
                scores.append(jnp.where(sink_bot, sink_ref[kvh * A_GROUP + 2 + half], s))
    s_all = jnp.concatenate(scores, axis=0)
    p_all = jnp.exp(s_all - jnp.max(s_all, axis=-1, keepdims=True))
    inv = 1.0 / jnp.sum(p_all, axis=-1, keepdims=True)
    p_all = p_all.astype(BF16)
    outs = [_dot(p_all[c * 2 * w:(c + 1) * 2 * w], values[c]) * inv[c * 2 * w:(c + 1) * 2 * w]
            for c in range(len(values))]
    for p in range(A_KV_HEADS // 2):
        o_a = outs[4 * p] + outs[4 * p + 1]
        o_b = outs[4 * p + 2] + outs[4 * p + 3]
        for jr in range(2):
            a = o_a[jr * w:(jr + 1) * w]
            b = o_b[jr * w:(jr + 1) * w]
            c_even = (2 * p * A_GROUP + 2 * jr) * HEAD_DIM
            c_odd = ((2 * p + 1) * A_GROUP + 2 * jr) * HEAD_DIM
            o_ref[:, c_even:c_even + tw] = jnp.where(first_o, a, b).astype(o_ref.dtype)
            o_ref[:, c_odd:c_odd + tw] = jnp.where(first_o, b, a).astype(o_ref.dtype)


def _when(cond, fn):
    if cond is not None:
        pl.when(cond)(fn)


def _attn_prompt_block(sink_ref, q_ref, k_ref, v_ref, qg_ref, kg_ref, o_ref, nk_ref, nv_ref, kprev, vprev,
                       first, last):
    w = WINDOW

    def reset():
        kprev[...] = jnp.zeros_like(kprev)
        vprev[...] = jnp.zeros_like(vprev)

    _when(first, reset)

    seg, _ = _half_mats()
    v = v_ref[...]
    qn = (_pair_rms(q_ref[...].astype(F32), qg_ref[...], seg) * HEAD_DIM ** -0.5).astype(BF16)
    kn = _pair_rms(k_ref[...].astype(F32), kg_ref[...], seg)
    knb = kn.astype(BF16)
    kcat = jnp.concatenate([kprev[...], knb], axis=0)
    vcat = jnp.concatenate([vprev[...], v], axis=0)

    i = lax.broadcasted_iota(jnp.int32, (2 * w, 2 * w), 0) % w
    j = lax.broadcasted_iota(jnp.int32, (2 * w, 2 * w), 1)
    lo = -1 if first is None else jnp.where(first, w - 1, -1)
    mask = (j > i) & (j <= i + w) & (j > lo)
    _pair_tile_attention(sink_ref, qn, kcat, vcat, mask, w, o_ref)

    kprev[...] = knb
    vprev[...] = v

    def emit():
        nk_ref[0] = kn
        nv_ref[0] = v.astype(F32)

    _when(last, emit)


ATTN_S_SEQS = 8


def _attn_sample_kernel(sink_ref, q_ref, k_ref, v_ref, kc_ref, vc_ref, qg_ref, kg_ref, o_ref, nk_ref, nv_ref):
    _, sb, wb, kvw = kc_ref.shape
    rows = q_ref.shape[0]
    l = rows // sb
    seg, _ = _half_mats()
    v = v_ref[...]
    qn = (_pair_rms(q_ref[...].astype(F32), qg_ref[...], seg) * HEAD_DIM ** -0.5).astype(BF16)
    kn = _pair_rms(k_ref[...].astype(F32), kg_ref[...], seg)
    nk_ref[...] = kn
    nv_ref[...] = v.astype(F32)
    nc = sb * wb
    kall = jnp.concatenate([kc_ref[0].reshape(nc, kvw).astype(BF16), kn.astype(BF16)], axis=0)
    vall = jnp.concatenate([vc_ref[0].reshape(nc, kvw).astype(BF16), v], axis=0)

    r = lax.broadcasted_iota(jnp.int32, (2 * rows, nc + rows), 0) % rows
    c = lax.broadcasted_iota(jnp.int32, (2 * rows, nc + rows), 1)
    cached = c < nc
    key_seq = jnp.where(cached, c // wb, (c - nc) // l)
    i = r % l
    seen = jnp.where(cached, c % wb - (wb - WINDOW), i + 1) > jnp.where(cached, i, (c - nc) % l)
    mask = (key_seq == r // l) & seen
    _pair_tile_attention(sink_ref, qn, kall, vall, mask, rows, o_ref)


def _attn_sample_call(z, kc, vc, sinks, qg, kg, batch, l, layer):
    kvw = A_KV_HEADS * HEAD_DIM
    sb = ATTN_S_SEQS
    rows = sb * l
    wb = kc.shape[2]
    assert wb == WINDOW, "key 0 of the window buffer must be out of every new token's window"
    pair_gain = lambda g: jnp.tile(g, 2).reshape(1, 2 * HEAD_DIM)
    grid_spec = pltpu.PrefetchScalarGridSpec(
        num_scalar_prefetch=1,
        grid=(batch // sb,),
        in_specs=[
            pl.BlockSpec((rows, D_MODEL), lambda i, s: (i, Z_AQ // D_MODEL)),
            pl.BlockSpec((rows, kvw), lambda i, s: (i, Z_AK // kvw)),
            pl.BlockSpec((rows, kvw), lambda i, s: (i, Z_AV // kvw)),
            pl.BlockSpec((1, sb, wb, kvw), lambda i, s: (layer, i, 0, 0)),
            pl.BlockSpec((1, sb, wb, kvw), lambda i, s: (layer, i, 0, 0)),
            pl.BlockSpec((1, 2 * HEAD_DIM), lambda i, s: (0, 0)),
            pl.BlockSpec((1, 2 * HEAD_DIM), lambda i, s: (0, 0)),
        ],
        out_specs=[
            pl.BlockSpec((rows, D_MODEL), lambda i, s: (i, 0)),
            pl.BlockSpec((rows, kvw), lambda i, s: (i, 0)),
            pl.BlockSpec((rows, kvw), lambda i, s: (i, 0)),
        ],
    )
    return pl.pallas_call(
        _attn_sample_kernel,
        grid_spec=grid_spec,
        out_shape=[
            jax.ShapeDtypeStruct((batch * l, D_MODEL), BF16),
            jax.ShapeDtypeStruct((batch * l, kvw), F32),
            jax.ShapeDtypeStruct((batch * l, kvw), F32),
        ],
        compiler_params=_cparams(("arbitrary",)),
        name="attn_sample",
    )(sinks, z, z, z, kc, vc, pair_gain(qg), pair_gain(kg))


def _ret_tables(chunk, pos0, length):
    h = np.arange(R_HEADS, dtype=np.float64)
    log_gamma = np.log1p(-np.exp2(-5.0 - h))
    idx = np.arange(chunk, dtype=np.float64)
    diff = idx[:, None] - idx[None, :]
    intra = np.where(diff[None] >= 0, np.exp(np.maximum(diff, 0.0)[None] * log_gamma[:, None, None]), 0.0)
    q_decay = np.exp((idx + 1.0)[:, None] * log_gamma[None, :])
    k_decay = np.exp((chunk - 1.0 - idx)[:, None] * log_gamma[None, :])
    c_decay = np.exp(chunk * log_gamma)
    qd = np.repeat(q_decay, R_DK, axis=1)
    kd = np.repeat(k_decay, R_DK, axis=1)
    inv_freq = ROPE_BASE ** (-np.arange(0, R_DK, 2, dtype=np.float64) / R_DK)
    ang = (pos0 + np.arange(length, dtype=np.float64))[:, None] * inv_freq[None, :]
    cos = np.tile(np.concatenate([np.cos(ang), np.cos(ang)], axis=1), (1, R_HEADS))
    sin = np.tile(np.concatenate([-np.sin(ang), np.sin(ang)], axis=1), (1, R_HEADS))
    f = lambda a: jnp.asarray(a, F32)
    return f(intra), f(qd), f(kd), [float(c) for c in c_decay], f(cos), f(sin)


def _rope(x, cos, sin):
    n = x.shape[-1]
    half = R_DK // 2
    lane = lax.broadcasted_iota(jnp.int32, x.shape, 1)
    up = pltpu.roll(x, n - half, axis=1)
    dn = pltpu.roll(x, half, axis=1)
    partner = jnp.where(lane % R_DK < half, up, dn)
    return x * cos + partner * sin


def _rope_mxu(x, cos, sin):
    tw = 2 * R_DK
    half = R_DK // 2
    r = lax.broadcasted_iota(jnp.int32, (tw, tw), 0)
    c = lax.broadcasted_iota(jnp.int32, (tw, tw), 1)
    perm = jnp.where((r // R_DK == c // R_DK) & ((r + half) % R_DK == c % R_DK), 1.0, 0.0).astype(BF16)
    parts = []
    for t in range(x.shape[1] // tw):
        hi, lo = _split_bf16(x[:, t * tw:(t + 1) * tw])
        parts.append(_dot(hi, perm) + _dot(lo, perm))
    return x * cos + jnp.concatenate(parts, axis=-1) * sin


def _head_ln(o, g):
    mu = jnp.mean(o, axis=-1, keepdims=True)
    oc = o - mu
    var = jnp.mean(oc * oc, axis=-1, keepdims=True)
    return oc * lax.rsqrt(var + EPS) * g


def _ret_prompt_block(q_ref, k_ref, v_ref, g_ref, cos_ref, sin_ref, intra_ref, qd_ref, kd_ref, ng_ref,
                      o_ref, s_ref, state, first, last, *, c_decay):
    def reset():
        state[...] = jnp.zeros_like(state)

    _when(first, reset)

    cos = cos_ref[...]
    sin = sin_ref[...]
    q = _rope_mxu(q_ref[...].astype(F32), cos, sin)
    k = _rope_mxu(k_ref[...].astype(F32), cos, sin) * (R_DK ** -0.5)
    qb = q.astype(BF16)
    kb = k.astype(BF16)
    qdb = (q * qd_ref[...]).astype(BF16)
    kdb = (k * kd_ref[...]).astype(BF16)
    c = q.shape[0]
    ks = [slice(h * R_DK, (h + 1) * R_DK) for h in range(R_HEADS)]
    vs = [slice(h * R_DV, (h + 1) * R_DV) for h in range(R_HEADS)]
    a = jnp.concatenate([_dot_nt(qb[:, ks[h]], kb[:, ks[h]]) for h in range(R_HEADS)], axis=0)
    ab = (a * intra_ref[...].reshape(R_HEADS * c, c)).astype(BF16)
    outs = []
    for h in range(R_HEADS):
        vh = v_ref[:, vs[h]]
        s_old = state[h]
        outs.append(_dot(ab[h * c:(h + 1) * c], vh) + _dot(qdb[:, ks[h]], s_old.astype(BF16)))
        state[h] = s_old * c_decay[h] + _dot_tn(kdb[:, ks[h]], vh)
    y = _head_ln(jnp.concatenate(outs, axis=0), 1.0)
    gate = g_ref[...].astype(F32) * ng_ref[...]
    for h in range(R_HEADS):
        o_ref[:, vs[h]] = (y[h * c:(h + 1) * c] * gate[:, vs[h]]).astype(o_ref.dtype)

    def emit():
        s_ref[0] = state[...]

    _when(last, emit)


MIXER_BLOCKS = 4


def _mixer_prompt_kernel(sink_ref, q_ref, k_ref, v_ref, qg_ref, kg_ref,
                         rq_ref, rk_ref, rv_ref, rg_ref, cos_ref, sin_ref, intra_ref, qd_ref, kd_ref, ng_ref,
                         oa_ref, nk_ref, nv_ref, ob_ref, s_ref, kprev, vprev, state, *, c_decay):
    n = pl.program_id(1)
    c = WINDOW
    for h in range(MIXER_BLOCKS):
        rows = pl.ds(h * c, c)
        first = (n == 0) if h == 0 else None
        last = (n == pl.num_programs(1) - 1) if h == MIXER_BLOCKS - 1 else None
        _attn_prompt_block(sink_ref, q_ref.at[rows], k_ref.at[rows], v_ref.at[rows], qg_ref, kg_ref,
                           oa_ref.at[rows], nk_ref, nv_ref, kprev, vprev, first, last)
        _ret_prompt_block(rq_ref.at[rows], rk_ref.at[rows], rv_ref.at[rows], rg_ref.at[rows], cos_ref.at[rows],
                          sin_ref.at[rows], intra_ref, qd_ref, kd_ref, ng_ref, ob_ref.at[rows], s_ref, state,
                          first, last, c_decay=c_decay)


def _mixer_prompt_call(z, sinks, qg, kg, ret_norm_g, batch, seq):
    assert WINDOW == R_CHUNK
    w = WINDOW
    c = MIXER_BLOCKS * w
    nb = seq // c
    kvw = A_KV_HEADS * HEAD_DIM
    qkw = R_HEADS * R_DK
    intra, qd, kd, c_decay, cos, sin = _ret_tables(WINDOW, 0, seq)
    row = lambda b, n, s: b * nb + n
    zcol = lambda width, off: pl.BlockSpec((c, width), lambda b, n, s: (row(b, n, s), off // width))
    const = lambda shape: pl.BlockSpec(shape, lambda b, n, s: (0,) * len(shape))
    pos = pl.BlockSpec((c, qkw), lambda b, n, s: (n, 0))
    grid_spec = pltpu.PrefetchScalarGridSpec(
        num_scalar_prefetch=1,
        grid=(batch, nb),
        in_specs=[
            zcol(D_MODEL, Z_AQ), zcol(kvw, Z_AK), zcol(kvw, Z_AV), const((1, 2 * HEAD_DIM)), const((1, 2 * HEAD_DIM)),
            zcol(qkw, Z_RQ), zcol(qkw, Z_RK), zcol(D_MODEL, Z_RV), zcol(D_MODEL, Z_RG), pos, pos,
            const((R_HEADS, w, w)), const((w, qkw)), const((w, qkw)), const((1, D_MODEL)),
        ],
        out_specs=[
            pl.BlockSpec((c, D_MODEL), lambda b, n, s: (row(b, n, s), 0)),
            pl.BlockSpec((1, w, kvw), lambda b, n, s: (b, 0, 0)),
            pl.BlockSpec((1, w, kvw), lambda b, n, s: (b, 0, 0)),
            pl.BlockSpec((c, D_MODEL), lambda b, n, s: (row(b, n, s), 0)),
            pl.BlockSpec((1, R_HEADS, R_DK, R_DV), lambda b, n, s: (b, 0, 0, 0)),
        ],
        scratch_shapes=[pltpu.VMEM((w, kvw), BF16), pltpu.VMEM((w, kvw), BF16),
                        pltpu.VMEM((R_HEADS, R_DK, R_DV), F32)],
    )
    pair_gain = lambda g: jnp.tile(g, 2).reshape(1, 2 * HEAD_DIM)
    return pl.pallas_call(
        functools.partial(_mixer_prompt_kernel, c_decay=c_decay),
        grid_spec=grid_spec,
        out_shape=[
            jax.ShapeDtypeStruct((batch * seq, D_MODEL), BF16),
            jax.ShapeDtypeStruct((batch, w, kvw), F32),
            jax.ShapeDtypeStruct((batch, w, kvw), F32),
            jax.ShapeDtypeStruct((batch * seq, D_MODEL), BF16),
            jax.ShapeDtypeStruct((batch, R_HEADS, R_DK, R_DV), F32),
        ],
        compiler_params=_cparams(("arbitrary", "arbitrary")),
        name="mixer_prompt",
    )(sinks, z, z, z, pair_gain(qg), pair_gain(kg), z, z, z, z, cos, sin, intra, qd, kd,
      ret_norm_g.reshape(1, D_MODEL))


RET_S_SEQS = 8


def _ret_sample_kernel(q_ref, k_ref, v_ref, g_ref, s0_ref, cos_ref, sin_ref, intra_ref, qd_ref, kd_ref, ng_ref,
                       *rest, c_decay, l):
    o_ref, s_ref = rest[-2:]
    sb = s0_ref.shape[1]
    rows = sb * l
    cos = cos_ref[...]
    sin = sin_ref[...]
    q = _rope(q_ref[...].astype(F32), cos, sin)
    k = _rope(k_ref[...].astype(F32), cos, sin) * (R_DK ** -0.5)
    qb = q.astype(BF16)
    kb = k.astype(BF16)
    qdb = (q * qd_ref[...]).astype(BF16)
    kdb = (k * kd_ref[...]).astype(BF16)
    for h in range(R_HEADS):
        ks = slice(h * R_DK, (h + 1) * R_DK)
        vs = slice(h * R_DV, (h + 1) * R_DV)
        vh = v_ref[:, vs]
        a = _dot_nt(qb[:, ks], kb[:, ks]) * intra_ref[h]
        o = _dot(a.astype(BF16), vh)
        cross, new_s = [], []
        for b in range(sb):
            rs = slice(b * l, (b + 1) * l)
            s_old = s0_ref[0, b, h]
            cross.append(_dot(qdb[rs, ks], s_old.astype(BF16)))
            s_ref[0, b, h] = s_old * c_decay[h] + _dot_tn(kdb[rs, ks], vh[rs])
        o = o + jnp.concatenate(cross, axis=0)
        y = _head_ln(o, ng_ref[:, vs]) * g_ref[:, vs].astype(F32)
        o_ref[:, vs] = y.astype(o_ref.dtype)


def _ret_sample_call(z, s0, ret_norm_g, batch, l, layer, stack):
    c = math.gcd(l, R_CHUNK)
    assert c == l, "sample step expects a single retention chunk"
    sb = RET_S_SEQS
    rows = sb * l
    intra, qd, kd, c_decay, cos, sin = _ret_tables(c, PAST_LEN, l)
    eye = jnp.eye(sb, dtype=F32)
    intra_bd = jnp.einsum("ab,hij->haibj", eye, intra).reshape(R_HEADS, rows, rows)
    tile = lambda t: jnp.tile(t, (sb, 1))
    qkw = R_HEADS * R_DK
    const2 = lambda i: (0, 0)
    return pl.pallas_call(
        functools.partial(_ret_sample_kernel, c_decay=c_decay, l=l),
        grid=(batch // sb,),
        in_specs=[
            pl.BlockSpec((rows, qkw), lambda i: (i, Z_RQ // qkw)),
            pl.BlockSpec((rows, qkw), lambda i: (i, Z_RK // qkw)),
            pl.BlockSpec((rows, D_MODEL), lambda i: (i, Z_RV // D_MODEL)),
            pl.BlockSpec((rows, D_MODEL), lambda i: (i, Z_RG // D_MODEL)),
            pl.BlockSpec((1, sb, R_HEADS, R_DK, R_DV), lambda i: (layer, i, 0, 0, 0)),
            pl.BlockSpec((rows, qkw), const2),
            pl.BlockSpec((rows, qkw), const2),
            pl.BlockSpec((R_HEADS, rows, rows), lambda i: (0, 0, 0)),
            pl.BlockSpec((rows, qkw), const2),
            pl.BlockSpec((rows, qkw), const2),
            pl.BlockSpec((1, D_MODEL), const2),
            pl.BlockSpec(memory_space=pl.ANY),
        ],
        out_specs=[
            pl.BlockSpec((rows, D_MODEL), lambda i: (i, 0)),
            pl.BlockSpec((1, sb, R_HEADS, R_DK, R_DV), lambda i: (layer, i, 0, 0, 0)),
        ],
        out_shape=[
            jax.ShapeDtypeStruct((batch * l, D_MODEL), BF16),
            jax.ShapeDtypeStruct(s0.shape, F32),
        ],
        input_output_aliases={11: 1},
        compiler_params=_cparams(("arbitrary",)),
        name="ret_sample",
    )(z, z, z, z, s0, tile(cos), tile(sin), intra_bd, tile(qd), tile(kd), ret_norm_g.reshape(1, D_MODEL), stack)


def _route_rows(logits, carry):
    tm = logits.shape[0]
    lane = lax.broadcasted_iota(jnp.int32, logits.shape, 1).astype(F32)
    work = logits
    sel = jnp.zeros(logits.shape, F32)
    vals, idxs = [], []
    for _ in range(TOP_K):
        m = jnp.max(work, axis=-1, keepdims=True)
        idx = jnp.min(jnp.where(work == m, lane, float(ROUTER_PAD)), axis=-1, keepdims=True)
        hit = lane == idx
        vals.append(m)
        idxs.append(idx)
        sel = jnp.where(hit, 1.0, sel)
        work = jnp.where(hit, -3e38, work)
    ex = [jnp.exp(v - vals[0]) for v in vals]
    den = ex[0] + ex[1] + ex[2] + ex[3]
    r = lax.broadcasted_iota(jnp.int32, (tm, tm), 0)
    c = lax.broadcasted_iota(jnp.int32, (tm, tm), 1)
    before = jnp.where(c < r, 1.0, 0.0).astype(BF16)
    rank = _dot(before, sel.astype(BF16)) + carry
    route = jnp.zeros(logits.shape, F32)
    for k in range(TOP_K):
        route = jnp.where(lane == float(k), ex[k] / den, route)
        route = jnp.where(lane == float(TOP_K + k), idxs[k], route)
        rk = jnp.sum(jnp.where(lane == idxs[k], rank, 0.0), axis=-1, keepdims=True)
        route = jnp.where(lane == float(2 * TOP_K + k), rk, route)
    return route, carry + jnp.sum(sel, axis=0, keepdims=True)


def _merge_kernel(*refs, emit_cv):
    (x_ref, oa_ref, ob_ref, cu_ref, cv_ref, ga_ref, gb_ref, gc_ref, g1_ref, sh2_ref, sc2_ref,
     lnf_ref, mix_ref, mixb_ref, wout_ref, rwh_ref, rwl_ref, rb_ref, cnt_ref) = refs[:19]
    n_out = 5 if emit_cv else 4
    outs = refs[-(n_out + 1):]
    if emit_cv:
        xo_ref, h2_ref, route_ref, cnto_ref, cvo_ref, carry = outs
    else:
        xo_ref, h2_ref, route_ref, cnto_ref, carry = outs
    first = (pl.program_id(0) == 0) & (pl.program_id(1) == 0)

    @pl.when(first)
    def _():
        carry[...] = cnt_ref[...]

    x = x_ref[...]
    gbk, rb, d = x.shape
    tm = gbk * rb
    cw = C_CHUNK
    gw = d // C_GROUPS

    cvb = cv_ref[...]
    if emit_cv:
        cvo_ref[...] = cvb.astype(F32)
    mixed_rows = []
    for c in range(tm // cw):
        rs = slice(c * cw, (c + 1) * cw)
        cols = [_dot(mix_ref[g], cvb[rs, g * gw:(g + 1) * gw]) for g in range(C_GROUPS)]
        mixed_rows.append(jnp.concatenate(cols, axis=-1) + mixb_ref[...])
    mixed = jnp.concatenate(mixed_rows, axis=0)
    oc = cu_ref[...].astype(F32) * mixed
    merged = (ga_ref[...].astype(F32) * oa_ref[...].astype(F32) + gb_ref[...].astype(F32) * ob_ref[...].astype(F32)
              + gc_ref[...].astype(F32) * oc)
    y = _dot(merged.astype(BF16), wout_ref[...])
    x = x + g1_ref[...] * y.reshape(gbk, rb, d)
    xo_ref[...] = x

    h2 = (_rms(x) * lnf_ref[...] * (1.0 + sc2_ref[...]) + sh2_ref[...]).reshape(tm, d)
    h2b = h2.astype(BF16)
    h2_ref[0], h2_ref[1] = _pack_rows(h2)
    h2l = (h2 - h2b.astype(F32)).astype(BF16)
    rwh = rwh_ref[...]
    logits = _dot(h2b, rwh) + _dot(h2l, rwh) + _dot(h2b, rwl_ref[...]) + rb_ref[...]
    route, counts = _route_rows(logits, carry[...])
    route_ref[...] = route
    carry[...] = counts
    cnto_ref[...] = counts


def _merge_call(x, z, oa, ob, mod, ln_ffn_g, mix, mixb, w_out_bf, rw_hi, rw_lo, rb,
                counts, t_all, tile_off, shared, emit_cv):
    g, r, d = x.shape
    gb, rb_ = _group_blocks(g, r)
    tm = gb * rb_
    nj = r // rb_
    t = g * r
    xspec = pl.BlockSpec((gb, rb_, d), lambda i, j: (i, j, 0))
    rows = lambda col: pl.BlockSpec((tm, d), lambda i, j: (i * nj + j, col))
    mspec = lambda col: pl.BlockSpec((gb, 1, d), lambda i, j: (i, 0, col))
    const = lambda shape: pl.BlockSpec(shape, lambda i, j: (0,) * len(shape))
    in_specs = [
        xspec, rows(0), rows(0),
        rows(Z_CU // d), rows(Z_CV // d), rows(Z_GA // d), rows(Z_GB // d), rows(Z_GC // d),
        mspec(2), mspec(3), mspec(4),
        const((1, 1, d)),
        const((C_GROUPS, C_CHUNK, C_CHUNK)), const((C_CHUNK, d)),
        const((d, d)), const((d, ROUTER_PAD)), const((d, ROUTER_PAD)), const((1, ROUTER_PAD)),
        const((1, ROUTER_PAD)),
    ]
    aliases = {len(in_specs): 1, len(in_specs) + 1: 2}
    in_specs += [pl.BlockSpec(memory_space=pl.ANY), pl.BlockSpec(memory_space=pl.ANY)]
    out_specs = [xspec, pl.BlockSpec((2, tm, PACK_W), lambda i, j: (0, tile_off + i * nj + j, 0)),
                 pl.BlockSpec((tm, ROUTER_PAD), lambda i, j: (tile_off + i * nj + j, 0)), const((1, ROUTER_PAD))]
    out_shape = [jax.ShapeDtypeStruct(x.shape, F32), jax.ShapeDtypeStruct((2, t_all, PACK_W), U32),
                 jax.ShapeDtypeStruct((t_all, ROUTER_PAD), F32), jax.ShapeDtypeStruct((1, ROUTER_PAD), F32)]
    if emit_cv:
        out_specs.append(rows(0))
        out_shape.append(jax.ShapeDtypeStruct((t, d), F32))
    return pl.pallas_call(
        functools.partial(_merge_kernel, emit_cv=emit_cv),
        grid=(g // gb, nj), in_specs=in_specs, out_specs=out_specs, out_shape=out_shape,
        scratch_shapes=[pltpu.VMEM((1, ROUTER_PAD), F32)],
        input_output_aliases=aliases,
        compiler_params=_cparams(("arbitrary", "arbitrary")),
        name="merge",
    )(x, oa, ob, z, z, z, z, z, mod, mod, mod, ln_ffn_g.reshape(1, 1, d),
      mix, mixb, w_out_bf, rw_hi, rw_lo, rb, counts, *shared)


def _moe_kernel(be_ref, nx_ref, nh_ref, sl_ref, nu_ref, x_ref, w1_hbm, b1_ref, w2_hbm, b2_ref, o_ref,
                w1s, w2s, sem, *, layer):
    i = pl.program_id(0)
    e = be_ref[i]
    prev = be_ref[jnp.maximum(i - 1, 0)]
    slot = sl_ref[i]

    def weight_copies(ex, sl):
        return (pltpu.make_async_copy(w1_hbm.at[layer, ex], w1s.at[sl], sem.at[0, sl]),
                pltpu.make_async_copy(w2_hbm.at[layer, ex], w2s.at[sl], sem.at[1, sl]))

    @pl.when(i == 0)
    def _():
        for cp in weight_copies(e, slot):
            cp.start()

    @pl.when((i == 0) | (e != prev))
    def _():
        for cp in weight_copies(e, slot):
            cp.wait()
        nxt = nx_ref[i]

        @pl.when(nxt >= 0)
        def _():
            for cp in weight_copies(nxt, 1 - slot):
                cp.start()

    def experts(rows):
        rs = slice(0, rows)
        xb = _unpack_rows(x_ref[0, rs], x_ref[1, rs])
        hdn = _dot(xb, w1s[slot]) + b1_ref[0, 0]
        g = jnp.minimum(hdn[:, :D_FF], SWIGLU_LIMIT)
        up = jnp.clip(hdn[:, D_FF:], -SWIGLU_LIMIT, SWIGLU_LIMIT)
        act = (up + 1.0) * (g * jax.nn.sigmoid(g * SWIGLU_ALPHA))
        act = act.astype(BF16).astype(F32)
        o_ref[0, rs], o_ref[1, rs] = _pack_rows(_dot(act, w2s[slot]) + b2_ref[0, 0])

    for nh in range(1, MOE_HALVES + 1):
        pl.when(nh_ref[i] == nh)(functools.partial(experts, nh * MOE_ROWS))


def _moe_call(xb, block_e, next_e, n_halves, slots, n_used, w1, b1, w2, b2, layer):
    _, n_rows, _ = xb.shape
    step_rows = MOE_HALVES * MOE_ROWS
    nblk = n_rows // step_rows
    depth, ne, d, f2 = w1.shape
    last = lambda i, be, nx, nh, sl, nu: jnp.minimum(i, nu[0] - 1)
    xspec = pl.BlockSpec((2, step_rows, PACK_W), lambda i, be, nx, nh, sl, nu: (0, last(i, be, nx, nh, sl, nu), 0))
    grid_spec = pltpu.PrefetchScalarGridSpec(
        num_scalar_prefetch=5,
        grid=(nblk,),
        in_specs=[
            xspec,
            pl.BlockSpec(memory_space=pl.ANY),
            pl.BlockSpec((1, 1, 1, f2), lambda i, be, nx, nh, sl, nu: (layer, be[i], 0, 0)),
            pl.BlockSpec(memory_space=pl.ANY),
            pl.BlockSpec((1, 1, 1, d), lambda i, be, nx, nh, sl, nu: (layer, be[i], 0, 0)),
        ],
        out_specs=xspec,
        scratch_shapes=[pltpu.VMEM((2, d, f2), F32), pltpu.VMEM((2, f2 // 2, d), F32),
                        pltpu.SemaphoreType.DMA((2, 2))],
    )
    return pl.pallas_call(
        functools.partial(_moe_kernel, layer=layer),
        grid_spec=grid_spec,
        out_shape=jax.ShapeDtypeStruct(xb.shape, U32),
        compiler_params=_cparams(("arbitrary",)),
        name="moe",
    )(block_e, next_e, n_halves, slots, n_used, xb, w1, b1.reshape(depth, ne, 1, f2), w2, b2.reshape(depth, ne, 1, d))


def _sc_mesh():
    return plsc.VectorSubcoreMesh(core_axis_name="core", subcore_axis_name="subcore")


def _sc_scatter_rows(x, idx, n_out):
    t, c = x.shape
    kk = idx.shape[0]

    @pl.kernel(out_type=jax.ShapeDtypeStruct((n_out, c), x.dtype), mesh=_sc_mesh(), scratch_types=[])
    def scatter(x_hbm, i_hbm, o_hbm):
        def body(x_vmem, i_vmem):
            for k in range(kk):
                pltpu.sync_copy(x_vmem, o_hbm.at[i_vmem.at[k]])

        pltpu.emit_pipeline(
            body,
            grid=(t // SC_WINDOW,),
            in_specs=[pl.BlockSpec((SC_WINDOW, c), index_map=lambda i: (i, 0)),
                      pl.BlockSpec((kk, SC_WINDOW), index_map=lambda i: (0, i))],
            out_specs=[],
            core_axis_name=("core", "subcore"),
            dimension_semantics=(pltpu.PARALLEL,),
        )(x_hbm, i_hbm)

    return scatter(x, idx)


def _sc_gather_rows(data, idx):
    n = idx.shape[0]
    c = data.shape[1]

    @pl.kernel(out_type=jax.ShapeDtypeStruct((n, c), data.dtype), mesh=_sc_mesh(), scratch_types=[])
    def gather(x_hbm, i_hbm, o_hbm):
        def body(i_vmem, o_vmem):
            pltpu.sync_copy(x_hbm.at[i_vmem.at[0]], o_vmem)

        pltpu.emit_pipeline(
            body,
            grid=(n // SC_WINDOW,),
            in_specs=[pl.BlockSpec((1, SC_WINDOW), index_map=lambda i: (0, i))],
            out_specs=[pl.BlockSpec((SC_WINDOW, c), index_map=lambda i: (i, 0))],
            core_axis_name=("core", "subcore"),
            dimension_semantics=(pltpu.PARALLEL,),
        )(i_hbm, o_hbm)

    return gather(data, idx.reshape(1, n))


def _plan(route, counts):
    t = route.shape[0]
    step_rows = MOE_HALVES * MOE_ROWS
    n_steps = -(-t * TOP_K // step_rows) + N_EXPERTS
    e4 = route[:, TOP_K:2 * TOP_K].astype(jnp.int32)
    r4 = route[:, 2 * TOP_K:3 * TOP_K].astype(jnp.int32)
    cnt = counts[0, :N_EXPERTS].astype(jnp.int32)
    halves = (cnt + MOE_ROWS - 1) // MOE_ROWS
    steps = (halves + MOE_HALVES - 1) // MOE_HALVES
    send = jnp.cumsum(steps)
    sstart = send - steps
    onehot = e4[:, :, None] == jnp.arange(N_EXPERTS, dtype=jnp.int32)[None, None, :]
    dest = (r4 + jnp.sum(jnp.where(onehot, (sstart * step_rows)[None, None, :], 0), axis=-1)).T
    step = jnp.arange(n_steps, dtype=jnp.int32)
    step_e = jnp.sum((step[:, None] >= send[None, :]).astype(jnp.int32), axis=-1)
    n_used = send[-1].astype(jnp.int32).reshape(1)
    step_e = jnp.minimum(step_e, jnp.sum((send < send[-1]).astype(jnp.int32))).astype(jnp.int32)
    mine = step_e[:, None] == jnp.arange(N_EXPERTS, dtype=jnp.int32)[None, :]
    pick = lambda table: jnp.sum(jnp.where(mine, table[None, :], 0), axis=-1)
    n_halves = jnp.clip(pick(halves) - MOE_HALVES * (step - pick(sstart)), 0, MOE_HALVES)
    n_halves = jnp.where(step < n_used[0], n_halves, 0).astype(jnp.int32)
    slots = pick(jnp.cumsum((steps > 0).astype(jnp.int32)) - (steps > 0).astype(jnp.int32)) % 2
    after = pick(send)
    next_e = jnp.sum((after[:, None] >= send[None, :]).astype(jnp.int32), axis=-1)
    next_e = jnp.where(after < n_used[0], next_e, -1).astype(jnp.int32)
    return dest, step_e, next_e, n_halves, slots.astype(jnp.int32), n_used, n_steps * step_rows


def _moe(h2p, route, counts, w1, b1, w2, b2, layer):
    _, t, pw = h2p.shape
    dest, step_e, next_e, n_halves, slots, n_used, n_rows = _plan(route, counts)
    idx_s = jnp.concatenate([dest, dest + n_rows], axis=1)
    xb = _sc_scatter_rows(h2p.reshape(2 * t, pw), idx_s, 2 * n_rows).reshape(2, n_rows, pw)
    yb = _moe_call(xb, step_e, next_e, n_halves, slots, n_used, w1, b1, w2, b2, layer)
    idx_g = jnp.concatenate([dest.reshape(-1), dest.reshape(-1) + n_rows])
    y4 = _sc_gather_rows(yb.reshape(2 * n_rows, pw), idx_g)
    return y4.reshape(2, TOP_K, t, pw)


def _reorder_w_in(w):
    o = _ORIG
    seg = lambda a, n: w[:, a:a + n]
    d = D_MODEL
    q_heads = sorted(((_q_col(kv, g), kv * A_GROUP + g) for kv in range(A_KV_HEADS) for g in range(A_GROUP)))
    aq = jnp.concatenate([seg(o["aq"] + h * HEAD_DIM, HEAD_DIM) for _, h in q_heads], axis=1)
    pad = jnp.zeros((d, 256), w.dtype)
    parts = [aq, seg(o["rv"], d), seg(o["rg"], d), seg(o["cu"], d), seg(o["rq"], 512), seg(o["ak"], 256), pad,
             seg(o["cv"], d), seg(o["mg"], d), seg(o["mg"] + d, d), seg(o["mg"] + 2 * d, d),
             seg(o["rk"], 512), seg(o["av"], 256), pad]
    w = jnp.concatenate(parts, axis=1).astype(BF16)
    return w.reshape(D_MODEL, IN_CHUNKS, D_IN // IN_CHUNKS).transpose(1, 0, 2)


def _gmlp_tables(ws, bs, chunk_len, rows):
    causal = jnp.tril(jnp.ones((C_CHUNK, C_CHUNK), dtype=bool))
    w = jnp.where(causal[None], ws, 0.0)[:, :chunk_len, :chunk_len]
    reps = rows // chunk_len
    eye = jnp.eye(reps, dtype=F32)
    mix = jnp.einsum("ab,gts->gatbs", eye, w).reshape(C_GROUPS, rows, rows).astype(BF16)
    b = jnp.tile(bs[:, :chunk_len].T, (reps, 1))
    mixb = jnp.repeat(b, D_MODEL // C_GROUPS, axis=1)
    return mix, mixb


def kernel(x_prompt, x_sample, c_prompt, c_sample, cache_attn_k, cache_attn_v, state_retention, ln_mix_g, ln_ffn_g, w_ada, b_ada, w_in, b_gate, q_norm_g, k_norm_g, attn_sinks, ret_norm_g, gm_ln_g, gm_ln_b, gm_ws, gm_bs, w_out, router_w, router_b, moe_w1, moe_b1, moe_w2, moe_b2):
    bp, lp, d = x_prompt.shape
    bs, ls, _ = x_sample.shape
    tp, ts = bp * lp, bs * ls
    kvw = A_KV_HEADS * HEAD_DIM
    wb = cache_attn_k.shape[2]

    mod_all = _ada_call(jnp.concatenate([c_prompt, c_sample], axis=0), w_ada, b_ada)

    cache_k = cache_attn_k.reshape(DEPTH, bs, wb, kvw)
    cache_v = cache_attn_v.reshape(DEPTH, bs, wb, kvw)
    xp, xs = x_prompt, x_sample
    y4 = None
    mod_p = mod_s = None
    s_all = jnp.zeros(state_retention.shape, F32)
    h2 = jnp.zeros((2, tp + ts, PACK_W), U32)
    route = jnp.zeros((tp + ts, ROUTER_PAD), F32)
    pk, pv, ps, sk, sv, sg = [], [], [], [], [], []
    for l in range(DEPTH):
        prev_mod_p, prev_mod_s = mod_p, mod_s
        mod_p = mod_all[l, :bp].reshape(bp, 1, 6 * d)
        mod_s = mod_all[l, bp:].reshape(bs, 1, 6 * d)
        w_in_bf = _reorder_w_in(w_in[l])
        w_out_bf = w_out[l].astype(BF16)
        rw = jnp.pad(router_w[l], ((0, 0), (0, ROUTER_PAD - N_EXPERTS)))
        rw_hi, rw_lo = _split_bf16(rw)
        rb = jnp.pad(router_b[l], (0, ROUTER_PAD - N_EXPERTS), constant_values=NEG_BIG).reshape(1, ROUTER_PAD)

        proj = (ln_mix_g[l], w_in_bf, b_gate[l], gm_ln_g[l], gm_ln_b[l])
        if l == 0:
            zp = _inproj_call(xp, mod_p, *proj)
            zs = _inproj_call(xs, mod_s, *proj)
        else:
            zp, xp = _inproj_call(xp, mod_p, *proj, moe_out=(y4, route, prev_mod_p, 0))
            zs, xs = _inproj_call(xs, mod_s, *proj, moe_out=(y4, route, prev_mod_s, tp // ROW_TILE))
        oa_p, k_p, v_p, ob_p, s_p = _mixer_prompt_call(zp, attn_sinks[l], q_norm_g[l], k_norm_g[l], ret_norm_g[l],
                                                       bp, lp)
        oa_s, k_s, v_s = _attn_sample_call(zs, cache_k, cache_v, attn_sinks[l], q_norm_g[l], k_norm_g[l], bs, ls, l)
        ob_s, s_all = _ret_sample_call(zs, state_retention, ret_norm_g[l], bs, ls, l, stack=s_all)

        mix_p, mixb_p = _gmlp_tables(gm_ws[l], gm_bs[l], C_CHUNK, C_CHUNK)
        mix_s, mixb_s = _gmlp_tables(gm_ws[l], gm_bs[l], ls, C_CHUNK)
        common = (ln_ffn_g[l],)
        tail = (w_out_bf, rw_hi, rw_lo, rb)
        zero_counts = jnp.zeros((1, ROUTER_PAD), F32)
        xp, h2, route, cnt_p = _merge_call(xp, zp, oa_p, ob_p, mod_p, *common, mix_p, mixb_p, *tail,
                                           zero_counts, tp + ts, 0, (h2, route), emit_cv=False)
        xs, h2, route, cnt, cv_s = _merge_call(xs, zs, oa_s, ob_s, mod_s, *common, mix_s, mixb_s, *tail,
                                               cnt_p, tp + ts, tp // ROW_TILE, (h2, route), emit_cv=True)
        y4 = _moe(h2, route, cnt, moe_w1, moe_b1, moe_w2, moe_b2, l)

        pk.append(k_p.reshape(bp, WINDOW, A_KV_HEADS, HEAD_DIM))
        pv.append(v_p.reshape(bp, WINDOW, A_KV_HEADS, HEAD_DIM))
        ps.append(s_p)
        sk.append(k_s.reshape(bs, ls, A_KV_HEADS, HEAD_DIM))
        sv.append(v_s.reshape(bs, ls, A_KV_HEADS, HEAD_DIM))
        sg.append(cv_s.reshape(bs, ls, d))

    xp = _resid_call(xp, y4, route, mod_p, 0)
    xs = _resid_call(xs, y4, route, mod_s, tp // ROW_TILE)
    return (xp, xs, jnp.stack(pk), jnp.stack(pv), jnp.stack(ps), jnp.stack(sk), jnp.stack(sv),
            s_all, jnp.stack(sg))
```

```python
import functools
import math

import numpy as np
import jax
import jax.numpy as jnp
from jax import lax
from jax.experimental import pallas as pl
from jax.experimental.pallas import tpu as pltpu
from jax.experimental.pallas import tpu_sc as plsc

F32 = jnp.float32
BF16 = jnp.bfloat16
U32 = jnp.uint32

D_MODEL = 1024
DEPTH = 4
PAST_LEN = 8192
HEAD_DIM = 64
A_Q_HEADS = 16
A_KV_HEADS = 4
A_GROUP = 4
WINDOW = 128
R_HEADS = 8
R_DK = 64
R_DV = 128
R_CHUNK = 128
ROPE_BASE = 10000.0
C_CHUNK = 128
C_GROUPS = 8
N_EXPERTS = 32
TOP_K = 4
D_FF = D_MODEL
SWIGLU_LIMIT = 7.0
SWIGLU_ALPHA = 1.702
EPS = 1e-6

Z_AQ, Z_RV, Z_RG, Z_CU, Z_RQ, Z_AK = 0, 1024, 2048, 3072, 4096, 4608
Z_CV, Z_GA, Z_GB, Z_GC, Z_RK, Z_AV = 5120, 6144, 7168, 8192, 9216, 9728
D_IN = 10240
IN_CHUNKS = 2
_ORIG = dict(aq=0, ak=1024, av=1280, rq=1536, rk=2048, rv=2560, rg=3584, cu=4608, cv=5632, mg=6656)

ROW_TILE = 512
MOE_ROWS = 256
MOE_HALVES = 4
ROUTER_PAD = 128
PACK_W = D_MODEL // 4
SC_WINDOW = 128
HI_MASK = 0xFFFF0000
VMEM_LIMIT = 56 * 1024 * 1024
NEG_BIG = -1e30


def _cparams(sem):
    return pltpu.CompilerParams(dimension_semantics=sem, vmem_limit_bytes=VMEM_LIMIT)


def _split_bf16(x):
    hi = x.astype(BF16)
    lo = (x - hi.astype(F32)).astype(BF16)
    return hi, lo


def _dot(a, b):
    return jnp.dot(a, b, preferred_element_type=F32)


def _dot_nt(a, b):
    return lax.dot_general(a, b, (((1,), (1,)), ((), ())), preferred_element_type=F32)


def _dot_tn(a, b):
    return lax.dot_general(a, b, (((0,), (0,)), ((), ())), preferred_element_type=F32)


def _ada_kernel(c_ref, w_ref, b_ref, o_ref):
    c = c_ref[...]
    s_hi, s_lo = _split_bf16(c * jax.nn.sigmoid(c))
    w_hi, w_lo = _split_bf16(w_ref[0])
    acc = _dot(s_hi, w_hi) + _dot(s_lo, w_hi) + _dot(s_hi, w_lo)
    o_ref[0] = acc + b_ref[0]


def _ada_call(c_all, w_ada, b_ada):
    depth, d, n = w_ada.shape
    m = c_all.shape[0]
    tn = 1024
    return pl.pallas_call(
        _ada_kernel,
        grid=(depth, n // tn),
        in_specs=[
            pl.BlockSpec((m, d), lambda l, j: (0, 0)),
            pl.BlockSpec((1, d, tn), lambda l, j: (l, 0, j)),
            pl.BlockSpec((1, 1, tn), lambda l, j: (l, 0, j)),
        ],
        out_specs=pl.BlockSpec((1, m, tn), lambda l, j: (l, 0, j)),
        out_shape=jax.ShapeDtypeStruct((depth, m, n), F32),
        compiler_params=_cparams(("arbitrary", "arbitrary")),
        name="ada",
    )(c_all, w_ada, b_ada.reshape(depth, 1, n))


def _pack_rows(y):
    bits = pltpu.bitcast(y.astype(BF16).astype(F32), U32)
    q = [bits[:, i * PACK_W:(i + 1) * PACK_W] for i in range(4)]
    return (q[0] >> 16) | q[1], (q[2] >> 16) | q[3]


def _unpack_rows(a, b):
    f = lambda w: pltpu.bitcast(w, F32)
    return jnp.concatenate([f(a << 16), f(a & jnp.uint32(HI_MASK)), f(b << 16), f(b & jnp.uint32(HI_MASK))], axis=-1)


def _combine(y4_ref, route_ref):
    route = route_ref[...]
    acc = None
    for k in range(TOP_K):
        term = route[:, k:k + 1] * _unpack_rows(y4_ref[0, k], y4_ref[1, k])
        acc = term if acc is None else acc + term
    return acc


def _rms(x):
    return x * lax.rsqrt(jnp.mean(x * x, axis=-1, keepdims=True) + EPS)


def _gelu(x):
    return jax.nn.gelu(x, approximate=True)


def _silu(x):
    return x * jax.nn.sigmoid(x)


def _inproj_kernel(*refs, has_resid):
    if has_resid:
        (x_ref, y4_ref, route_ref, g2_ref, sh_ref, sc_ref, lng_ref, w_ref, bg_ref, gmg_ref, gmb_ref,
         z_ref, xo_ref, hb) = refs
    else:
        x_ref, sh_ref, sc_ref, lng_ref, w_ref, bg_ref, gmg_ref, gmb_ref, z_ref, hb = refs
    c = pl.program_id(2)

    def gm_v(acc):
        v = _gelu(acc)
        vc = v - jnp.mean(v, axis=-1, keepdims=True)
        var = jnp.mean(vc * vc, axis=-1, keepdims=True)
        return vc * lax.rsqrt(var + EPS) * gmg_ref[...] + gmb_ref[...]

    merge_gate = lambda k: (lambda acc: jax.nn.sigmoid(acc + bg_ref[k:k + 1]))
    keep = lambda acc: acc
    segment_fns = ((keep, keep, _silu, _gelu, keep),
                   (gm_v, merge_gate(0), merge_gate(1), merge_gate(2), keep))

    @pl.when(c == 0)
    def _():
        x = x_ref[...]
        if has_resid:
            x = x + g2_ref[...] * _combine(y4_ref, route_ref).reshape(x.shape)
            xo_ref[...] = x
        h = _rms(x) * lng_ref[...] * (1.0 + sc_ref[...]) + sh_ref[...]
        hb[...] = h.reshape(hb.shape).astype(BF16)

    def project(half):
        h = hb[...]
        for s, fn in enumerate(segment_fns[half]):
            cols = slice(s * D_MODEL, (s + 1) * D_MODEL)
            z_ref[:, cols] = fn(_dot(h, w_ref[half, :, cols])).astype(BF16)

    for half in range(IN_CHUNKS):
        pl.when(c == half)(functools.partial(project, half))


def _group_blocks(g, r):
    if r >= ROW_TILE:
        return 1, ROW_TILE
    return ROW_TILE // r, r


def _moe_out_specs(tm, nj, tile_off):
    return [pl.BlockSpec((2, TOP_K, tm, PACK_W), lambda i, j, *_: (0, 0, tile_off + i * nj + j, 0)),
            pl.BlockSpec((tm, ROUTER_PAD), lambda i, j, *_: (tile_off + i * nj + j, 0))]


def _inproj_call(x, mod, ln_g, w_bf, b_gate, gm_ln_g, gm_ln_b, moe_out=None):
    g, r, d = x.shape
    gb, rb = _group_blocks(g, r)
    nj = r // rb
    nc, _, cw = w_bf.shape
    grid = (g // gb, nj, nc)
    xspec = pl.BlockSpec((gb, rb, d), lambda i, j, c: (i, j, 0))
    mspec = lambda col: pl.BlockSpec((gb, 1, d), lambda i, j, c: (i, 0, col))
    in_specs, args = [xspec], [x]
    if moe_out is not None:
        y4, route, mod_prev, tile_off = moe_out
        in_specs += _moe_out_specs(gb * rb, nj, tile_off) + [mspec(5)]
        args += [y4, route, mod_prev]
    in_specs += [mspec(0), mspec(1), pl.BlockSpec((1, 1, d), lambda i, j, c: (0, 0, 0)),
                 pl.BlockSpec((nc, d, cw), lambda i, j, c: (0, 0, 0), pipeline_mode=pl.Buffered(1)),
                 pl.BlockSpec((3, d), lambda i, j, c: (0, 0)),
                 pl.BlockSpec((1, d), lambda i, j, c: (0, 0)),
                 pl.BlockSpec((1, d), lambda i, j, c: (0, 0))]
    args += [mod, mod, ln_g.reshape(1, 1, d), w_bf, b_gate.reshape(3, d), gm_ln_g.reshape(1, d),
             gm_ln_b.reshape(1, d)]
    zspec = pl.BlockSpec((gb * rb, cw), lambda i, j, c: (i * nj + j, c))
    zshape = jax.ShapeDtypeStruct((g * r, nc * cw), BF16)
    if moe_out is not None:
        out_specs, out_shape = [zspec, xspec], [zshape, jax.ShapeDtypeStruct(x.shape, F32)]
    else:
        out_specs, out_shape = zspec, zshape
    return pl.pallas_call(
        functools.partial(_inproj_kernel, has_resid=moe_out is not None),
        grid=grid, in_specs=in_specs, out_specs=out_specs, out_shape=out_shape,
        scratch_shapes=[pltpu.VMEM((gb * rb, d), BF16)],
        compiler_params=_cparams(("arbitrary", "arbitrary", "arbitrary")),
        name="inproj",
    )(*args)


def _resid_kernel(x_ref, y4_ref, route_ref, g2_ref, o_ref):
    x = x_ref[...]
    o_ref[...] = x + g2_ref[...] * _combine(y4_ref, route_ref).reshape(x.shape)


def _resid_call(x, y4, route, mod, tile_off):
    g, r, d = x.shape
    gb, rb = _group_blocks(g, r)
    nj = r // rb
    xspec = pl.BlockSpec((gb, rb, d), lambda i, j: (i, j, 0))
    return pl.pallas_call(
        _resid_kernel,
        grid=(g // gb, nj),
        in_specs=[xspec] + _moe_out_specs(gb * rb, nj, tile_off)
        + [pl.BlockSpec((gb, 1, d), lambda i, j: (i, 0, 5))],
        out_specs=xspec,
        out_shape=jax.ShapeDtypeStruct(x.shape, F32),
        compiler_params=_cparams(("arbitrary", "arbitrary")),
        name="resid",
    )(x, y4, route, mod)


def _q_col(kvh, g):
    p, kv_odd = divmod(kvh, 2)
    j, g_odd = divmod(g, 2)
    tile = p * 4 + (kv_odd ^ g_odd) * 2 + j
    return tile * 2 * HEAD_DIM + g_odd * HEAD_DIM


def _half_mats():
    r = lax.broadcasted_iota(jnp.int32, (2 * HEAD_DIM, 2 * HEAD_DIM), 0)
    c = lax.broadcasted_iota(jnp.int32, (2 * HEAD_DIM, 2 * HEAD_DIM), 1)
    seg = jnp.where(r // HEAD_DIM == c // HEAD_DIM, 1.0, 0.0).astype(BF16)
    swap = jnp.where((r + HEAD_DIM) % (2 * HEAD_DIM) == c, 1.0, 0.0).astype(BF16)
    return seg, swap


def _pair_rms(x, g2, seg):
    outs = []
    for t in range(x.shape[1] // (2 * HEAD_DIM)):
        xt = x[:, t * 2 * HEAD_DIM:(t + 1) * 2 * HEAD_DIM]
        hi, lo = _split_bf16(xt * xt)
        ss = _dot(hi, seg) + _dot(lo, seg)
        outs.append(xt * lax.rsqrt(ss * (1.0 / HEAD_DIM) + EPS) * g2)
    return jnp.concatenate(outs, axis=-1)


def _pair_tile_attention(sink_ref, qn, kall, vall, mask, w, o_ref):
    _, swap = _half_mats()
    tw = 2 * HEAD_DIM
    nk = kall.shape[0]
    row = lax.broadcasted_iota(jnp.int32, (2 * w, nk), 0)
    col = lax.broadcasted_iota(jnp.int32, (2 * w, nk), 1)
    sink_top = (col == 0) & (row < w)
    sink_bot = (col == 0) & (row >= w)
    lane_kv = lax.broadcasted_iota(jnp.int32, (nk, tw), 1)
    key_kv = lax.broadcasted_iota(jnp.int32, (nk, tw), 0)
    first_o = lax.broadcasted_iota(jnp.int32, (w, tw), 1) < HEAD_DIM
    scores, values = [], []
    for p in range(A_KV_HEADS // 2):
        kp = kall[:, p * tw:(p + 1) * tw]
        vp = vall[:, p * tw:(p + 1) * tw]
        kv_tiles = ((kp, vp), (_dot(kp, swap).astype(BF16), _dot(vp, swap).astype(BF16)))
        for variant, (kk, vv) in enumerate(kv_tiles):
            t0 = p * 4 + variant * 2
            qs = jnp.concatenate([qn[:, t0 * tw:(t0 + 1) * tw], qn[:, (t0 + 1) * tw:(t0 + 2) * tw]], axis=0)
            for half in range(2):
                keep = (lane_kv < HEAD_DIM) if half == 0 else (lane_kv >= HEAD_DIM)
                kh = jnp.where(keep, kk, jnp.zeros_like(kk))
                values.append(jnp.where(keep & (key_kv > 0), vv, jnp.zeros_like(vv)))
                kvh = 2 * p + (half ^ variant)
                s = jnp.where(mask, _dot_nt(qs, kh), NEG_BIG)
                s = jnp.where(sink_top, sink_ref[kvh * A_GROUP + half], s)
                scores.append(jnp.where(sink_bot, sink_ref[kvh * A_GROUP + 2 + half], s))
    s_all = jnp.concatenate(scores, axis=0)
    p_all = jnp.exp(s_all - jnp.max(s_all, axis=-1, keepdims=True))
    inv = 1.0 / jnp.sum(p_all, axis=-1, keepdims=True)
    p_all = p_all.astype(BF16)
    outs = [_dot(p_all[c * 2 * w:(c + 1) * 2 * w], values[c]) * inv[c * 2 * w:(c + 1) * 2 * w]
            for c in range(len(values))]
    for p in range(A_KV_HEADS // 2):
        o_a = outs[4 * p] + outs[4 * p + 1]
        o_b = outs[4 * p + 2] + outs[4 * p + 3]
        for jr in range(2):
            a = o_a[jr * w:(jr + 1) * w]
            b = o_b[jr * w:(jr + 1) * w]
            c_even = (2 * p * A_GROUP + 2 * jr) * HEAD_DIM
            c_odd = ((2 * p + 1) * A_GROUP + 2 * jr) * HEAD_DIM
            o_ref[:, c_even:c_even + tw] = jnp.where(first_o, a, b).astype(o_ref.dtype)
            o_ref[:, c_odd:c_odd + tw] = jnp.where(first_o, b, a).astype(o_ref.dtype)


def _when(cond, fn):
    if cond is not None:
        pl.when(cond)(fn)


def _attn_prompt_block(sink_ref, q_ref, k_ref, v_ref, qg_ref, kg_ref, o_ref, nk_ref, nv_ref, kprev, vprev,
                       first, last):
    w = WINDOW

    def reset():
        kprev[...] = jnp.zeros_like(kprev)
        vprev[...] = jnp.zeros_like(vprev)

    _when(first, reset)

    seg, _ = _half_mats()
    v = v_ref[...]
    qn = (_pair_rms(q_ref[...].astype(F32), qg_ref[...], seg) * HEAD_DIM ** -0.5).astype(BF16)
    kn = _pair_rms(k_ref[...].astype(F32), kg_ref[...], seg)
    knb = kn.astype(BF16)
    kcat = jnp.concatenate([kprev[...], knb], axis=0)
    vcat = jnp.concatenate([vprev[...], v], axis=0)

    i = lax.broadcasted_iota(jnp.int32, (2 * w, 2 * w), 0) % w
    j = lax.broadcasted_iota(jnp.int32, (2 * w, 2 * w), 1)
    lo = -1 if first is None else jnp.where(first, w - 1, -1)
    mask = (j > i) & (j <= i + w) & (j > lo)
    _pair_tile_attention(sink_ref, qn, kcat, vcat, mask, w, o_ref)

    kprev[...] = knb
    vprev[...] = v

    def emit():
        nk_ref[0] = kn
        nv_ref[0] = v.astype(F32)

    _when(last, emit)


ATTN_S_SEQS = 8


def _attn_sample_kernel(sink_ref, q_ref, k_ref, v_ref, kc_ref, vc_ref, qg_ref, kg_ref, o_ref, nk_ref, nv_ref):
    _, sb, wb, kvw = kc_ref.shape
    rows = q_ref.shape[0]
    l = rows // sb
    seg, _ = _half_mats()
    v = v_ref[...]
    qn = (_pair_rms(q_ref[...].astype(F32), qg_ref[...], seg) * HEAD_DIM ** -0.5).astype(BF16)
    kn = _pair_rms(k_ref[...].astype(F32), kg_ref[...], seg)
    nk_ref[...] = kn
    nv_ref[...] = v.astype(F32)
    nc = sb * wb
    kall = jnp.concatenate([kc_ref[0].reshape(nc, kvw).astype(BF16), kn.astype(BF16)], axis=0)
    vall = jnp.concatenate([vc_ref[0].reshape(nc, kvw).astype(BF16), v], axis=0)

    r = lax.broadcasted_iota(jnp.int32, (2 * rows, nc + rows), 0) % rows
    c = lax.broadcasted_iota(jnp.int32, (2 * rows, nc + rows), 1)
    cached = c < nc
    key_seq = jnp.where(cached, c // wb, (c - nc) // l)
    i = r % l
    seen = jnp.where(cached, c % wb - (wb - WINDOW), i + 1) > jnp.where(cached, i, (c - nc) % l)
    mask = (key_seq == r // l) & seen
    _pair_tile_attention(sink_ref, qn, kall, vall, mask, rows, o_ref)


def _attn_sample_call(z, kc, vc, sinks, qg, kg, batch, l, layer):
    kvw = A_KV_HEADS * HEAD_DIM
    sb = ATTN_S_SEQS
    rows = sb * l
    wb = kc.shape[2]
    assert wb == WINDOW, "key 0 of the window buffer must be out of every new token's window"
    pair_gain = lambda g: jnp.tile(g, 2).reshape(1, 2 * HEAD_DIM)
    grid_spec = pltpu.PrefetchScalarGridSpec(
        num_scalar_prefetch=1,
        grid=(batch // sb,),
        in_specs=[
            pl.BlockSpec((rows, D_MODEL), lambda i, s: (i, Z_AQ // D_MODEL)),
            pl.BlockSpec((rows, kvw), lambda i, s: (i, Z_AK // kvw)),
            pl.BlockSpec((rows, kvw), lambda i, s: (i, Z_AV // kvw)),
            pl.BlockSpec((1, sb, wb, kvw), lambda i, s: (layer, i, 0, 0)),
            pl.BlockSpec((1, sb, wb, kvw), lambda i, s: (layer, i, 0, 0)),
            pl.BlockSpec((1, 2 * HEAD_DIM), lambda i, s: (0, 0)),
            pl.BlockSpec((1, 2 * HEAD_DIM), lambda i, s: (0, 0)),
        ],
        out_specs=[
            pl.BlockSpec((rows, D_MODEL), lambda i, s: (i, 0)),
            pl.BlockSpec((rows, kvw), lambda i, s: (i, 0)),
            pl.BlockSpec((rows, kvw), lambda i, s: (i, 0)),
        ],
    )
    return pl.pallas_call(
        _attn_sample_kernel,
        grid_spec=grid_spec,
        out_shape=[
            jax.ShapeDtypeStruct((batch * l, D_MODEL), BF16),
            jax.ShapeDtypeStruct((batch * l, kvw), F32),
            jax.ShapeDtypeStruct((batch * l, kvw), F32),
        ],
        compiler_params=_cparams(("arbitrary",)),
        name="attn_sample",
    )(sinks, z, z, z, kc, vc, pair_gain(qg), pair_gain(kg))


def _ret_tables(chunk, pos0, length):
    h = np.arange(R_HEADS, dtype=np.float64)
    log_gamma = np.log1p(-np.exp2(-5.0 - h))
    idx = np.arange(chunk, dtype=np.float64)
    diff = idx[:, None] - idx[None, :]
    intra = np.where(diff[None] >= 0, np.exp(np.maximum(diff, 0.0)[None] * log_gamma[:, None, None]), 0.0)
    q_decay = np.exp((idx + 1.0)[:, None] * log_gamma[None, :])
    k_decay = np.exp((chunk - 1.0 - idx)[:, None] * log_gamma[None, :])
    c_decay = np.exp(chunk * log_gamma)
    qd = np.repeat(q_decay, R_DK, axis=1)
    kd = np.repeat(k_decay, R_DK, axis=1)
    inv_freq = ROPE_BASE ** (-np.arange(0, R_DK, 2, dtype=np.float64) / R_DK)
    ang = (pos0 + np.arange(length, dtype=np.float64))[:, None] * inv_freq[None, :]
    cos = np.tile(np.concatenate([np.cos(ang), np.cos(ang)], axis=1), (1, R_HEADS))
    sin = np.tile(np.concatenate([-np.sin(ang), np.sin(ang)], axis=1), (1, R_HEADS))
    f = lambda a: jnp.asarray(a, F32)
    return f(intra), f(qd), f(kd), [float(c) for c in c_decay], f(cos), f(sin)


def _rope(x, cos, sin):
    n = x.shape[-1]
    half = R_DK // 2
    lane = lax.broadcasted_iota(jnp.int32, x.shape, 1)
    up = pltpu.roll(x, n - half, axis=1)
    dn = pltpu.roll(x, half, axis=1)
    partner = jnp.where(lane % R_DK < half, up, dn)
    return x * cos + partner * sin


def _rope_mxu(x, cos, sin):
    tw = 2 * R_DK
    half = R_DK // 2
    r = lax.broadcasted_iota(jnp.int32, (tw, tw), 0)
    c = lax.broadcasted_iota(jnp.int32, (tw, tw), 1)
    perm = jnp.where((r // R_DK == c // R_DK) & ((r + half) % R_DK == c % R_DK), 1.0, 0.0).astype(BF16)
    parts = []
    for t in range(x.shape[1] // tw):
        hi, lo = _split_bf16(x[:, t * tw:(t + 1) * tw])
        parts.append(_dot(hi, perm) + _dot(lo, perm))
    return x * cos + jnp.concatenate(parts, axis=-1) * sin


def _head_ln(o, g):
    mu = jnp.mean(o, axis=-1, keepdims=True)
    oc = o - mu
    var = jnp.mean(oc * oc, axis=-1, keepdims=True)
    return oc * lax.rsqrt(var + EPS) * g


def _ret_prompt_block(q_ref, k_ref, v_ref, g_ref, cos_ref, sin_ref, intra_ref, qd_ref, kd_ref, ng_ref,
                      o_ref, s_ref, state, first, last, *, c_decay):
    def reset():
        state[...] = jnp.zeros_like(state)

    _when(first, reset)

    cos = cos_ref[...]
    sin = sin_ref[...]
    q = _rope_mxu(q_ref[...].astype(F32), cos, sin)
    k = _rope_mxu(k_ref[...].astype(F32), cos, sin) * (R_DK ** -0.5)
    qb = q.astype(BF16)
    kb = k.astype(BF16)
    qdb = (q * qd_ref[...]).astype(BF16)
    kdb = (k * kd_ref[...]).astype(BF16)
    c = q.shape[0]
    ks = [slice(h * R_DK, (h + 1) * R_DK) for h in range(R_HEADS)]
    vs = [slice(h * R_DV, (h + 1) * R_DV) for h in range(R_HEADS)]
    a = jnp.concatenate([_dot_nt(qb[:, ks[h]], kb[:, ks[h]]) for h in range(R_HEADS)], axis=0)
    ab = (a * intra_ref[...].reshape(R_HEADS * c, c)).astype(BF16)
    outs = []
    for h in range(R_HEADS):
        vh = v_ref[:, vs[h]]
        s_old = state[h]
        outs.append(_dot(ab[h * c:(h + 1) * c], vh) + _dot(qdb[:, ks[h]], s_old.astype(BF16)))
        state[h] = s_old * c_decay[h] + _dot_tn(kdb[:, ks[h]], vh)
    y = _head_ln(jnp.concatenate(outs, axis=0), 1.0)
    gate = g_ref[...].astype(F32) * ng_ref[...]
    for h in range(R_HEADS):
        o_ref[:, vs[h]] = (y[h * c:(h + 1) * c] * gate[:, vs[h]]).astype(o_ref.dtype)

    def emit():
        s_ref[0] = state[...]

    _when(last, emit)


MIXER_BLOCKS = 4


def _mixer_prompt_kernel(sink_ref, q_ref, k_ref, v_ref, qg_ref, kg_ref,
                         rq_ref, rk_ref, rv_ref, rg_ref, cos_ref, sin_ref, intra_ref, qd_ref, kd_ref, ng_ref,
                         oa_ref, nk_ref, nv_ref, ob_ref, s_ref, kprev, vprev, state, *, c_decay):
    n = pl.program_id(1)
    c = WINDOW
    for h in range(MIXER_BLOCKS):
        rows = pl.ds(h * c, c)
        first = (n == 0) if h == 0 else None
        last = (n == pl.num_programs(1) - 1) if h == MIXER_BLOCKS - 1 else None
        _attn_prompt_block(sink_ref, q_ref.at[rows], k_ref.at[rows], v_ref.at[rows], qg_ref, kg_ref,
                           oa_ref.at[rows], nk_ref, nv_ref, kprev, vprev, first, last)
        _ret_prompt_block(rq_ref.at[rows], rk_ref.at[rows], rv_ref.at[rows], rg_ref.at[rows], cos_ref.at[rows],
                          sin_ref.at[rows], intra_ref, qd_ref, kd_ref, ng_ref, ob_ref.at[rows], s_ref, state,
                          first, last, c_decay=c_decay)


def _mixer_prompt_call(z, sinks, qg, kg, ret_norm_g, batch, seq):
    assert WINDOW == R_CHUNK
    w = WINDOW
    c = MIXER_BLOCKS * w
    nb = seq // c
    kvw = A_KV_HEADS * HEAD_DIM
    qkw = R_HEADS * R_DK
    intra, qd, kd, c_decay, cos, sin = _ret_tables(WINDOW, 0, seq)
    row = lambda b, n, s: b * nb + n
    zcol = lambda width, off: pl.BlockSpec((c, width), lambda b, n, s: (row(b, n, s), off // width))
    const = lambda shape: pl.BlockSpec(shape, lambda b, n, s: (0,) * len(shape))
    pos = pl.BlockSpec((c, qkw), lambda b, n, s: (n, 0))
    grid_spec = pltpu.PrefetchScalarGridSpec(
        num_scalar_prefetch=1,
        grid=(batch, nb),
        in_specs=[
            zcol(D_MODEL, Z_AQ), zcol(kvw, Z_AK), zcol(kvw, Z_AV), const((1, 2 * HEAD_DIM)), const((1, 2 * HEAD_DIM)),
            zcol(qkw, Z_RQ), zcol(qkw, Z_RK), zcol(D_MODEL, Z_RV), zcol(D_MODEL, Z_RG), pos, pos,
            const((R_HEADS, w, w)), const((w, qkw)), const((w, qkw)), const((1, D_MODEL)),
        ],
        out_specs=[
            pl.BlockSpec((c, D_MODEL), lambda b, n, s: (row(b, n, s), 0)),
            pl.BlockSpec((1, w, kvw), lambda b, n, s: (b, 0, 0)),
            pl.BlockSpec((1, w, kvw), lambda b, n, s: (b, 0, 0)),
            pl.BlockSpec((c, D_MODEL), lambda b, n, s: (row(b, n, s), 0)),
            pl.BlockSpec((1, R_HEADS, R_DK, R_DV), lambda b, n, s: (b, 0, 0, 0)),
        ],
        scratch_shapes=[pltpu.VMEM((w, kvw), BF16), pltpu.VMEM((w, kvw), BF16),
                        pltpu.VMEM((R_HEADS, R_DK, R_DV), F32)],
    )
    pair_gain = lambda g: jnp.tile(g, 2).reshape(1, 2 * HEAD_DIM)
    return pl.pallas_call(
        functools.partial(_mixer_prompt_kernel, c_decay=c_decay),
        grid_spec=grid_spec,
        out_shape=[
            jax.ShapeDtypeStruct((batch * seq, D_MODEL), BF16),
            jax.ShapeDtypeStruct((batch, w, kvw), F32),
            jax.ShapeDtypeStruct((batch, w, kvw), F32),
            jax.ShapeDtypeStruct((batch * seq, D_MODEL), BF16),
            jax.ShapeDtypeStruct((batch, R_HEADS, R_DK, R_DV), F32),
        ],
        compiler_params=_cparams(("arbitrary", "arbitrary")),
        name="mixer_prompt",
    )(sinks, z, z, z, pair_gain(qg), pair_gain(kg), z, z, z, z, cos, sin, intra, qd, kd,
      ret_norm_g.reshape(1, D_MODEL))


RET_S_SEQS = 8


def _ret_sample_kernel(q_ref, k_ref, v_ref, g_ref, s0_ref, cos_ref, sin_ref, intra_ref, qd_ref, kd_ref, ng_ref,
                       *rest, c_decay, l):
    o_ref, s_ref = rest[-2:]
    sb = s0_ref.shape[1]
    rows = sb * l
    cos = cos_ref[...]
    sin = sin_ref[...]
    q = _rope(q_ref[...].astype(F32), cos, sin)
    k = _rope(k_ref[...].astype(F32), cos, sin) * (R_DK ** -0.5)
    qb = q.astype(BF16)
    kb = k.astype(BF16)
    qdb = (q * qd_ref[...]).astype(BF16)
    kdb = (k * kd_ref[...]).astype(BF16)
    for h in range(R_HEADS):
        ks = slice(h * R_DK, (h + 1) * R_DK)
        vs = slice(h * R_DV, (h + 1) * R_DV)
        vh = v_ref[:, vs]
        a = _dot_nt(qb[:, ks], kb[:, ks]) * intra_ref[h]
        o = _dot(a.astype(BF16), vh)
        cross, new_s = [], []
        for b in range(sb):
            rs = slice(b * l, (b + 1) * l)
            s_old = s0_ref[0, b, h]
            cross.append(_dot(qdb[rs, ks], s_old.astype(BF16)))
            s_ref[0, b, h] = s_old * c_decay[h] + _dot_tn(kdb[rs, ks], vh[rs])
        o = o + jnp.concatenate(cross, axis=0)
        y = _head_ln(o, ng_ref[:, vs]) * g_ref[:, vs].astype(F32)
        o_ref[:, vs] = y.astype(o_ref.dtype)


def _ret_sample_call(z, s0, ret_norm_g, batch, l, layer, stack):
    c = math.gcd(l, R_CHUNK)
    assert c == l, "sample step expects a single retention chunk"
    sb = RET_S_SEQS
    rows = sb * l
    intra, qd, kd, c_decay, cos, sin = _ret_tables(c, PAST_LEN, l)
    eye = jnp.eye(sb, dtype=F32)
    intra_bd = jnp.einsum("ab,hij->haibj", eye, intra).reshape(R_HEADS, rows, rows)
    tile = lambda t: jnp.tile(t, (sb, 1))
    qkw = R_HEADS * R_DK
    const2 = lambda i: (0, 0)
    return pl.pallas_call(
        functools.partial(_ret_sample_kernel, c_decay=c_decay, l=l),
        grid=(batch // sb,),
        in_specs=[
            pl.BlockSpec((rows, qkw), lambda i: (i, Z_RQ // qkw)),
            pl.BlockSpec((rows, qkw), lambda i: (i, Z_RK // qkw)),
            pl.BlockSpec((rows, D_MODEL), lambda i: (i, Z_RV // D_MODEL)),
            pl.BlockSpec((rows, D_MODEL), lambda i: (i, Z_RG // D_MODEL)),
            pl.BlockSpec((1, sb, R_HEADS, R_DK, R_DV), lambda i: (layer, i, 0, 0, 0)),
            pl.BlockSpec((rows, qkw), const2),
            pl.BlockSpec((rows, qkw), const2),
            pl.BlockSpec((R_HEADS, rows, rows), lambda i: (0, 0, 0)),
            pl.BlockSpec((rows, qkw), const2),
            pl.BlockSpec((rows, qkw), const2),
            pl.BlockSpec((1, D_MODEL), const2),
            pl.BlockSpec(memory_space=pl.ANY),
        ],
        out_specs=[
            pl.BlockSpec((rows, D_MODEL), lambda i: (i, 0)),
            pl.BlockSpec((1, sb, R_HEADS, R_DK, R_DV), lambda i: (layer, i, 0, 0, 0)),
        ],
        out_shape=[
            jax.ShapeDtypeStruct((batch * l, D_MODEL), BF16),
            jax.ShapeDtypeStruct(s0.shape, F32),
        ],
        input_output_aliases={11: 1},
        compiler_params=_cparams(("arbitrary",)),
        name="ret_sample",
    )(z, z, z, z, s0, tile(cos), tile(sin), intra_bd, tile(qd), tile(kd), ret_norm_g.reshape(1, D_MODEL), stack)


def _route_rows(logits, carry):
    tm = logits.shape[0]
    lane = lax.broadcasted_iota(jnp.int32, logits.shape, 1).astype(F32)
    work = logits
    sel = jnp.zeros(logits.shape, F32)
    vals, idxs = [], []
    for _ in range(TOP_K):
        m = jnp.max(work, axis=-1, keepdims=True)
        idx = jnp.min(jnp.where(work == m, lane, float(ROUTER_PAD)), axis=-1, keepdims=True)
        hit = lane == idx
        vals.append(m)
        idxs.append(idx)
        sel = jnp.where(hit, 1.0, sel)
        work = jnp.where(hit, -3e38, work)
    ex = [jnp.exp(v - vals[0]) for v in vals]
    den = ex[0] + ex[1] + ex[2] + ex[3]
    r = lax.broadcasted_iota(jnp.int32, (tm, tm), 0)
    c = lax.broadcasted_iota(jnp.int32, (tm, tm), 1)
    before = jnp.where(c < r, 1.0, 0.0).astype(BF16)
    rank = _dot(before, sel.astype(BF16)) + carry
    route = jnp.zeros(logits.shape, F32)
    for k in range(TOP_K):
        route = jnp.where(lane == float(k), ex[k] / den, route)
        route = jnp.where(lane == float(TOP_K + k), idxs[k], route)
        rk = jnp.sum(jnp.where(lane == idxs[k], rank, 0.0), axis=-1, keepdims=True)
        route = jnp.where(lane == float(2 * TOP_K + k), rk, route)
    return route, carry + jnp.sum(sel, axis=0, keepdims=True)


def _merge_kernel(*refs, emit_cv):
    (x_ref, oa_ref, ob_ref, cu_ref, cv_ref, ga_ref, gb_ref, gc_ref, g1_ref, sh2_ref, sc2_ref,
     lnf_ref, mix_ref, mixb_ref, wout_ref, rwh_ref, rwl_ref, rb_ref, cnt_ref) = refs[:19]
    n_out = 5 if emit_cv else 4
    outs = refs[-(n_out + 1):]
    if emit_cv:
        xo_ref, h2_ref, route_ref, cnto_ref, cvo_ref, carry = outs
    else:
        xo_ref, h2_ref, route_ref, cnto_ref, carry = outs
    first = (pl.program_id(0) == 0) & (pl.program_id(1) == 0)

    @pl.when(first)
    def _():
        carry[...] = cnt_ref[...]

    x = x_ref[...]
    gbk, rb, d = x.shape
    tm = gbk * rb
    cw = C_CHUNK
    gw = d // C_GROUPS

    cvb = cv_ref[...]
    if emit_cv:
        cvo_ref[...] = cvb.astype(F32)
    mixed_rows = []
    for c in range(tm // cw):
        rs = slice(c * cw, (c + 1) * cw)
        cols = [_dot(mix_ref[g], cvb[rs, g * gw:(g + 1) * gw]) for g in range(C_GROUPS)]
        mixed_rows.append(jnp.concatenate(cols, axis=-1) + mixb_ref[...])
    mixed = jnp.concatenate(mixed_rows, axis=0)
    oc = cu_ref[...].astype(F32) * mixed
    merged = (ga_ref[...].astype(F32) * oa_ref[...].astype(F32) + gb_ref[...].astype(F32) * ob_ref[...].astype(F32)
              + gc_ref[...].astype(F32) * oc)
    y = _dot(merged.astype(BF16).astype(F32), wout_ref[0])
    x = x + g1_ref[...] * y.reshape(gbk, rb, d)
    xo_ref[...] = x

    h2 = (_rms(x) * lnf_ref[...] * (1.0 + sc2_ref[...]) + sh2_ref[...]).reshape(tm, d)
    h2b = h2.astype(BF16)
    h2_ref[0], h2_ref[1] = _pack_rows(h2)
    h2l = (h2 - h2b.astype(F32)).astype(BF16)
    rwh = rwh_ref[...]
    logits = _dot(h2b, rwh) + _dot(h2l, rwh) + _dot(h2b, rwl_ref[...]) + rb_ref[...]
    route, counts = _route_rows(logits, carry[...])
    route_ref[...] = route
    carry[...] = counts
    cnto_ref[...] = counts


def _merge_call(x, z, oa, ob, mod, ln_ffn_g, mix, mixb, w_out, layer, rw_hi, rw_lo, rb,
                counts, t_all, tile_off, shared, emit_cv):
    g, r, d = x.shape
    gb, rb_ = _group_blocks(g, r)
    tm = gb * rb_
    nj = r // rb_
    t = g * r
    xspec = pl.BlockSpec((gb, rb_, d), lambda i, j: (i, j, 0))
    rows = lambda col: pl.BlockSpec((tm, d), lambda i, j: (i * nj + j, col))
    mspec = lambda col: pl.BlockSpec((gb, 1, d), lambda i, j: (i, 0, col))
    const = lambda shape: pl.BlockSpec(shape, lambda i, j: (0,) * len(shape))
    in_specs = [
        xspec, rows(0), rows(0),
        rows(Z_CU // d), rows(Z_CV // d), rows(Z_GA // d), rows(Z_GB // d), rows(Z_GC // d),
        mspec(2), mspec(3), mspec(4),
        const((1, 1, d)),
        const((C_GROUPS, C_CHUNK, C_CHUNK)), const((C_CHUNK, d)),
        pl.BlockSpec((1, d, d), lambda i, j: (layer, 0, 0)),
        const((d, ROUTER_PAD)), const((d, ROUTER_PAD)), const((1, ROUTER_PAD)),
        const((1, ROUTER_PAD)),
    ]
    aliases = {len(in_specs): 1, len(in_specs) + 1: 2}
    in_specs += [pl.BlockSpec(memory_space=pl.ANY), pl.BlockSpec(memory_space=pl.ANY)]
    out_specs = [xspec, pl.BlockSpec((2, tm, PACK_W), lambda i, j: (0, tile_off + i * nj + j, 0)),
                 pl.BlockSpec((tm, ROUTER_PAD), lambda i, j: (tile_off + i * nj + j, 0)), const((1, ROUTER_PAD))]
    out_shape = [jax.ShapeDtypeStruct(x.shape, F32), jax.ShapeDtypeStruct((2, t_all, PACK_W), U32),
                 jax.ShapeDtypeStruct((t_all, ROUTER_PAD), F32), jax.ShapeDtypeStruct((1, ROUTER_PAD), F32)]
    if emit_cv:
        out_specs.append(rows(0))
        out_shape.append(jax.ShapeDtypeStruct((t, d), F32))
    return pl.pallas_call(
        functools.partial(_merge_kernel, emit_cv=emit_cv),
        grid=(g // gb, nj), in_specs=in_specs, out_specs=out_specs, out_shape=out_shape,
        scratch_shapes=[pltpu.VMEM((1, ROUTER_PAD), F32)],
        input_output_aliases=aliases,
        compiler_params=_cparams(("arbitrary", "arbitrary")),
        name="merge",
    )(x, oa, ob, z, z, z, z, z, mod, mod, mod, ln_ffn_g.reshape(1, 1, d),
      mix, mixb, w_out, rw_hi, rw_lo, rb, counts, *shared)


def _moe_kernel(be_ref, nx_ref, nh_ref, sl_ref, nu_ref, x_ref, w1_hbm, b1_ref, w2_hbm, b2_ref, o_ref,
                w1s, w2s, sem, *, layer):
    i = pl.program_id(0)
    e = be_ref[i]
    prev = be_ref[jnp.maximum(i - 1, 0)]
    slot = sl_ref[i]

    def weight_copies(ex, sl):
        return (pltpu.make_async_copy(w1_hbm.at[layer, ex], w1s.at[sl], sem.at[0, sl]),
                pltpu.make_async_copy(w2_hbm.at[layer, ex], w2s.at[sl], sem.at[1, sl]))

    @pl.when(i == 0)
    def _():
        for cp in weight_copies(e, slot):
            cp.start()

    @pl.when((i == 0) | (e != prev))
    def _():
        for cp in weight_copies(e, slot):
            cp.wait()
        nxt = nx_ref[i]

        @pl.when(nxt >= 0)
        def _():
            for cp in weight_copies(nxt, 1 - slot):
                cp.start()

    def experts(rows):
        rs = slice(0, rows)
        xb = _unpack_rows(x_ref[0, rs], x_ref[1, rs])
        hdn = _dot(xb, w1s[slot]) + b1_ref[0, 0]
        g = jnp.minimum(hdn[:, :D_FF], SWIGLU_LIMIT)
        up = jnp.clip(hdn[:, D_FF:], -SWIGLU_LIMIT, SWIGLU_LIMIT)
        act = (up + 1.0) * (g * jax.nn.sigmoid(g * SWIGLU_ALPHA))
        act = act.astype(BF16).astype(F32)
        o_ref[0, rs], o_ref[1, rs] = _pack_rows(_dot(act, w2s[slot]) + b2_ref[0, 0])

    for nh in range(1, MOE_HALVES + 1):
        pl.when(nh_ref[i] == nh)(functools.partial(experts, nh * MOE_ROWS))


def _moe_call(xb, block_e, next_e, n_halves, slots, n_used, w1, b1, w2, b2, layer):
    _, n_rows, _ = xb.shape
    step_rows = MOE_HALVES * MOE_ROWS
    nblk = n_rows // step_rows
    depth, ne, d, f2 = w1.shape
    last = lambda i, be, nx, nh, sl, nu: jnp.minimum(i, nu[0] - 1)
    xspec = pl.BlockSpec((2, step_rows, PACK_W), lambda i, be, nx, nh, sl, nu: (0, last(i, be, nx, nh, sl, nu), 0))
    grid_spec = pltpu.PrefetchScalarGridSpec(
        num_scalar_prefetch=5,
        grid=(nblk,),
        in_specs=[
            xspec,
            pl.BlockSpec(memory_space=pl.ANY),
            pl.BlockSpec((1, 1, 1, f2), lambda i, be, nx, nh, sl, nu: (layer, be[i], 0, 0)),
            pl.BlockSpec(memory_space=pl.ANY),
            pl.BlockSpec((1, 1, 1, d), lambda i, be, nx, nh, sl, nu: (layer, be[i], 0, 0)),
        ],
        out_specs=xspec,
        scratch_shapes=[pltpu.VMEM((2, d, f2), F32), pltpu.VMEM((2, f2 // 2, d), F32),
                        pltpu.SemaphoreType.DMA((2, 2))],
    )
    return pl.pallas_call(
        functools.partial(_moe_kernel, layer=layer),
        grid_spec=grid_spec,
        out_shape=jax.ShapeDtypeStruct(xb.shape, U32),
        compiler_params=_cparams(("arbitrary",)),
        name="moe",
    )(block_e, next_e, n_halves, slots, n_used, xb, w1, b1.reshape(depth, ne, 1, f2), w2, b2.reshape(depth, ne, 1, d))


def _sc_mesh():
    return plsc.VectorSubcoreMesh(core_axis_name="core", subcore_axis_name="subcore")


def _sc_scatter_rows(x, idx, n_out):
    t, c = x.shape
    kk = idx.shape[0]

    @pl.kernel(out_type=jax.ShapeDtypeStruct((n_out, c), x.dtype), mesh=_sc_mesh(), scratch_types=[])
    def scatter(x_hbm, i_hbm, o_hbm):
        def body(x_vmem, i_vmem):
            for k in range(kk):
                pltpu.sync_copy(x_vmem, o_hbm.at[i_vmem.at[k]])

        pltpu.emit_pipeline(
            body,
            grid=(t // SC_WINDOW,),
            in_specs=[pl.BlockSpec((SC_WINDOW, c), index_map=lambda i: (i, 0)),
                      pl.BlockSpec((kk, SC_WINDOW), index_map=lambda i: (0, i))],
            out_specs=[],
            core_axis_name=("core", "subcore"),
            dimension_semantics=(pltpu.PARALLEL,),
        )(x_hbm, i_hbm)

    return scatter(x, idx)


def _sc_gather_rows(data, idx):
    n = idx.shape[0]
    c = data.shape[1]

    @pl.kernel(out_type=jax.ShapeDtypeStruct((n, c), data.dtype), mesh=_sc_mesh(), scratch_types=[])
    def gather(x_hbm, i_hbm, o_hbm):
        def body(i_vmem, o_vmem):
            pltpu.sync_copy(x_hbm.at[i_vmem.at[0]], o_vmem)

        pltpu.emit_pipeline(
            body,
            grid=(n // SC_WINDOW,),
            in_specs=[pl.BlockSpec((1, SC_WINDOW), index_map=lambda i: (0, i))],
            out_specs=[pl.BlockSpec((SC_WINDOW, c), index_map=lambda i: (i, 0))],
            core_axis_name=("core", "subcore"),
            dimension_semantics=(pltpu.PARALLEL,),
        )(i_hbm, o_hbm)

    return gather(data, idx.reshape(1, n))


def _plan(route, counts):
    t = route.shape[0]
    step_rows = MOE_HALVES * MOE_ROWS
    n_steps = -(-t * TOP_K // step_rows) + N_EXPERTS
    e4 = route[:, TOP_K:2 * TOP_K].astype(jnp.int32)
    r4 = route[:, 2 * TOP_K:3 * TOP_K].astype(jnp.int32)
    cnt = counts[0, :N_EXPERTS].astype(jnp.int32)
    halves = (cnt + MOE_ROWS - 1) // MOE_ROWS
    steps = (halves + MOE_HALVES - 1) // MOE_HALVES
    send = jnp.cumsum(steps)
    sstart = send - steps
    onehot = e4[:, :, None] == jnp.arange(N_EXPERTS, dtype=jnp.int32)[None, None, :]
    dest = (r4 + jnp.sum(jnp.where(onehot, (sstart * step_rows)[None, None, :], 0), axis=-1)).T
    step = jnp.arange(n_steps, dtype=jnp.int32)
    step_e = jnp.sum((step[:, None] >= send[None, :]).astype(jnp.int32), axis=-1)
    n_used = send[-1].astype(jnp.int32).reshape(1)
    step_e = jnp.minimum(step_e, jnp.sum((send < send[-1]).astype(jnp.int32))).astype(jnp.int32)
    mine = step_e[:, None] == jnp.arange(N_EXPERTS, dtype=jnp.int32)[None, :]
    pick = lambda table: jnp.sum(jnp.where(mine, table[None, :], 0), axis=-1)
    n_halves = jnp.clip(pick(halves) - MOE_HALVES * (step - pick(sstart)), 0, MOE_HALVES)
    n_halves = jnp.where(step < n_used[0], n_halves, 0).astype(jnp.int32)
    slots = pick(jnp.cumsum((steps > 0).astype(jnp.int32)) - (steps > 0).astype(jnp.int32)) % 2
    after = pick(send)
    next_e = jnp.sum((after[:, None] >= send[None, :]).astype(jnp.int32), axis=-1)
    next_e = jnp.where(after < n_used[0], next_e, -1).astype(jnp.int32)
    return dest, step_e, next_e, n_halves, slots.astype(jnp.int32), n_used, n_steps * step_rows


def _moe(h2p, route, counts, w1, b1, w2, b2, layer):
    _, t, pw = h2p.shape
    dest, step_e, next_e, n_halves, slots, n_used, n_rows = _plan(route, counts)
    idx_s = jnp.concatenate([dest, dest + n_rows], axis=1)
    xb = _sc_scatter_rows(h2p.reshape(2 * t, pw), idx_s, 2 * n_rows).reshape(2, n_rows, pw)
    yb = _moe_call(xb, step_e, next_e, n_halves, slots, n_used, w1, b1, w2, b2, layer)
    idx_g = jnp.concatenate([dest.reshape(-1), dest.reshape(-1) + n_rows])
    y4 = _sc_gather_rows(yb.reshape(2 * n_rows, pw), idx_g)
    return y4.reshape(2, TOP_K, t, pw)


def _reorder_w_in(w):
    o = _ORIG
    seg = lambda a, n: w[:, a:a + n]
    d = D_MODEL
    q_heads = sorted(((_q_col(kv, g), kv * A_GROUP + g) for kv in range(A_KV_HEADS) for g in range(A_GROUP)))
    aq = jnp.concatenate([seg(o["aq"] + h * HEAD_DIM, HEAD_DIM) for _, h in q_heads], axis=1)
    pad = jnp.zeros((d, 256), w.dtype)
    parts = [aq, seg(o["rv"], d), seg(o["rg"], d), seg(o["cu"], d), seg(o["rq"], 512), seg(o["ak"], 256), pad,
             seg(o["cv"], d), seg(o["mg"], d), seg(o["mg"] + d, d), seg(o["mg"] + 2 * d, d),
             seg(o["rk"], 512), seg(o["av"], 256), pad]
    w = jnp.concatenate(parts, axis=1).astype(BF16)
    return w.reshape(D_MODEL, IN_CHUNKS, D_IN // IN_CHUNKS).transpose(1, 0, 2)


def _gmlp_tables(ws, bs, chunk_len, rows):
    causal = jnp.tril(jnp.ones((C_CHUNK, C_CHUNK), dtype=bool))
    w = jnp.where(causal[None], ws, 0.0)[:, :chunk_len, :chunk_len]
    reps = rows // chunk_len
    eye = jnp.eye(reps, dtype=F32)
    mix = jnp.einsum("ab,gts->gatbs", eye, w).reshape(C_GROUPS, rows, rows).astype(BF16)
    b = jnp.tile(bs[:, :chunk_len].T, (reps, 1))
    mixb = jnp.repeat(b, D_MODEL // C_GROUPS, axis=1)
    return mix, mixb


def kernel(x_prompt, x_sample, c_prompt, c_sample, cache_attn_k, cache_attn_v, state_retention, ln_mix_g, ln_ffn_g, w_ada, b_ada, w_in, b_gate, q_norm_g, k_norm_g, attn_sinks, ret_norm_g, gm_ln_g, gm_ln_b, gm_ws, gm_bs, w_out, router_w, router_b, moe_w1, moe_b1, moe_w2, moe_b2):
    bp, lp, d = x_prompt.shape
    bs, ls, _ = x_sample.shape
    tp, ts = bp * lp, bs * ls
    kvw = A_KV_HEADS * HEAD_DIM
    wb = cache_attn_k.shape[2]

    mod_all = _ada_call(jnp.concatenate([c_prompt, c_sample], axis=0), w_ada, b_ada)

    cache_k = cache_attn_k.reshape(DEPTH, bs, wb, kvw)
    cache_v = cache_attn_v.reshape(DEPTH, bs, wb, kvw)
    xp, xs = x_prompt, x_sample
    y4 = None
    mod_p = mod_s = None
    s_all = jnp.zeros(state_retention.shape, F32)
    h2 = jnp.zeros((2, tp + ts, PACK_W), U32)
    route = jnp.zeros((tp + ts, ROUTER_PAD), F32)
    pk, pv, ps, sk, sv, sg = [], [], [], [], [], []
    for l in range(DEPTH):
        prev_mod_p, prev_mod_s = mod_p, mod_s
        mod_p = mod_all[l, :bp].reshape(bp, 1, 6 * d)
        mod_s = mod_all[l, bp:].reshape(bs, 1, 6 * d)
        w_in_bf = _reorder_w_in(w_in[l])
        rw = jnp.pad(router_w[l], ((0, 0), (0, ROUTER_PAD - N_EXPERTS)))
        rw_hi, rw_lo = _split_bf16(rw)
        rb = jnp.pad(router_b[l], (0, ROUTER_PAD - N_EXPERTS), constant_values=NEG_BIG).reshape(1, ROUTER_PAD)

        proj = (ln_mix_g[l], w_in_bf, b_gate[l], gm_ln_g[l], gm_ln_b[l])
        if l == 0:
            zp = _inproj_call(xp, mod_p, *proj)
            zs = _inproj_call(xs, mod_s, *proj)
        else:
            zp, xp = _inproj_call(xp, mod_p, *proj, moe_out=(y4, route, prev_mod_p, 0))
            zs, xs = _inproj_call(xs, mod_s, *proj, moe_out=(y4, route, prev_mod_s, tp // ROW_TILE))
        oa_p, k_p, v_p, ob_p, s_p = _mixer_prompt_call(zp, attn_sinks[l], q_norm_g[l], k_norm_g[l], ret_norm_g[l],
                                                       bp, lp)
        oa_s, k_s, v_s = _attn_sample_call(zs, cache_k, cache_v, attn_sinks[l], q_norm_g[l], k_norm_g[l], bs, ls, l)
        ob_s, s_all = _ret_sample_call(zs, state_retention, ret_norm_g[l], bs, ls, l, stack=s_all)

        mix_p, mixb_p = _gmlp_tables(gm_ws[l], gm_bs[l], C_CHUNK, C_CHUNK)
        mix_s, mixb_s = _gmlp_tables(gm_ws[l], gm_bs[l], ls, C_CHUNK)
        common = (ln_ffn_g[l],)
        tail = (w_out, l, rw_hi, rw_lo, rb)
        zero_counts = jnp.zeros((1, ROUTER_PAD), F32)
        xp, h2, route, cnt_p = _merge_call(xp, zp, oa_p, ob_p, mod_p, *common, mix_p, mixb_p, *tail,
                                           zero_counts, tp + ts, 0, (h2, route), emit_cv=False)
        xs, h2, route, cnt, cv_s = _merge_call(xs, zs, oa_s, ob_s, mod_s, *common, mix_s, mixb_s, *tail,
                                               cnt_p, tp + ts, tp // ROW_TILE, (h2, route), emit_cv=True)
        y4 = _moe(h2, route, cnt, moe_w1, moe_b1, moe_w2, moe_b2, l)

        pk.append(k_p.reshape(bp, WINDOW, A_KV_HEADS, HEAD_DIM))
        pv.append(v_p.reshape(bp, WINDOW, A_KV_HEADS, HEAD_DIM))
        ps.append(s_p)
        sk.append(k_s.reshape(bs, ls, A_KV_HEADS, HEAD_DIM))
        sv.append(v_s.reshape(bs, ls, A_KV_HEADS, HEAD_DIM))
        sg.append(cv_s.reshape(bs, ls, d))

    xp = _resid_call(xp, y4, route, mod_p, 0)
    xs = _resid_call(xs, y4, route, mod_s, tp // ROW_TILE)
    return (xp, xs, jnp.stack(pk), jnp.stack(pv), jnp.stack(ps), jnp.stack(sk), jnp.stack(sv),
            s_all, jnp.stack(sg))
```
